```python
import math, functools
import jax, jax.numpy as jnp
from jax import lax
import numpy as np

D_MODEL = 1024
BATCH = 2
SEQ = 8192
DEPTH = 1
DEC_BATCH = 32
DEC_SEQ = 1
PAST_LEN = 8192
PAGE_SIZE = 128

GLA_HEADS = 4
GLA_DK = D_MODEL // 8
GLA_DV = D_MODEL // 4
GLA_GATE_RANK = 16
GLA_TAU = 16.0
GLA_CHUNK = 64
NSA_HEADS = 16
NSA_KV_GROUPS = 4
NSA_GROUP_HEADS = NSA_HEADS // NSA_KV_GROUPS
NSA_DH = D_MODEL // NSA_HEADS
CMP_LEN = 32
CMP_STRIDE = 16
CMP_RATIO = CMP_LEN // CMP_STRIDE
CMP_HIDDEN = 2 * NSA_DH
SLC_BLOCK = 64
SLC_RATIO = SLC_BLOCK // CMP_STRIDE
SLC_TOPK = 16
WINDOW = 512
NSA_QBLK = 128
FORCE_SCORE = 1.0e4
D_FF = 2816
CONV_W = 3
PLE_DIM = 256
EPS = 1e-6

kernel_name = 'hybrid_gla_nsa_convffn_decode_step'


def _in_sizes():
    kv = 2 * NSA_KV_GROUPS * NSA_DH
    return (GLA_HEADS * GLA_DK, GLA_HEADS * GLA_DK, GLA_HEADS * GLA_DV, GLA_HEADS * GLA_DV, GLA_GATE_RANK,
            NSA_HEADS * NSA_DH, kv, kv, kv, NSA_HEADS * 3, 2 * D_MODEL)


def _split_points():
    return [int(v) for v in np.cumsum(_in_sizes())[:-1]]


def rms_norm(x, g):
    xf = x.astype(jnp.float32)
    y = xf * lax.rsqrt(jnp.mean(xf * xf, axis=-1, keepdims=True) + EPS)
    return (y * g.astype(jnp.float32)).astype(x.dtype)


def masked_softmax(s, mask):
    s = jnp.where(mask, s.astype(jnp.float32), -jnp.inf)
    m = jnp.max(s, axis=-1, keepdims=True)
    m = jnp.where(jnp.isfinite(m), m, 0.0)
    e = jnp.exp(s - m)
    d = jnp.sum(e, axis=-1, keepdims=True)
    return e / jnp.where(d > 0, d, 1.0)


def gla_scan(q, k, v, log_a, s0):
    B, L, H, DK = q.shape
    C = math.gcd(L, GLA_CHUNK)
    n = L // C

    def to_chunks(a):
        return a.reshape(B, n, C, H, a.shape[-1]).transpose(1, 0, 3, 2, 4)

    causal = jnp.tril(jnp.ones((C, C), bool))

    def step(S, inp):
        qc, kc, vc, lac = inp
        b = lax.cumsum(lac, axis=2)
        o = jnp.einsum('bhtd,bhde->bhte', qc * jnp.exp(b), S)
        diff = b[:, :, :, None, :] - b[:, :, None, :, :]
        decay = jnp.exp(jnp.where(causal[:, :, None], diff, -jnp.inf))
        A = jnp.einsum('bhtd,bhsd,bhtsd->bhts', qc, kc, decay)
        o = o + jnp.einsum('bhts,bhse->bhte', A, vc)
        b_last = b[:, :, -1:, :]
        S = jnp.exp(b_last[:, :, 0, :])[..., None] * S + jnp.einsum('bhsd,bhse->bhde', kc * jnp.exp(b_last - b), vc)
        return S, o

    S, o = lax.scan(step, s0, (to_chunks(q), to_chunks(k), to_chunks(v), to_chunks(log_a)))
    o = o.transpose(1, 0, 3, 2, 4).reshape(B, L, H, v.shape[-1])
    return o, S


def gla_mixer(gq, gk, gv, gr, ga, s0, w_gate_up, b_gate, g_norm, w_o):
    B, L, _ = gq.shape
    f32 = jnp.float32
    q = gq.astype(f32).reshape(B, L, GLA_HEADS, GLA_DK) * (GLA_DK ** -0.5)
    k = gk.astype(f32).reshape(B, L, GLA_HEADS, GLA_DK)
    v = gv.astype(f32).reshape(B, L, GLA_HEADS, GLA_DV)
    log_a = jax.nn.log_sigmoid((ga @ w_gate_up + b_gate).astype(f32)).reshape(B, L, GLA_HEADS, GLA_DK) / GLA_TAU
    o, S = gla_scan(q, k, v, log_a, s0.astype(f32))
    o = o * lax.rsqrt(jnp.mean(o * o, axis=-1, keepdims=True) + EPS) * g_norm.astype(f32).reshape(GLA_HEADS, GLA_DV)
    o = o.reshape(B, L, GLA_HEADS * GLA_DV).astype(gq.dtype) * jax.nn.silu(gr)
    return o @ w_o, S


def compress(kv, w1, b1, w2, b2, pe):
    B, L = kv.shape[:2]
    n_chunk = L // CMP_STRIDE
    n_cmp = n_chunk - CMP_RATIO + 1
    chunks = kv[:, :n_chunk * CMP_STRIDE].reshape(B, n_chunk, CMP_STRIDE, NSA_KV_GROUPS, 2, NSA_DH)
    blocks = jnp.concatenate([chunks[:, j:j + n_cmp] for j in range(CMP_RATIO)], axis=2)
    blocks = blocks + pe[:, None]
    flat = blocks.transpose(0, 1, 3, 4, 2, 5).reshape(B, n_cmp, NSA_KV_GROUPS, 2, CMP_LEN * NSA_DH)
    hid = jax.nn.gelu(jnp.einsum('bngcf,cfh->bngch', flat, w1) + b1)
    return jnp.einsum('bngch,chd->bngcd', hid, w2) + b2


def block_importance(imp, n_slc):
    n_cmp = imp.shape[-1]
    left = CMP_RATIO - 1
    right = max(SLC_RATIO * (n_slc + 1) - left - n_cmp, 0)
    padded = jnp.pad(imp, [(0, 0)] * (imp.ndim - 1) + [(left, right)])
    terms = [padded[..., m - n + left: m - n + left + SLC_RATIO * n_slc: SLC_RATIO]
             for m in range(SLC_RATIO) for n in range(CMP_RATIO)]
    return sum(terms[1:], terms[0])


def nsa_core(q, t, kvc, slc_gather, n_slc, kvw, pos_w, gate):
    dt = q.dtype
    n_cmp = kvc.shape[1]
    s_c = jnp.einsum('bqgrd,bngd->bgrqn', q, kvc[..., 0, :])
    c_mask = (jnp.arange(n_cmp) * CMP_STRIDE + CMP_LEN - 1)[None, :] <= t[:, None]
    p_c = masked_softmax(s_c, c_mask)
    o_c = jnp.einsum('bgrqn,bngd->bqgrd', p_c.astype(dt), kvc[..., 1, :])
    imp = block_importance(p_c.sum(axis=2), n_slc)
    blk = jnp.arange(n_slc)[None, :]
    cur = (t // SLC_BLOCK)[:, None]
    forced = (blk == 0) | (blk == cur) | (blk == cur - 1)
    score = jnp.where(blk <= cur, jnp.where(forced, FORCE_SCORE, imp), -jnp.inf)
    top_val, top_idx = lax.top_k(score, min(SLC_TOPK, n_slc))
    tok = (top_idx[..., None] * SLC_BLOCK + jnp.arange(SLC_BLOCK)).reshape(top_idx.shape[:3] + (-1,))
    s_mask = (tok <= t[None, None, :, None]) & jnp.repeat(jnp.isfinite(top_val), SLC_BLOCK, axis=-1)
    kv_s = slc_gather(tok)
    s_s = jnp.einsum('bqgrd,bgqnd->bgrqn', q, kv_s[..., 0, :])
    p_s = masked_softmax(s_s, s_mask[:, :, None])
    o_s = jnp.einsum('bgrqn,bgqnd->bqgrd', p_s.astype(dt), kv_s[..., 1, :])
    s_w = jnp.einsum('bqgrd,bwgd->bgrqw', q, kvw[..., 0, :])
    pw = pos_w[None, :]
    w_mask = (pw <= t[:, None]) & (pw > t[:, None] - WINDOW) & (pw >= 0)
    p_w = masked_softmax(s_w, w_mask)
    o_w = jnp.einsum('bgrqw,bwgd->bqgrd', p_w.astype(dt), kvw[..., 1, :])
    g = jax.nn.sigmoid(gate.astype(jnp.float32)).astype(dt)
    return g[..., 0:1] * o_c + g[..., 1:2] * o_s + g[..., 2:3] * o_w


def nsa_prompt(q, kv_cmp, kv_slc, kv_win, gate, cmp_params):
    B, L = q.shape[:2]
    kvc = compress(kv_cmp, *cmp_params)
    n_slc = -(-L // SLC_BLOCK)
    kvw_pad = jnp.pad(kv_win, ((0, 0), (WINDOW, 0), (0, 0), (0, 0), (0, 0)))
    b_idx = jnp.arange(B)[:, None, None, None]
    g_idx = jnp.arange(NSA_KV_GROUPS)[None, :, None, None]

    def slc_gather(tok):
        return kv_slc[b_idx, jnp.minimum(tok, L - 1), g_idx]

    def block(bi):
        start = bi * NSA_QBLK
        qb = lax.dynamic_slice_in_dim(q, start, NSA_QBLK, axis=1)
        gb = lax.dynamic_slice_in_dim(gate, start, NSA_QBLK, axis=1)
        kvw = lax.dynamic_slice_in_dim(kvw_pad, start, WINDOW + NSA_QBLK, axis=1)
        t = start + jnp.arange(NSA_QBLK)
        pos_w = start - WINDOW + jnp.arange(WINDOW + NSA_QBLK)
        return nsa_core(qb, t, kvc, slc_gather, n_slc, kvw, pos_w, gb)

    o = lax.map(block, jnp.arange(L // NSA_QBLK))
    o = jnp.moveaxis(o, 0, 1).reshape(B, L, NSA_HEADS * NSA_DH)
    return o, (kv_cmp, kv_slc, kv_win[:, L - min(WINDOW, L):])


def nsa_sample(q, kv_cmp, kv_slc, kv_win, gate, cache_cmp, cache_slc, cache_win, page_table, cmp_params):
    B, S = q.shape[:2]
    past = page_table.shape[1] * PAGE_SIZE
    dt = q.dtype
    past_cmp = cache_cmp[page_table].reshape(B, past, NSA_KV_GROUPS, 2, NSA_DH).astype(dt)
    kvc = compress(jnp.concatenate([past_cmp, kv_cmp], axis=1), *cmp_params)
    n_slc = -(-(past + S) // SLC_BLOCK)
    b_idx = jnp.arange(B)[:, None, None, None]
    g_idx = jnp.arange(NSA_KV_GROUPS)[None, :, None, None]

    def slc_gather(tok):
        tp = jnp.minimum(tok, past - 1)
        page = page_table[b_idx, tp // PAGE_SIZE]
        from_pool = cache_slc[page, tp % PAGE_SIZE, g_idx].astype(dt)
        from_new = kv_slc[b_idx, jnp.clip(tok - past, 0, S - 1), g_idx]
        return jnp.where((tok < past)[..., None, None], from_pool, from_new)

    win = cache_win.shape[1]
    kvw = jnp.concatenate([cache_win.astype(dt), kv_win], axis=1)
    pos_w = past - win + jnp.arange(win + S)
    t = past + jnp.arange(S)
    o = nsa_core(q, t, kvc, slc_gather, n_slc, kvw, pos_w, gate)
    return o.reshape(B, S, NSA_HEADS * NSA_DH), (kv_cmp, kv_slc, kv_win)


def conv_ffn(h, conv0, w_up, conv_w, conv_b, w_down):
    L = h.shape[1]
    a, b = jnp.split(h @ w_up, 2, axis=-1)
    a_ext = jnp.concatenate([conv0.astype(a.dtype), a], axis=1)
    taps = [a_ext[:, j:j + L] * conv_w[j] for j in range(CONV_W)]
    c = sum(taps[1:], taps[0]) + conv_b
    y = (jax.nn.gelu(c) * b) @ w_down
    return y, a_ext[:, L:]


def layer(x, p, nsa_fn, gla_s0, conv0, w):
    B, L, _ = x.shape
    h = rms_norm(x, w['g_mix_pre'])
    gq, gk, gv, gr, ga, nq, nkc, nks, nkw, ng, mg = jnp.split(h @ w['w_in'], _split_points(), axis=-1)
    y_gla, gla_state = gla_mixer(gq, gk, gv, gr, ga, gla_s0, w['w_gla_gate_up'], w['b_gla_gate'],
                                 w['g_gla_norm'], w['w_gla_out'])
    kv_shape = (B, L, NSA_KV_GROUPS, 2, NSA_DH)
    q = nq.reshape(B, L, NSA_KV_GROUPS, NSA_GROUP_HEADS, NSA_DH) * (NSA_DH ** -0.5)
    gate = ng.reshape(B, L, NSA_KV_GROUPS, NSA_GROUP_HEADS, 3)
    o, nsa_rows = nsa_fn(q, nkc.reshape(kv_shape), nks.reshape(kv_shape), nkw.reshape(kv_shape), gate)
    y_nsa = o @ w['w_nsa_out']
    m_gla, m_nsa = jnp.split(mg, 2, axis=-1)
    mix = jax.nn.sigmoid(m_gla) * y_gla.astype(x.dtype) + jax.nn.sigmoid(m_nsa) * y_nsa
    x = x + rms_norm(mix @ w['w_out'], w['g_mix_post'])
    f, conv_state = conv_ffn(rms_norm(x, w['g_ffn_pre']), conv0, w['w_ffn_up'], w['conv_ffn_w'],
                             w['b_conv_ffn'], w['w_ffn_down'])
    x = x + rms_norm(f, w['g_ffn_post'])
    x = x + jax.nn.sigmoid(rms_norm(x, w['g_ple']) @ w['w_ple_gate']) * (p @ w['w_ple_proj'])
    return x, (nsa_rows[0], nsa_rows[1], nsa_rows[2], gla_state, conv_state)


def setup_inputs(seed: int = 0) -> dict:
    key = jax.random.key(seed)
    ks = iter(jax.random.split(key, 48))

    def nrm(shape, scale):
        return jax.random.normal(next(ks), shape, jnp.float32) * scale

    def gain(shape):
        return 1.0 + nrm(shape, 0.05)

    n_pages = PAST_LEN // PAGE_SIZE
    n_used = DEC_BATCH * n_pages
    n_pool = n_used + n_used // 4
    win_buf = min(WINDOW, PAST_LEN)
    page_table = jax.random.permutation(next(ks), n_pool)[:n_used].reshape(DEC_BATCH, n_pages).astype(jnp.int32)
    in_cols = sum(_in_sizes())
    G, DH = NSA_KV_GROUPS, NSA_DH
    return {
        'x_prompt': nrm((BATCH, SEQ, D_MODEL), 1.0),
        'x_sample': nrm((DEC_BATCH, DEC_SEQ, D_MODEL), 1.0),
        'cache_cmp_kv': nrm((DEPTH, n_pool, PAGE_SIZE, G, 2, DH), 1.0),
        'cache_slc_kv': nrm((DEPTH, n_pool, PAGE_SIZE, G, 2, DH), 1.0),
        'cache_win_kv': nrm((DEPTH, DEC_BATCH, win_buf, G, 2, DH), 1.0),
        'state_gla': nrm((DEPTH, DEC_BATCH, GLA_HEADS, GLA_DK, GLA_DV), 0.5),
        'state_ffn_conv': nrm((DEPTH, DEC_BATCH, CONV_W - 1, D_FF), 1.0),
        'page_table': page_table,
        'p_prompt': nrm((DEPTH, BATCH, SEQ, PLE_DIM), 1.0),
        'p_sample': nrm((DEPTH, DEC_BATCH, DEC_SEQ, PLE_DIM), 1.0),
        'g_mix_pre': gain((DEPTH, D_MODEL)),
        'g_mix_post': gain((DEPTH, D_MODEL)),
        'g_ffn_pre': gain((DEPTH, D_MODEL)),
        'g_ffn_post': gain((DEPTH, D_MODEL)),
        'g_ple': gain((DEPTH, D_MODEL)),
        'w_in': nrm((DEPTH, D_MODEL, in_cols), D_MODEL ** -0.5),
        'w_gla_gate_up': nrm((DEPTH, GLA_GATE_RANK, GLA_HEADS * GLA_DK), GLA_GATE_RANK ** -0.5),
        'b_gla_gate': nrm((DEPTH, GLA_HEADS * GLA_DK), 0.1),
        'g_gla_norm': gain((DEPTH, GLA_HEADS * GLA_DV)),
        'w_gla_out': nrm((DEPTH, GLA_HEADS * GLA_DV, D_MODEL), (GLA_HEADS * GLA_DV) ** -0.5),
        'w_cmp1': nrm((DEPTH, 2, CMP_LEN * DH, CMP_HIDDEN), (CMP_LEN * DH) ** -0.5),
        'b_cmp1': nrm((DEPTH, 2, CMP_HIDDEN), 0.02),
        'w_cmp2': nrm((DEPTH, 2, CMP_HIDDEN, DH), CMP_HIDDEN ** -0.5),
        'b_cmp2': nrm((DEPTH, 2, DH), 0.02),
        'pe_cmp': nrm((DEPTH, CMP_LEN, 2, DH), 0.1),
        'w_nsa_out': nrm((DEPTH, NSA_HEADS * DH, D_MODEL), (NSA_HEADS * DH) ** -0.5),
        'w_out': nrm((DEPTH, D_MODEL, D_MODEL), D_MODEL ** -0.5),
        'w_ffn_up': nrm((DEPTH, D_MODEL, 2 * D_FF), D_MODEL ** -0.5),
        'conv_ffn_w': nrm((DEPTH, CONV_W, D_FF), CONV_W ** -0.5),
        'b_conv_ffn': nrm((DEPTH, D_FF), 0.02),
        'w_ffn_down': nrm((DEPTH, D_FF, D_MODEL), D_FF ** -0.5),
        'w_ple_proj': nrm((DEPTH, PLE_DIM, D_MODEL), PLE_DIM ** -0.5),
        'w_ple_gate': nrm((DEPTH, D_MODEL, D_MODEL), D_MODEL ** -0.5),
    }


def reference(x_prompt, x_sample, cache_cmp_kv, cache_slc_kv, cache_win_kv, state_gla, state_ffn_conv,
              page_table, p_prompt, p_sample, g_mix_pre, g_mix_post, g_ffn_pre, g_ffn_post, g_ple, w_in,
              w_gla_gate_up, b_gla_gate, g_gla_norm, w_gla_out, w_cmp1, b_cmp1, w_cmp2, b_cmp2, pe_cmp,
              w_nsa_out, w_out, w_ffn_up, conv_ffn_w, b_conv_ffn, w_ffn_down, w_ple_proj, w_ple_gate):
    yp, ys = x_prompt, x_sample
    Bp, Bs = x_prompt.shape[0], x_sample.shape[0]
    cmp_p, cmp_s, slc_p, slc_s, win_p, win_s, gla_p, gla_s, conv_p, conv_s = ([] for _ in range(10))
    for i in range(DEPTH):
        w = dict(g_mix_pre=g_mix_pre[i], g_mix_post=g_mix_post[i], g_ffn_pre=g_ffn_pre[i],
                 g_ffn_post=g_ffn_post[i], g_ple=g_ple[i], w_in=w_in[i], w_gla_gate_up=w_gla_gate_up[i],
                 b_gla_gate=b_gla_gate[i], g_gla_norm=g_gla_norm[i], w_gla_out=w_gla_out[i],
                 w_nsa_out=w_nsa_out[i], w_out=w_out[i], w_ffn_up=w_ffn_up[i], conv_ffn_w=conv_ffn_w[i],
                 b_conv_ffn=b_conv_ffn[i], w_ffn_down=w_ffn_down[i], w_ple_proj=w_ple_proj[i],
                 w_ple_gate=w_ple_gate[i])
        cmp_params = (w_cmp1[i], b_cmp1[i], w_cmp2[i], b_cmp2[i], pe_cmp[i])
        prompt_nsa = functools.partial(nsa_prompt, cmp_params=cmp_params)
        gla0 = jnp.zeros((Bp, GLA_HEADS, GLA_DK, GLA_DV), jnp.float32)
        conv0 = jnp.zeros((Bp, CONV_W - 1, D_FF), x_prompt.dtype)
        yp, ep = layer(yp, p_prompt[i], prompt_nsa, gla0, conv0, w)
        sample_nsa = functools.partial(nsa_sample, cache_cmp=cache_cmp_kv[i], cache_slc=cache_slc_kv[i],
                                       cache_win=cache_win_kv[i], page_table=page_table, cmp_params=cmp_params)
        ys, es = layer(ys, p_sample[i], sample_nsa, state_gla[i], state_ffn_conv[i], w)
        cmp_p.append(ep[0]); slc_p.append(ep[1]); win_p.append(ep[2]); gla_p.append(ep[3]); conv_p.append(ep[4])
        cmp_s.append(es[0]); slc_s.append(es[1]); win_s.append(es[2]); gla_s.append(es[3]); conv_s.append(es[4])
    return (yp, ys, jnp.stack(cmp_p), jnp.stack(cmp_s), jnp.stack(slc_p), jnp.stack(slc_s),
            jnp.stack(win_p), jnp.stack(win_s), jnp.stack(gla_p), jnp.stack(gla_s),
            jnp.stack(conv_p), jnp.stack(conv_s))
```

```python
import functools

import numpy as np
import jax
import jax.numpy as jnp
from jax import lax
from jax.experimental import pallas as pl
from jax.experimental.pallas import tpu as pltpu

F32 = jnp.float32
BF16 = jnp.bfloat16

D_MODEL = 1024
PAGE_SIZE = 128
GLA_HEADS = 4
GLA_DK = 128
GLA_DV = 256
GLA_GATE_RANK = 16
GLA_TAU = 16.0
GLA_SUB = 16
NSA_HEADS = 16
NSA_KV_GROUPS = 4
NSA_GROUP_HEADS = 4
NSA_DH = 64
NSA_KV_W = 2 * NSA_KV_GROUPS * NSA_DH
CMP_LEN = 32
CMP_STRIDE = 16
CMP_HIDDEN = 128
SLC_BLOCK = 64
SLC_TOPK = 16
WINDOW = 512
FORCE_SCORE = 1.0e4
D_FF = 2816
CONV_W = 3
PLE_DIM = 256
EPS = 1e-6

V7X_VMEM_BYTES = 64 * 1024 * 1024
VMEM_LIMIT = V7X_VMEM_BYTES - 8 * 1024 * 1024

_GLA_OFF, _GLA_W = 0, 2 * GLA_HEADS * GLA_DK + 2 * GLA_HEADS * GLA_DV
_SMALL_OFF, _SMALL_W = _GLA_OFF + _GLA_W, 128
_NQ_OFF, _NQ_W = _SMALL_OFF + _SMALL_W, NSA_HEADS * NSA_DH
_NK_OFF, _NK_W = _NQ_OFF + _NQ_W, 3 * NSA_KV_W
_MG_OFF, _MG_W = _NK_OFF + _NK_W, 2 * D_MODEL
_PACK_W = _MG_OFF + _MG_W
_NG_LANE0 = GLA_GATE_RANK

_MASK_BIG = float(2.0 ** 60)


def _cparams(*sem):
    return pltpu.CompilerParams(dimension_semantics=sem, vmem_limit_bytes=VMEM_LIMIT)


def _const_spec(shape):
    nd = len(shape)
    return pl.BlockSpec(shape, lambda *_: (0,) * nd, pipeline_mode=pl.Buffered(1))


def _rms(x, g):
    return x * lax.rsqrt(jnp.mean(x * x, axis=-1, keepdims=True) + EPS) * g


def _gelu_tanh(x):
    return 0.5 * x * (1.0 + jnp.tanh(np.sqrt(2.0 / np.pi).astype(np.float32) * (x + 0.044715 * (x * x * x))))


def _sigmoid(x):
    return 1.0 / (1.0 + jnp.exp(-x))


def _dot(a, b):
    return jnp.dot(a, b, preferred_element_type=F32)


def _dot_nt(a, b):
    return lax.dot_general(a, b, (((1,), (1,)), ((), ())), preferred_element_type=F32)


def _dot_tn(a, b):
    return lax.dot_general(a, b, (((0,), (0,)), ((), ())), preferred_element_type=F32)


def _masked_softmax(s, mask):
    s = jnp.where(mask, s, -jnp.inf)
    m = jnp.max(s, axis=-1, keepdims=True)
    m = jnp.where(m > -jnp.inf, m, 0.0)
    e = jnp.exp(s - m)
    d = jnp.sum(e, axis=-1, keepdims=True)
    return e / jnp.where(d > 0, d, 1.0)


def _lane_column(x, col):
    lane = lax.broadcasted_iota(jnp.int32, x.shape, 1)
    return jnp.sum(jnp.where(lane == col, x, 0.0), axis=-1, keepdims=True)


def _pack_w_in(w_in):
    sizes = (512, 512, 1024, 1024, GLA_GATE_RANK, 1024, NSA_KV_W, NSA_KV_W, NSA_KV_W, NSA_HEADS * 3, 2 * D_MODEL)
    pts = [int(v) for v in np.cumsum(sizes)[:-1]]
    gq, gk, gv, gr, ga, nq, nkc, nks, nkw, ng, mg = jnp.split(w_in, pts, axis=1)
    small = jnp.concatenate([ga, ng, jnp.zeros((D_MODEL, _SMALL_W - GLA_GATE_RANK - NSA_HEADS * 3), w_in.dtype)], axis=1)
    return jnp.concatenate([gq, gk, gv, gr, small, nq, nkc, nks, nkw, mg], axis=1).astype(BF16)


def _proj_in_kernel(x_ref, g_ref, w_ref, gla_ref, small_ref, nq_ref, kc_ref, ks_ref, kw_ref, nkb_ref, mg_ref):
    h = _rms(x_ref[...], g_ref[...]).astype(BF16)

    def seg(off, width):
        return _dot(h, w_ref[:, off:off + width])

    gla_ref[...] = seg(_GLA_OFF, _GLA_W)
    small_ref[...] = seg(_SMALL_OFF, _SMALL_W)
    nq_ref[...] = (seg(_NQ_OFF, _NQ_W) * (NSA_DH ** -0.5)).astype(BF16)
    nk = seg(_NK_OFF, _NK_W)
    kc_ref[...] = nk[:, 0:NSA_KV_W]
    ks_ref[...] = nk[:, NSA_KV_W:2 * NSA_KV_W]
    kw_ref[...] = nk[:, 2 * NSA_KV_W:3 * NSA_KV_W]
    nkb_ref[...] = nk.astype(BF16)
    mg_ref[...] = seg(_MG_OFF, _MG_W)


def _proj_in(x, g_pre, w_pack, tm):
    m = x.shape[0]
    assert m % tm == 0
    row = lambda w: pl.BlockSpec((tm, w), lambda i: (i, 0))
    out_shape = (
        jax.ShapeDtypeStruct((m, _GLA_W), F32), jax.ShapeDtypeStruct((m, _SMALL_W), F32),
        jax.ShapeDtypeStruct((m, _NQ_W), BF16),
        jax.ShapeDtypeStruct((m, NSA_KV_W), F32), jax.ShapeDtypeStruct((m, NSA_KV_W), F32),
        jax.ShapeDtypeStruct((m, NSA_KV_W), F32),
        jax.ShapeDtypeStruct((m, _NK_W), BF16), jax.ShapeDtypeStruct((m, _MG_W), F32))
    return pl.pallas_call(
        _proj_in_kernel, out_shape=out_shape, grid=(m // tm,),
        in_specs=[row(D_MODEL), _const_spec((1, D_MODEL)), _const_spec((D_MODEL, _PACK_W))],
        out_specs=(row(_GLA_W), row(_SMALL_W), row(_NQ_W), row(NSA_KV_W), row(NSA_KV_W), row(NSA_KV_W),
                   row(_NK_W), row(_MG_W)),
        compiler_params=_cparams("parallel"), name="proj_in",
    )(x, g_pre.reshape(1, D_MODEL), w_pack)


def _log_decay(small, wg, bg):
    x = _dot(small.astype(BF16), wg) + bg
    return (jnp.minimum(x, 0.0) - jnp.log1p(jnp.exp(-jnp.abs(x)))) * (1.0 / GLA_TAU)


def _pack_w_gate(w_gate_up):
    pad = jnp.zeros((_SMALL_W - GLA_GATE_RANK, GLA_HEADS * GLA_DK), w_gate_up.dtype)
    return jnp.concatenate([w_gate_up, pad], axis=0).astype(BF16)


def _gla_out(o, gn, r):
    o = o * lax.rsqrt(jnp.mean(o * o, axis=-1, keepdims=True) + EPS) * gn
    return (o * (r * _sigmoid(r))).astype(BF16)


def _gla_chunk_kernel(q_ref, k_ref, v_ref, r_ref, small_ref, wg_ref, bg_ref, gn_ref, og_ref, st_ref,
                      s_scr, kp_scr, bp_scr, *, chunk):
    c = pl.program_id(2)
    C, SUB = chunk, GLA_SUB

    @pl.when(c == 0)
    def _():
        s_scr[...] = jnp.zeros_like(s_scr)
        kp_scr[0:SUB, :] = jnp.zeros((SUB, GLA_DK), F32)
        bp_scr[0:SUB, :] = jnp.zeros((SUB, GLA_DK), F32)

    q = q_ref[...] * (GLA_DK ** -0.5)
    k = k_ref[...]
    vb = v_ref[...].astype(BF16)
    la = _log_decay(small_ref[...], wg_ref[...], bg_ref[...])

    ri = lax.broadcasted_iota(jnp.int32, (C, C), 0)
    ci = lax.broadcasted_iota(jnp.int32, (C, C), 1)
    tri = jnp.where(ci <= ri, 1.0, 0.0).astype(BF16)
    hi = la.astype(BF16)
    r1 = la - hi.astype(F32)
    mid = r1.astype(BF16)
    lo = (r1 - mid.astype(F32)).astype(BF16)
    b = _dot(tri, hi) + _dot(tri, mid) + _dot(tri, lo)

    st = s_scr[...]
    o = _dot_nt((q * jnp.exp(b)).astype(BF16), st.astype(BF16))
    b_last = b[C - 1:C, :]
    kdec = (k * jnp.exp(b_last - b)).astype(BF16)
    s_scr[...] = st * jnp.exp(b_last) + _dot_tn(vb, kdec)

    row = lax.broadcasted_iota(jnp.int32, (C, GLA_DK), 0)
    blocks = [jnp.zeros((SUB, C), F32)]
    for i in range(1, C // SUB):
        beta = b[SUB * i - 1:SUB * i, :]
        qi = (q[SUB * i:SUB * (i + 1)] * jnp.exp(b[SUB * i:SUB * (i + 1)] - beta)).astype(BF16)
        ki = (k * jnp.exp(jnp.where(row < SUB * i, beta - b, -jnp.inf))).astype(BF16)
        blocks.append(_dot_nt(qi, ki))
    a = jnp.concatenate(blocks, axis=0)

    kp_scr[SUB:SUB + C, :] = k
    bp_scr[SUB:SUB + C, :] = b
    tmod = ri % SUB
    for j in range(SUB):
        ks = kp_scr[SUB - j:SUB - j + C, :]
        bs = bp_scr[SUB - j:SUB - j + C, :]
        aj = jnp.sum(q * ks * jnp.exp(b - bs), axis=-1, keepdims=True)
        a = a + jnp.where((ci == ri - j) & (tmod >= j), aj, 0.0)
    o = o + _dot(a.astype(BF16), vb)

    og_ref[...] = _gla_out(o, gn_ref[...], r_ref[...])

    @pl.when(c == pl.num_programs(2) - 1)
    def _():
        st_ref[0, 0] = s_scr[...].T


def _gla_prompt(gla, small, w_gate_pack, b_gate, g_norm, batch, seq, chunk=128):
    n = seq // chunk
    assert seq % chunk == 0 and chunk % GLA_SUB == 0
    H = GLA_HEADS
    rows = lambda b, h, c: b * n + c
    kern = functools.partial(_gla_chunk_kernel, chunk=chunk)
    return pl.pallas_call(
        kern,
        out_shape=(jax.ShapeDtypeStruct((batch * seq, H * GLA_DV), BF16),
                   jax.ShapeDtypeStruct((batch, H, GLA_DK, GLA_DV), F32)),
        grid=(batch, H, n),
        in_specs=[
            pl.BlockSpec((chunk, GLA_DK), lambda b, h, c: (rows(b, h, c), h)),
            pl.BlockSpec((chunk, GLA_DK), lambda b, h, c: (rows(b, h, c), H + h)),
            pl.BlockSpec((chunk, GLA_DV), lambda b, h, c: (rows(b, h, c), H + h)),
            pl.BlockSpec((chunk, GLA_DV), lambda b, h, c: (rows(b, h, c), 2 * H + h)),
            pl.BlockSpec((chunk, _SMALL_W), lambda b, h, c: (rows(b, h, c), 0)),
            pl.BlockSpec((_SMALL_W, GLA_DK), lambda b, h, c: (0, h)),
            pl.BlockSpec((1, GLA_DK), lambda b, h, c: (0, h)),
            pl.BlockSpec((1, GLA_DV), lambda b, h, c: (0, h)),
        ],
        out_specs=(pl.BlockSpec((chunk, GLA_DV), lambda b, h, c: (rows(b, h, c), h)),
                   pl.BlockSpec((1, 1, GLA_DK, GLA_DV), lambda b, h, c: (b, h, 0, 0))),
        scratch_shapes=[pltpu.VMEM((GLA_DV, GLA_DK), F32),
                        pltpu.VMEM((GLA_SUB + chunk, GLA_DK), F32),
                        pltpu.VMEM((GLA_SUB + chunk, GLA_DK), F32)],
        compiler_params=_cparams("parallel", "parallel", "arbitrary"), name="gla_chunk",
    )(gla, gla, gla, gla, small, w_gate_pack, b_gate.reshape(1, -1), g_norm.reshape(1, -1))


def _pack_w_cmp1(w1):
    w = w1.reshape(2, 2, CMP_STRIDE, NSA_DH, CMP_HIDDEN)
    per_c = jnp.concatenate([w[:, 0], w[:, 1]], axis=-1)
    z = jnp.zeros_like(per_c[0])
    return jnp.concatenate([jnp.concatenate([per_c[0], z], axis=-1),
                            jnp.concatenate([z, per_c[1]], axis=-1)], axis=1).astype(BF16)


def _compress_kernel(pt_ref, kv_hbm, wp_ref, w1_ref, pef_ref, b1_ref, w2_ref, b2_ref, out_ref,
                     buf0, buf1, buf2, buf3, hb_scr, sem, *, n_pages):
    bidx = pl.program_id(0)
    n_chunk = n_pages * (PAGE_SIZE // CMP_STRIDE)
    bufs = (buf0, buf1, buf2, buf3)
    GW = 2 * NSA_DH

    def page_copy(p, g):
        page = pt_ref[bidx, p]
        src = kv_hbm.at[pl.ds(pl.multiple_of(page * PAGE_SIZE, PAGE_SIZE), PAGE_SIZE), pl.ds(g * GW, GW)]
        return pltpu.make_async_copy(src, bufs[g].at[pl.ds(pl.multiple_of(p * PAGE_SIZE, PAGE_SIZE), PAGE_SIZE)], sem)

    def start(p, carry):
        for g in range(NSA_KV_GROUPS):
            page_copy(p, g).start()
        return carry

    def wait(p, carry):
        for g in range(NSA_KV_GROUPS):
            page_copy(p, g).wait()
        return carry

    lax.fori_loop(0, n_pages, start, 0)
    lax.fori_loop(0, n_pages, wait, 0)

    HID = CMP_HIDDEN
    hb_scr[n_chunk:n_chunk + 8, :] = jnp.zeros((8, HID), F32)
    const = [_dot(pef_ref[c].astype(BF16), w1_ref[c]) + b1_ref[c] for c in range(2)]
    for g in range(NSA_KV_GROUPS):
        acc = jnp.zeros((n_chunk, 4 * HID), F32)
        for p in range(CMP_STRIDE):
            xs = bufs[g][pl.ds(p, n_chunk, stride=CMP_STRIDE), :]
            acc = acc + _dot(xs.astype(BF16), wp_ref[p])
        for c in range(2):
            hb_scr[0:n_chunk, :] = acc[:, (2 * c + 1) * HID:(2 * c + 2) * HID]
            hid = acc[:, 2 * c * HID:(2 * c + 1) * HID] + hb_scr[pl.ds(1, n_chunk), :] + const[c]
            out_ref[0, g * 2 + c] = _dot(_gelu_tanh(hid).astype(BF16), w2_ref[c]) + b2_ref[c]


def _compress(page_table, kv_rows, wp, w1b, pe_flat, b1, w2b, b2):
    batch, n_pages = page_table.shape
    n_chunk = n_pages * (PAGE_SIZE // CMP_STRIDE)
    kern = functools.partial(_compress_kernel, n_pages=n_pages)
    grid_spec = pltpu.PrefetchScalarGridSpec(
        num_scalar_prefetch=1, grid=(batch,),
        in_specs=[pl.BlockSpec(memory_space=pl.ANY),
                  _const_spec((CMP_STRIDE, 2 * NSA_DH, 4 * CMP_HIDDEN)),
                  _const_spec((2, CMP_LEN * NSA_DH, CMP_HIDDEN)),
                  _const_spec((2, 1, CMP_LEN * NSA_DH)),
                  _const_spec((2, 1, CMP_HIDDEN)),
                  _const_spec((2, CMP_HIDDEN, NSA_DH)),
                  _const_spec((2, 1, NSA_DH))],
        out_specs=pl.BlockSpec((1, 2 * NSA_KV_GROUPS, n_chunk, NSA_DH), lambda b, pt: (b, 0, 0, 0)),
        scratch_shapes=[pltpu.VMEM((n_pages * PAGE_SIZE, 2 * NSA_DH), F32) for _ in range(NSA_KV_GROUPS)] + [
                        pltpu.VMEM((n_chunk + 8, CMP_HIDDEN), F32),
                        pltpu.SemaphoreType.DMA(())])
    return pl.pallas_call(
        kern, out_shape=jax.ShapeDtypeStruct((batch, 2 * NSA_KV_GROUPS, n_chunk, NSA_DH), F32),
        grid_spec=grid_spec, compiler_params=_cparams("arbitrary"), name="compress",
    )(page_table, kv_rows, wp, w1b, pe_flat, b1, w2b, b2)


def _permute_cmp(kvc):
    b, gc, n, d = kvc.shape
    return kvc.reshape(b, gc, n // 4, 4, d).transpose(0, 1, 3, 2, 4).reshape(b, gc, n, d)


def _group_queries(qall):
    return jnp.concatenate([qall[:, r * NSA_DH:(r + 1) * NSA_DH] for r in range(NSA_GROUP_HEADS)], axis=0)


def _block_importance(psum, n_slc):
    p0, p1, p2, p3 = (psum[:, m * n_slc:(m + 1) * n_slc] for m in range(4))
    lane = lax.broadcasted_iota(jnp.int32, p3.shape, 1)
    p3s = jnp.where(lane == 0, 0.0, pltpu.roll(p3, 1, axis=1))
    return ((((((p0 + p3s) + p1) + p0) + p2) + p1) + p3) + p2


def _top_blocks(score, count):
    lane = lax.broadcasted_iota(jnp.int32, score.shape, 1)
    n = score.shape[-1]
    sel = jnp.zeros(score.shape, F32)
    for _ in range(count):
        mx = jnp.max(score, axis=-1, keepdims=True)
        idx = jnp.min(jnp.where(score == mx, lane, n), axis=-1, keepdims=True)
        pick = (lane == idx) & (mx > -jnp.inf)
        sel = jnp.where(pick, 1.0, sel)
        score = jnp.where(pick, -jnp.inf, score)
    return sel


def _cmp_topk_kernel(q_ref, kc_ref, vc_ref, small_ref, oc_ref, sel_ref, *, tq, n_slc):
    g = pl.program_id(1)
    q0 = pl.program_id(2) * tq
    rows = NSA_GROUP_HEADS * tq
    n_cmp = 4 * n_slc
    q = _group_queries(q_ref[...])
    s = _dot_nt(q, kc_ref[0, 0].astype(BF16))
    lane = lax.broadcasted_iota(jnp.int32, (1, n_cmp), 1)
    blk_end = (4 * (lane % n_slc) + lane // n_slc) * CMP_STRIDE + (CMP_LEN - 1)
    t_rows = q0 + lax.broadcasted_iota(jnp.int32, (rows, 1), 0) % tq
    p = _masked_softmax(s, blk_end <= t_rows)
    oc = _dot(p.astype(BF16), vc_ref[0, 0].astype(BF16))

    psum = ((p[0:tq] + p[tq:2 * tq]) + p[2 * tq:3 * tq]) + p[3 * tq:4 * tq]
    imp = _block_importance(psum, n_slc)
    t = q0 + lax.broadcasted_iota(jnp.int32, (tq, 1), 0)
    cur = t // SLC_BLOCK
    blk = lax.broadcasted_iota(jnp.int32, (1, n_slc), 1)
    forced = (blk == 0) | (blk == cur) | (blk == cur - 1)
    score = jnp.where(blk <= cur, jnp.where(forced, FORCE_SCORE, imp), -jnp.inf)
    sel = _top_blocks(score, min(SLC_TOPK, n_slc))
    sel_ref[0] = (sel - 1.0).astype(BF16)

    sig = _sigmoid(small_ref[...])
    outs = []
    for r in range(NSA_GROUP_HEADS):
        gate = _lane_column(sig, _NG_LANE0 + (g * NSA_GROUP_HEADS + r) * 3)
        outs.append(gate * oc[r * tq:(r + 1) * tq])
    oc_ref[...] = jnp.concatenate(outs, axis=1)


def _cmp_topk_prompt(nq, kvc_perm, small, batch, seq, tq=128):
    G = NSA_KV_GROUPS
    n_slc = seq // SLC_BLOCK
    assert kvc_perm.shape[2] == 4 * n_slc and seq % tq == 0
    nqt = seq // tq
    gw = NSA_GROUP_HEADS * NSA_DH
    kern = functools.partial(_cmp_topk_kernel, tq=tq, n_slc=n_slc)
    return pl.pallas_call(
        kern,
        out_shape=(jax.ShapeDtypeStruct((batch * seq, NSA_HEADS * NSA_DH), F32),
                   jax.ShapeDtypeStruct((batch * G, seq, n_slc), BF16)),
        grid=(batch, G, nqt),
        in_specs=[pl.BlockSpec((tq, gw), lambda b, g, i: (b * nqt + i, g)),
                  pl.BlockSpec((1, 1, 4 * n_slc, NSA_DH), lambda b, g, i: (b, 2 * g, 0, 0)),
                  pl.BlockSpec((1, 1, 4 * n_slc, NSA_DH), lambda b, g, i: (b, 2 * g + 1, 0, 0)),
                  pl.BlockSpec((tq, _SMALL_W), lambda b, g, i: (b * nqt + i, 0))],
        out_specs=(pl.BlockSpec((tq, gw), lambda b, g, i: (b * nqt + i, g)),
                   pl.BlockSpec((1, tq, n_slc), lambda b, g, i: (b * G + g, i, 0))),
        compiler_params=_cparams("parallel", "parallel", "parallel"), name="cmp_topk",
    )(nq, kvc_perm, kvc_perm, small)


def _slc_win_kernel(q_ref, sel_ref, ks_ref, kw_ref, small_ref, oc_ref, on_ref, *, tq, tk, n_slc):
    g = pl.program_id(1)
    q0 = pl.program_id(2) * tq
    rows = NSA_GROUP_HEADS * tq
    DH = NSA_DH
    q = _group_queries(q_ref[...])
    qpad = jnp.concatenate([q, jnp.zeros((rows, DH), BF16)], axis=1)
    sel4 = jnp.concatenate([sel_ref[0]] * NSA_GROUP_HEADS, axis=0)
    qaug = jnp.concatenate([qpad, sel4], axis=1)
    t_rows = q0 + lax.broadcasted_iota(jnp.int32, (rows, 1), 0) % tq

    lane_k = lax.broadcasted_iota(jnp.int32, (tk, 2 * DH), 1)
    is_k = lane_k < DH
    tok_r = lax.broadcasted_iota(jnp.int32, (tk, n_slc), 0)
    lane_b = lax.broadcasted_iota(jnp.int32, (tk, n_slc), 1)
    tok_l = lax.broadcasted_iota(jnp.int32, (1, tk), 1)

    def tile(kt, m, acc, causal):
        k0 = pl.multiple_of(kt * tk, tk)
        kv = ks_ref[pl.ds(k0, tk), :]
        onehot = jnp.where(lane_b == (k0 + tok_r) // SLC_BLOCK, _MASK_BIG, 0.0).astype(BF16)
        kaug = jnp.concatenate([jnp.where(is_k, kv, jnp.zeros_like(kv)), onehot], axis=1)
        s = _dot_nt(qaug, kaug)
        if causal:
            s = jnp.where(k0 + tok_l <= t_rows, s, -_MASK_BIG)
        m_new = jnp.maximum(m, jnp.max(s, axis=-1, keepdims=True))
        p = jnp.exp(s - m_new)
        vaug = jnp.where(is_k, jnp.ones_like(kv), kv)
        acc = jnp.exp(m - m_new) * acc + _dot(p.astype(BF16), vaug)
        return m_new, acc

    last = (q0 + tq - 1) // tk
    m0 = jnp.full((rows, 1), -1e30, F32)
    acc0 = jnp.zeros((rows, 2 * DH), F32)
    m, acc = lax.fori_loop(0, last, lambda kt, c: tile(kt, c[0], c[1], False), (m0, acc0))
    m, acc = tile(last, m, acc, True)
    o_s = acc[:, DH:] / acc[:, 0:DH]

    wlen = WINDOW + tq
    w0 = pl.multiple_of(jnp.maximum(q0 - WINDOW, 0), tq)
    kvw = kw_ref[pl.ds(w0, wlen), :]
    lane_w = lax.broadcasted_iota(jnp.int32, (wlen, 2 * DH), 1)
    s_w = _dot_nt(qpad, jnp.where(lane_w < DH, kvw, jnp.zeros_like(kvw)))
    pos = w0 + lax.broadcasted_iota(jnp.int32, (1, wlen), 1)
    p_w = _masked_softmax(s_w, (pos <= t_rows) & (pos > t_rows - WINDOW))
    o_w = _dot(p_w.astype(BF16), kvw)[:, DH:]

    sig = _sigmoid(small_ref[...])
    oc = oc_ref[...]
    outs = []
    for r in range(NSA_GROUP_HEADS):
        lane0 = _NG_LANE0 + (g * NSA_GROUP_HEADS + r) * 3
        g_s = _lane_column(sig, lane0 + 1)
        g_w = _lane_column(sig, lane0 + 2)
        rs = slice(r * tq, (r + 1) * tq)
        outs.append(oc[:, r * DH:(r + 1) * DH] + g_s * o_s[rs] + g_w * o_w[rs])
    on_ref[...] = jnp.concatenate(outs, axis=1).astype(BF16)


def _slc_win_prompt(nq, selm, nkb, small, oc, batch, seq, tq=128, tk=256):
    G = NSA_KV_GROUPS
    n_slc = seq // SLC_BLOCK
    assert seq % tk == 0 and seq % tq == 0 and tk % tq == 0 and seq >= WINDOW + tq
    nqt = seq // tq
    gw = NSA_GROUP_HEADS * NSA_DH
    kvw = 2 * NSA_DH
    kern = functools.partial(_slc_win_kernel, tq=tq, tk=tk, n_slc=n_slc)
    return pl.pallas_call(
        kern, out_shape=jax.ShapeDtypeStruct((batch * seq, NSA_HEADS * NSA_DH), BF16),
        grid=(batch, G, nqt),
        in_specs=[pl.BlockSpec((tq, gw), lambda b, g, i: (b * nqt + i, g)),
                  pl.BlockSpec((1, tq, n_slc), lambda b, g, i: (b * G + g, i, 0)),
                  pl.BlockSpec((seq, kvw), lambda b, g, i: (b, G + g)),
                  pl.BlockSpec((seq, kvw), lambda b, g, i: (b, 2 * G + g)),
                  pl.BlockSpec((tq, _SMALL_W), lambda b, g, i: (b * nqt + i, 0)),
                  pl.BlockSpec((tq, gw), lambda b, g, i: (b * nqt + i, g))],
        out_specs=pl.BlockSpec((tq, gw), lambda b, g, i: (b * nqt + i, g)),
        compiler_params=_cparams("parallel", "parallel", "parallel"), name="slc_win",
    )(nq, selm, nkb, nkb, small, oc)


def _mix_out_kernel(og_ref, on_ref, mg_ref, x_ref, wgo_ref, wno_ref, wo_ref, gpost_ref, y_ref):
    yg = _dot(og_ref[...], wgo_ref[...])
    yn = _dot(on_ref[...], wno_ref[...])
    mg = mg_ref[...]
    mix = _sigmoid(mg[:, 0:D_MODEL]) * yg + _sigmoid(mg[:, D_MODEL:]) * yn
    z = _dot(mix.astype(BF16), wo_ref[...])
    y_ref[...] = x_ref[...] + _rms(z, gpost_ref[...])


def _mix_out(og, on, mg, x, wgo, wno, wo, g_post, tm):
    m = x.shape[0]
    row = lambda w: pl.BlockSpec((tm, w), lambda i: (i, 0))
    wspec = _const_spec((D_MODEL, D_MODEL))
    return pl.pallas_call(
        _mix_out_kernel, out_shape=jax.ShapeDtypeStruct((m, D_MODEL), F32), grid=(m // tm,),
        in_specs=[row(D_MODEL), row(D_MODEL), row(2 * D_MODEL), row(D_MODEL), wspec, wspec, wspec,
                  _const_spec((1, D_MODEL))],
        out_specs=row(D_MODEL), compiler_params=_cparams("parallel"), name="mix_out",
    )(og, on, mg, x, wgo, wno, wo, g_post.reshape(1, D_MODEL))


_FFN_HALO = 16


def _ffn_tail(x, conv, bgate, p, wdn_ref, gpost_ref, gple_ref, wpg_ref, wpp_ref):
    y = _dot((_gelu_tanh(conv) * bgate).astype(BF16), wdn_ref[...])
    x2 = x + _rms(y, gpost_ref[...])
    gate = _sigmoid(_dot(_rms(x2, gple_ref[...]).astype(BF16), wpg_ref[...]))
    return x2 + gate * _dot(p.astype(BF16), wpp_ref[...])


def _ffn_prompt_kernel(x_ref, xh_ref, p_ref, gpre_ref, wup_ref, cw_ref, cb_ref, wdn_ref, gpost_ref, gple_ref,
                       wpg_ref, wpp_ref, y_ref, alast_ref, h_scr, a_scr, *, tm, tiles_per_seq):
    H = _FFN_HALO
    x = x_ref[...]
    h_scr[0:H, :] = _rms(xh_ref[...], gpre_ref[...]).astype(BF16)
    h_scr[H:H + tm, :] = _rms(x, gpre_ref[...]).astype(BF16)
    ab = _dot(h_scr[...], wup_ref[...])
    keep = jnp.where(pl.program_id(0) % tiles_per_seq == 0, 0.0, 1.0)
    a_scr[0:H, :] = ab[0:H, 0:D_FF] * keep
    a_scr[H:H + tm, :] = ab[H:, 0:D_FF]
    cw = cw_ref[...]
    conv = (a_scr[H - 2:H - 2 + tm, :] * cw[0:1] + a_scr[H - 1:H - 1 + tm, :] * cw[1:2]
            + a_scr[H:H + tm, :] * cw[2:3]) + cb_ref[...]
    y_ref[...] = _ffn_tail(x, conv, ab[H:, D_FF:], p_ref[...], wdn_ref, gpost_ref, gple_ref, wpg_ref, wpp_ref)
    alast_ref[0] = a_scr[H + tm - 8:H + tm, :]


def _ffn_weight_specs():
    return [_const_spec((1, D_MODEL)), _const_spec((D_MODEL, 2 * D_FF)), _const_spec((CONV_W, D_FF)),
            _const_spec((1, D_FF)), _const_spec((D_FF, D_MODEL)), _const_spec((1, D_MODEL)),
            _const_spec((1, D_MODEL)), _const_spec((D_MODEL, D_MODEL)), _const_spec((PLE_DIM, D_MODEL))]


def _ffn_prompt(x, p, fw, batch, seq, tm=256):
    assert seq % tm == 0 and tm % _FFN_HALO == 0
    tps = seq // tm
    hb = tm // _FFN_HALO
    kern = functools.partial(_ffn_prompt_kernel, tm=tm, tiles_per_seq=tps)
    return pl.pallas_call(
        kern,
        out_shape=(jax.ShapeDtypeStruct((batch * seq, D_MODEL), F32), jax.ShapeDtypeStruct((batch, 8, D_FF), F32)),
        grid=(batch * tps,),
        in_specs=[pl.BlockSpec((tm, D_MODEL), lambda i: (i, 0)),
                  pl.BlockSpec((_FFN_HALO, D_MODEL), lambda i: (jnp.maximum(i * hb - 1, 0), 0)),
                  pl.BlockSpec((tm, PLE_DIM), lambda i: (i, 0))] + _ffn_weight_specs(),
        out_specs=(pl.BlockSpec((tm, D_MODEL), lambda i: (i, 0)),
                   pl.BlockSpec((1, 8, D_FF), lambda i: (i // tps, 0, 0))),
        scratch_shapes=[pltpu.VMEM((_FFN_HALO + tm, D_MODEL), BF16), pltpu.VMEM((_FFN_HALO + tm, D_FF), F32)],
        compiler_params=_cparams("arbitrary"), name="ffn_prompt",
    )(x, x, p, *fw)


def _layer_prompt(x, p, w, tm=256):
    batch, seq, _ = x.shape
    m = batch * seq
    x2 = x.reshape(m, D_MODEL)
    gla, small, nq, kc, ks, kw, nkb, mg = _proj_in(x2, w["g_mix_pre"], w["w_in_pack"], tm)
    og, gla_state = _gla_prompt(gla, small, w["w_gate_pack"], w["b_gla_gate"], w["g_gla_norm"], batch, seq)

    n_pages = seq // PAGE_SIZE
    ident_pt = jnp.arange(batch * n_pages, dtype=jnp.int32).reshape(batch, n_pages)
    kvc = _compress(ident_pt, kc, *w["cmp"])
    oc, selm = _cmp_topk_prompt(nq, _permute_cmp(kvc), small, batch, seq)
    on = _slc_win_prompt(nq, selm, nkb, small, oc, batch, seq)

    x1 = _mix_out(og, on, mg, x2, w["w_gla_out"], w["w_nsa_out"], w["w_out"], w["g_mix_post"], tm)
    y, alast = _ffn_prompt(x1, p.reshape(m, PLE_DIM), w["ffn"], batch, seq, tm)

    kv_shape = (batch, seq, NSA_KV_GROUPS, 2, NSA_DH)
    win = min(WINDOW, seq)
    return (y.reshape(batch, seq, D_MODEL), kc.reshape(kv_shape), ks.reshape(kv_shape),
            kw.reshape(kv_shape)[:, seq - win:], gla_state, alast[:, 8 - (CONV_W - 1):])


def _row_to_col(row):
    n = row.shape[-1]
    ri = lax.broadcasted_iota(jnp.int32, (n, n), 0)
    ci = lax.broadcasted_iota(jnp.int32, (n, n), 1)
    return jnp.sum(jnp.where(ri == ci, jnp.broadcast_to(row, (n, n)), 0.0), axis=-1, keepdims=True)


def _gla_step_kernel(gla_ref, small_ref, s0_ref, wg_ref, bg_ref, gn_ref, og_ref, s1_ref):
    row = gla_ref[0]
    small = small_ref[0]
    H, DK, DV = GLA_HEADS, GLA_DK, GLA_DV
    outs = []
    for h in range(H):
        q = row[:, h * DK:(h + 1) * DK] * (DK ** -0.5)
        k = row[:, H * DK + h * DK:H * DK + (h + 1) * DK]
        v = row[:, 2 * H * DK + h * DV:2 * H * DK + (h + 1) * DV]
        r = row[:, 2 * H * DK + H * DV + h * DV:2 * H * DK + H * DV + (h + 1) * DV]
        alpha = jnp.exp(_log_decay(small, wg_ref[:, h * DK:(h + 1) * DK], bg_ref[:, h * DK:(h + 1) * DK]))
        s0 = s0_ref[0, h]
        s1_ref[0, h] = _row_to_col(alpha) * s0 + _row_to_col(k) * v
        o = _dot((q * alpha).astype(BF16), s0.astype(BF16)) + jnp.sum(q * k, axis=-1, keepdims=True) * v
        outs.append(_gla_out(o, gn_ref[:, h * DV:(h + 1) * DV], r))
    og_ref[0] = jnp.concatenate(outs, axis=1)


def _gla_sample(gla, small, state, w_gate_pack, b_gate, g_norm):
    nb = gla.shape[0]
    H = GLA_HEADS
    og, s1 = pl.pallas_call(
        _gla_step_kernel,
        out_shape=(jax.ShapeDtypeStruct((nb, 1, H * GLA_DV), BF16), jax.ShapeDtypeStruct(state.shape, F32)),
        grid=(nb,),
        in_specs=[pl.BlockSpec((1, 1, _GLA_W), lambda b: (b, 0, 0)),
                  pl.BlockSpec((1, 1, _SMALL_W), lambda b: (b, 0, 0)),
                  pl.BlockSpec((1, H, GLA_DK, GLA_DV), lambda b: (b, 0, 0, 0)),
                  _const_spec((_SMALL_W, H * GLA_DK)), _const_spec((1, H * GLA_DK)), _const_spec((1, H * GLA_DV))],
        out_specs=(pl.BlockSpec((1, 1, H * GLA_DV), lambda b: (b, 0, 0)),
                   pl.BlockSpec((1, H, GLA_DK, GLA_DV), lambda b: (b, 0, 0, 0))),
        compiler_params=_cparams("parallel"), name="gla_step",
    )(gla.reshape(nb, 1, _GLA_W), small.reshape(nb, 1, _SMALL_W), state, w_gate_pack,
      b_gate.reshape(1, -1), g_norm.reshape(1, -1))
    return og.reshape(nb, H * GLA_DV), s1


def _head_group_rows(x_groups):
    grp = lax.broadcasted_iota(jnp.int32, x_groups[0].shape, 0) // NSA_GROUP_HEADS
    out = jnp.zeros(x_groups[0].shape, F32)
    for g, xg in enumerate(x_groups):
        out = jnp.where(grp == g, xg, out)
    return out


def _cmp_topk_sample_kernel(q_ref, kvc_ref, oc_ref, idx_ref, *, t, n_lane, n_pick):
    q = q_ref[0]
    n_cmp = kvc_ref.shape[2]
    lane = lax.broadcasted_iota(jnp.int32, (1, n_cmp), 1)
    blk_end = (4 * (lane % n_lane) + lane // n_lane) * CMP_STRIDE + (CMP_LEN - 1)
    ocs, imps = [], []
    for g in range(NSA_KV_GROUPS):
        s = _dot_nt(q, kvc_ref[0, 2 * g].astype(BF16))
        p = _masked_softmax(s, jnp.broadcast_to(blk_end <= t, s.shape))
        ocs.append(_dot(p.astype(BF16), kvc_ref[0, 2 * g + 1].astype(BF16)))
        r0 = NSA_GROUP_HEADS * g
        psum = ((p[r0:r0 + 1] + p[r0 + 1:r0 + 2]) + p[r0 + 2:r0 + 3]) + p[r0 + 3:r0 + 4]
        imps.append(_block_importance(psum, n_lane))
    oc_ref[0] = _head_group_rows(ocs)

    imp = jnp.concatenate(imps + [jnp.zeros((8 - NSA_KV_GROUPS, n_lane), F32)], axis=0)
    blk = lax.broadcasted_iota(jnp.int32, imp.shape, 1)
    cur = t // SLC_BLOCK
    forced = (blk == 0) | (blk == cur) | (blk == cur - 1)
    score = jnp.where(blk <= cur, jnp.where(forced, FORCE_SCORE, imp), -jnp.inf)
    idx_out = jnp.zeros(imp.shape, jnp.int32)
    for i in range(n_pick):
        mx = jnp.max(score, axis=-1, keepdims=True)
        idx = jnp.min(jnp.where(score == mx, blk, n_lane), axis=-1, keepdims=True)
        idx_out = jnp.where(blk == i, idx, idx_out)
        score = jnp.where(blk == idx, -jnp.inf, score)
    idx_ref[0] = idx_out


def _slc_win_sample_kernel(idx_ref, pt_ref, kv0_ref, kv1_ref, kv2_ref, kv3_ref, q_ref, ksn_ref, kwn_ref, cw_ref,
                           small_ref, oc_ref, on_ref, m_scr, acc_scr, *, t, win):
    k = pl.program_id(1)
    DH = NSA_DH
    G = NSA_KV_GROUPS
    q = q_ref[0]
    qf = q.astype(F32)
    qpad = jnp.concatenate([q, jnp.zeros_like(q)], axis=1)
    kv_refs = (kv0_ref, kv1_ref, kv2_ref, kv3_ref)
    lane_k = lax.broadcasted_iota(jnp.int32, (SLC_BLOCK, 2 * DH), 1)
    lane_o = lax.broadcasted_iota(jnp.int32, (NSA_HEADS, 2 * DH), 1)

    @pl.when(k == 0)
    def _():
        m_scr[...] = jnp.full(m_scr.shape, -1e30, F32)
        acc_scr[...] = jnp.zeros_like(acc_scr)

    for g in range(G):
        kv = kv_refs[g][0].astype(BF16)
        s = _dot_nt(qpad, jnp.where(lane_k < DH, kv, jnp.zeros_like(kv)))
        m = m_scr[g]
        m_new = jnp.maximum(m, jnp.max(s, axis=-1, keepdims=True))
        p = jnp.exp(s - m_new)
        acc_scr[g] = jnp.exp(m - m_new) * acc_scr[g] + _dot(p.astype(BF16), jnp.where(lane_k < DH, jnp.ones_like(kv), kv))
        m_scr[g] = m_new

    @pl.when(k == pl.num_programs(1) - 1)
    def _():
        ksn = ksn_ref[0]
        kwn = kwn_ref[0]
        o_s, o_w = [], []
        lane_w = lax.broadcasted_iota(jnp.int32, (win, 2 * DH), 1)
        pos = (t - win) + lax.broadcasted_iota(jnp.int32, (1, win), 1)
        w_valid = (pos <= t) & (pos > t - WINDOW) & (pos >= 0)
        for g in range(G):
            kn = ksn[:, g * 2 * DH:(g + 1) * 2 * DH]
            s_n = jnp.sum(qf * kn[:, 0:DH].astype(BF16).astype(F32), axis=-1, keepdims=True)
            m = m_scr[g]
            m_new = jnp.maximum(m, s_n)
            acc = jnp.exp(m - m_new) * acc_scr[g] + jnp.exp(s_n - m_new) * jnp.where(lane_o < DH, 1.0, kn)
            o_s.append(acc[:, DH:] / acc[:, 0:DH])
            kvw = cw_ref[0][:, g * 2 * DH:(g + 1) * 2 * DH].astype(BF16)
            s_w = _dot_nt(qpad, jnp.where(lane_w < DH, kvw, jnp.zeros_like(kvw)))
            s_w = jnp.where(w_valid, s_w, -jnp.inf)
            wn = kwn[:, g * 2 * DH:(g + 1) * 2 * DH]
            s_wn = jnp.sum(qf * wn[:, 0:DH].astype(BF16).astype(F32), axis=-1, keepdims=True)
            mw = jnp.maximum(jnp.max(s_w, axis=-1, keepdims=True), s_wn)
            e = jnp.exp(s_w - mw)
            en = jnp.exp(s_wn - mw)
            num = _dot(e.astype(BF16), kvw)[:, DH:] + en * wn[:, DH:]
            o_w.append(num / (jnp.sum(e, axis=-1, keepdims=True) + en))
        o_s = _head_group_rows(o_s)
        o_w = _head_group_rows(o_w)
        sig = jnp.broadcast_to(_sigmoid(small_ref[0]), (NSA_HEADS, _SMALL_W))
        lane = lax.broadcasted_iota(jnp.int32, sig.shape, 1)
        head = lax.broadcasted_iota(jnp.int32, sig.shape, 0)
        gate = [jnp.sum(jnp.where(lane == _NG_LANE0 + 3 * head + x, sig, 0.0), axis=-1, keepdims=True) for x in range(3)]
        on_ref[0] = (gate[0] * oc_ref[0] + gate[1] * o_s + gate[2] * o_w).astype(BF16)


def _nsa_sample(nq, ks_new, kw_new, small, kvc_perm, cache_slc, cache_win, page_table):
    nb, n_pages = page_table.shape
    past = n_pages * PAGE_SIZE
    t = past
    n_lane = past // SLC_BLOCK
    n_pick = SLC_TOPK - 1
    assert kvc_perm.shape[2] == 4 * n_lane and n_lane >= SLC_TOPK and n_lane % 128 == 0
    win = cache_win.shape[1]
    q3 = nq.reshape(nb, NSA_HEADS, NSA_DH)
    G = NSA_KV_GROUPS

    oc, idx = pl.pallas_call(
        functools.partial(_cmp_topk_sample_kernel, t=t, n_lane=n_lane, n_pick=n_pick),
        out_shape=(jax.ShapeDtypeStruct((nb, NSA_HEADS, NSA_DH), F32), jax.ShapeDtypeStruct((nb, 8, n_lane), jnp.int32)),
        grid=(nb,),
        in_specs=[pl.BlockSpec((1, NSA_HEADS, NSA_DH), lambda b: (b, 0, 0)),
                  pl.BlockSpec((1, 2 * G, 4 * n_lane, NSA_DH), lambda b: (b, 0, 0, 0))],
        out_specs=(pl.BlockSpec((1, NSA_HEADS, NSA_DH), lambda b: (b, 0, 0)),
                   pl.BlockSpec((1, 8, n_lane), lambda b: (b, 0, 0))),
        compiler_params=_cparams("parallel"), name="cmp_topk_sample",
    )(q3, kvc_perm)
    blk_idx = idx[:, :G, :n_pick].reshape(nb, G * n_pick)

    half = PAGE_SIZE // SLC_BLOCK
    slc_blocks = cache_slc.reshape(-1, SLC_BLOCK, NSA_KV_W)

    def kv_spec(g):
        def index(b, k, idx_ref, pt_ref):
            blk = idx_ref[b, g * n_pick + k]
            return (pt_ref[b, blk // half] * half + blk % half, 0, g)
        return pl.BlockSpec((1, SLC_BLOCK, 2 * NSA_DH), index)

    per_seq = lambda shape: pl.BlockSpec((1,) + shape, lambda b, k, *_: (b, 0, 0))
    grid_spec = pltpu.PrefetchScalarGridSpec(
        num_scalar_prefetch=2, grid=(nb, n_pick),
        in_specs=[kv_spec(g) for g in range(G)] + [
            per_seq((NSA_HEADS, NSA_DH)), per_seq((1, NSA_KV_W)), per_seq((1, NSA_KV_W)),
            per_seq((win, NSA_KV_W)), per_seq((1, _SMALL_W)), per_seq((NSA_HEADS, NSA_DH))],
        out_specs=per_seq((NSA_HEADS, NSA_DH)),
        scratch_shapes=[pltpu.VMEM((G, NSA_HEADS, 1), F32), pltpu.VMEM((G, NSA_HEADS, 2 * NSA_DH), F32)])
    on = pl.pallas_call(
        functools.partial(_slc_win_sample_kernel, t=t, win=win),
        out_shape=jax.ShapeDtypeStruct((nb, NSA_HEADS, NSA_DH), BF16), grid_spec=grid_spec,
        compiler_params=_cparams("parallel", "arbitrary"), name="slc_win_sample",
    )(blk_idx, page_table, slc_blocks, slc_blocks, slc_blocks, slc_blocks, q3,
      ks_new.reshape(nb, 1, NSA_KV_W), kw_new.reshape(nb, 1, NSA_KV_W),
      cache_win.reshape(nb, win, NSA_KV_W), small.reshape(nb, 1, _SMALL_W), oc)
    return on.reshape(nb, NSA_HEADS * NSA_DH)


def _ffn_sample_kernel(x_ref, s0_ref, s1_ref, p_ref, gpre_ref, wup_ref, cw_ref, cb_ref, wdn_ref, gpost_ref,
                       gple_ref, wpg_ref, wpp_ref, y_ref, a_ref):
    x = x_ref[...]
    ab = _dot(_rms(x, gpre_ref[...]).astype(BF16), wup_ref[...])
    a = ab[:, 0:D_FF]
    cw = cw_ref[...]
    conv = (s0_ref[...] * cw[0:1] + s1_ref[...] * cw[1:2] + a * cw[2:3]) + cb_ref[...]
    y_ref[...] = _ffn_tail(x, conv, ab[:, D_FF:], p_ref[...], wdn_ref, gpost_ref, gple_ref, wpg_ref, wpp_ref)
    a_ref[...] = a


def _ffn_sample(x, conv_state, p, fw):
    nb = x.shape[0]
    full = lambda w: pl.BlockSpec((nb, w), lambda i: (0, 0))
    return pl.pallas_call(
        _ffn_sample_kernel,
        out_shape=(jax.ShapeDtypeStruct((nb, D_MODEL), F32), jax.ShapeDtypeStruct((nb, D_FF), F32)),
        grid=(1,),
        in_specs=[full(D_MODEL), full(D_FF), full(D_FF), full(PLE_DIM)] + _ffn_weight_specs(),
        out_specs=(full(D_MODEL), full(D_FF)),
        compiler_params=_cparams("arbitrary"), name="ffn_sample",
    )(x, conv_state[:, 0], conv_state[:, 1], p, *fw)


def _layer_sample(x, p, cache_cmp, cache_slc, cache_win, state_gla, state_conv, page_table, w):
    nb, s, _ = x.shape
    assert s == 1
    x2 = x.reshape(nb, D_MODEL)
    gla, small, nq, kc, ks, kw, _, mg = _proj_in(x2, w["g_mix_pre"], w["w_in_pack"], nb)
    og, gla_state = _gla_sample(gla, small, state_gla, w["w_gate_pack"], w["b_gla_gate"], w["g_gla_norm"])

    kvc = _compress(page_table, cache_cmp.reshape(-1, NSA_KV_W), *w["cmp"])
    on = _nsa_sample(nq, ks, kw, small, _permute_cmp(kvc), cache_slc, cache_win, page_table)

    x1 = _mix_out(og, on, mg, x2, w["w_gla_out"], w["w_nsa_out"], w["w_out"], w["g_mix_post"], nb)
    y, a = _ffn_sample(x1, state_conv, p.reshape(nb, PLE_DIM), w["ffn"])
    kv_shape = (nb, 1, NSA_KV_GROUPS, 2, NSA_DH)
    conv_new = jnp.stack([state_conv[:, 1], a], axis=1)
    return (y.reshape(nb, 1, D_MODEL), kc.reshape(kv_shape), ks.reshape(kv_shape), kw.reshape(kv_shape),
            gla_state, conv_new)


def _prep_weights(g_mix_pre, g_mix_post, g_ffn_pre, g_ffn_post, g_ple, w_in, w_gla_gate_up, b_gla_gate,
                  g_gla_norm, w_gla_out, w_cmp1, b_cmp1, w_cmp2, b_cmp2, pe_cmp, w_nsa_out, w_out, w_ffn_up,
                  conv_ffn_w, b_conv_ffn, w_ffn_down, w_ple_proj, w_ple_gate):
    pe_flat = pe_cmp.transpose(1, 0, 2).reshape(2, 1, CMP_LEN * NSA_DH)
    row = lambda v: v.reshape(1, -1)
    return dict(
        g_mix_pre=g_mix_pre, g_mix_post=g_mix_post, w_in_pack=_pack_w_in(w_in),
        w_gate_pack=_pack_w_gate(w_gla_gate_up), b_gla_gate=b_gla_gate, g_gla_norm=g_gla_norm,
        w_gla_out=w_gla_out.astype(BF16), w_nsa_out=w_nsa_out.astype(BF16), w_out=w_out.astype(BF16),
        cmp=(_pack_w_cmp1(w_cmp1), w_cmp1.astype(BF16), pe_flat, b_cmp1.reshape(2, 1, CMP_HIDDEN),
             w_cmp2.astype(BF16), b_cmp2.reshape(2, 1, NSA_DH)),
        ffn=(row(g_ffn_pre), w_ffn_up.astype(BF16), conv_ffn_w, row(b_conv_ffn), w_ffn_down.astype(BF16),
             row(g_ffn_post), row(g_ple), w_ple_gate.astype(BF16), w_ple_proj.astype(BF16)),
    )


def kernel(x_prompt, x_sample, cache_cmp_kv, cache_slc_kv, cache_win_kv, state_gla, state_ffn_conv, page_table,
           p_prompt, p_sample, g_mix_pre, g_mix_post, g_ffn_pre, g_ffn_post, g_ple, w_in, w_gla_gate_up,
           b_gla_gate, g_gla_norm, w_gla_out, w_cmp1, b_cmp1, w_cmp2, b_cmp2, pe_cmp, w_nsa_out, w_out, w_ffn_up,
           conv_ffn_w, b_conv_ffn, w_ffn_down, w_ple_proj, w_ple_gate):
    layer_weights = (g_mix_pre, g_mix_post, g_ffn_pre, g_ffn_post, g_ple, w_in, w_gla_gate_up, b_gla_gate,
                     g_gla_norm, w_gla_out, w_cmp1, b_cmp1, w_cmp2, b_cmp2, pe_cmp, w_nsa_out, w_out, w_ffn_up,
                     conv_ffn_w, b_conv_ffn, w_ffn_down, w_ple_proj, w_ple_gate)
    depth = w_in.shape[0]
    yp, ys = x_prompt, x_sample
    extras_p, extras_s = [], []
    for i in range(depth):
        w = _prep_weights(*[v[i] for v in layer_weights])
        yp, *ep = _layer_prompt(yp, p_prompt[i], w)
        ys, *es = _layer_sample(ys, p_sample[i], cache_cmp_kv[i], cache_slc_kv[i], cache_win_kv[i], state_gla[i],
                                state_ffn_conv[i], page_table, w)
        extras_p.append(ep)
        extras_s.append(es)
    stack = lambda rows, j: jnp.stack([r[j] for r in rows])
    outs = [yp, ys]
    for j in range(5):
        outs += [stack(extras_p, j), stack(extras_s, j)]
    return tuple(outs)
```

```python
import functools

import numpy as np
import jax
import jax.numpy as jnp
from jax import lax
from jax.experimental import pallas as pl
from jax.experimental.pallas import tpu as pltpu

F32 = jnp.float32
BF16 = jnp.bfloat16

D_MODEL = 1024
PAGE_SIZE = 128
GLA_HEADS = 4
GLA_DK = 128
GLA_DV = 256
GLA_GATE_RANK = 16
GLA_TAU = 16.0
GLA_SUB = 16
NSA_HEADS = 16
NSA_KV_GROUPS = 4
NSA_GROUP_HEADS = 4
NSA_DH = 64
NSA_KV_W = 2 * NSA_KV_GROUPS * NSA_DH
CMP_LEN = 32
CMP_STRIDE = 16
CMP_HIDDEN = 128
SLC_BLOCK = 64
SLC_TOPK = 16
WINDOW = 512
FORCE_SCORE = 1.0e4
D_FF = 2816
CONV_W = 3
PLE_DIM = 256
EPS = 1e-6

V7X_VMEM_BYTES = 64 * 1024 * 1024
VMEM_LIMIT = V7X_VMEM_BYTES - 8 * 1024 * 1024

_GLA_OFF, _GLA_W = 0, 2 * GLA_HEADS * GLA_DK + 2 * GLA_HEADS * GLA_DV
_SMALL_OFF, _SMALL_W = _GLA_OFF + _GLA_W, 128
_NQ_OFF, _NQ_W = _SMALL_OFF + _SMALL_W, NSA_HEADS * NSA_DH
_NK_OFF, _NK_W = _NQ_OFF + _NQ_W, 3 * NSA_KV_W
_MG_OFF, _MG_W = _NK_OFF + _NK_W, 2 * D_MODEL
_PACK_W = _MG_OFF + _MG_W
_NG_LANE0 = GLA_GATE_RANK

_MASK_BIG = float(2.0 ** 60)


def _cparams(*sem):
    return pltpu.CompilerParams(dimension_semantics=sem, vmem_limit_bytes=VMEM_LIMIT)


def _const_spec(shape):
    nd = len(shape)
    return pl.BlockSpec(shape, lambda *_: (0,) * nd, pipeline_mode=pl.Buffered(1))


def _rms(x, g):
    return x * lax.rsqrt(jnp.mean(x * x, axis=-1, keepdims=True) + EPS) * g


def _gelu_tanh(x):
    return 0.5 * x * (1.0 + jnp.tanh(np.sqrt(2.0 / np.pi).astype(np.float32) * (x + 0.044715 * (x * x * x))))


def _sigmoid(x):
    return 1.0 / (1.0 + jnp.exp(-x))


def _dot(a, b):
    return jnp.dot(a, b, preferred_element_type=F32)


def _dot_nt(a, b):
    return lax.dot_general(a, b, (((1,), (1,)), ((), ())), preferred_element_type=F32)


def _dot_tn(a, b):
    return lax.dot_general(a, b, (((0,), (0,)), ((), ())), preferred_element_type=F32)


def _masked_softmax(s, mask):
    s = jnp.where(mask, s, -jnp.inf)
    m = jnp.max(s, axis=-1, keepdims=True)
    m = jnp.where(m > -jnp.inf, m, 0.0)
    e = jnp.exp(s - m)
    d = jnp.sum(e, axis=-1, keepdims=True)
    return e / jnp.where(d > 0, d, 1.0)


def _lane_column(x, col):
    lane = lax.broadcasted_iota(jnp.int32, x.shape, 1)
    return jnp.sum(jnp.where(lane == col, x, 0.0), axis=-1, keepdims=True)


def _pack_w_in(w_in):
    sizes = (512, 512, 1024, 1024, GLA_GATE_RANK, 1024, NSA_KV_W, NSA_KV_W, NSA_KV_W, NSA_HEADS * 3, 2 * D_MODEL)
    pts = [int(v) for v in np.cumsum(sizes)[:-1]]
    gq, gk, gv, gr, ga, nq, nkc, nks, nkw, ng, mg = jnp.split(w_in, pts, axis=1)
    small = jnp.concatenate([ga, ng, jnp.zeros((D_MODEL, _SMALL_W - GLA_GATE_RANK - NSA_HEADS * 3), w_in.dtype)], axis=1)
    return jnp.concatenate([gq, gk, gv, gr, small, nq, nkc, nks, nkw, mg], axis=1).astype(BF16)


def _pack_w_kv_t(w_in):
    lo = 512 + 512 + 1024 + 1024 + GLA_GATE_RANK + NSA_HEADS * NSA_DH
    return w_in[:, lo:lo + _NK_W].T.astype(BF16)


def _proj_in_kernel(x_ref, g_ref, w_ref, *refs, kv_major):
    h = _rms(x_ref[...], g_ref[...]).astype(BF16)

    def seg(off, width):
        return _dot(h, w_ref[:, off:off + width])

    if kv_major:
        wkv_ref, gla_ref, small_ref, nq_ref, kc_ref, ks_ref, kw_ref, nkb_ref, mg_ref = refs
        nk = _dot_nt(wkv_ref[...], h)
        kc_ref[0] = nk[0:NSA_KV_W]
        ks_ref[0] = nk[NSA_KV_W:2 * NSA_KV_W]
        kw_ref[0] = nk[2 * NSA_KV_W:3 * NSA_KV_W]
        nkb_ref[0] = nk.astype(BF16)
    else:
        gla_ref, small_ref, nq_ref, kc_ref, ks_ref, kw_ref, mg_ref = refs
        nk = seg(_NK_OFF, _NK_W)
        kc_ref[...] = nk[:, 0:NSA_KV_W]
        ks_ref[...] = nk[:, NSA_KV_W:2 * NSA_KV_W]
        kw_ref[...] = nk[:, 2 * NSA_KV_W:3 * NSA_KV_W]
    gla_ref[...] = seg(_GLA_OFF, _GLA_W)
    small_ref[...] = seg(_SMALL_OFF, _SMALL_W)
    nq_ref[...] = (seg(_NQ_OFF, _NQ_W) * (NSA_DH ** -0.5)).astype(BF16)
    mg_ref[...] = seg(_MG_OFF, _MG_W)


def _proj_in(x, g_pre, w_pack, tm, w_kv_t=None, seq=None):
    m = x.shape[0]
    assert m % tm == 0
    kv_major = w_kv_t is not None
    row = lambda w: pl.BlockSpec((tm, w), lambda i: (i, 0))
    sds = jax.ShapeDtypeStruct
    in_specs = [row(D_MODEL), _const_spec((1, D_MODEL)), _const_spec((D_MODEL, _PACK_W))]
    args = [x, g_pre.reshape(1, D_MODEL), w_pack]
    head = [(sds((m, _GLA_W), F32), row(_GLA_W)), (sds((m, _SMALL_W), F32), row(_SMALL_W)),
            (sds((m, _NQ_W), BF16), row(_NQ_W))]
    if kv_major:
        assert seq % tm == 0
        tps = seq // tm
        col = lambda w: pl.BlockSpec((1, w, tm), lambda i: (i // tps, 0, i % tps))
        in_specs.append(_const_spec((_NK_W, D_MODEL)))
        args.append(w_kv_t)
        kv = [(sds((m // seq, NSA_KV_W, seq), F32), col(NSA_KV_W))] * 3 + [(sds((m // seq, _NK_W, seq), BF16), col(_NK_W))]
    else:
        kv = [(sds((m, NSA_KV_W), F32), row(NSA_KV_W))] * 3
    outs = head + kv + [(sds((m, _MG_W), F32), row(_MG_W))]
    return pl.pallas_call(
        functools.partial(_proj_in_kernel, kv_major=kv_major),
        out_shape=tuple(o[0] for o in outs), grid=(m // tm,), in_specs=in_specs,
        out_specs=tuple(o[1] for o in outs),
        compiler_params=_cparams("parallel"), name="proj_in",
    )(*args)


def _log_decay(small, wg, bg):
    x = _dot(small.astype(BF16), wg) + bg
    return (jnp.minimum(x, 0.0) - jnp.log1p(jnp.exp(-jnp.abs(x)))) * (1.0 / GLA_TAU)


def _pack_w_gate(w_gate_up):
    pad = jnp.zeros((_SMALL_W - GLA_GATE_RANK, GLA_HEADS * GLA_DK), w_gate_up.dtype)
    return jnp.concatenate([w_gate_up, pad], axis=0).astype(BF16)


def _gla_out(o, gn, r):
    o = o * lax.rsqrt(jnp.mean(o * o, axis=-1, keepdims=True) + EPS) * gn
    return (o * (r * _sigmoid(r))).astype(BF16)


def _gla_chunk_kernel(q_ref, k_ref, v_ref, r_ref, small_ref, wg_ref, bg_ref, gn_ref, og_ref, st_ref,
                      s_scr, kp_scr, bp_scr, *, chunk):
    c = pl.program_id(2)
    C, SUB = chunk, GLA_SUB

    @pl.when(c == 0)
    def _():
        s_scr[...] = jnp.zeros_like(s_scr)
        kp_scr[0:SUB, :] = jnp.zeros((SUB, GLA_DK), F32)
        bp_scr[0:SUB, :] = jnp.zeros((SUB, GLA_DK), F32)

    q = q_ref[...] * (GLA_DK ** -0.5)
    k = k_ref[...]
    vb = v_ref[...].astype(BF16)
    la = _log_decay(small_ref[...], wg_ref[...], bg_ref[...])

    ri = lax.broadcasted_iota(jnp.int32, (C, C), 0)
    ci = lax.broadcasted_iota(jnp.int32, (C, C), 1)
    tri = jnp.where(ci <= ri, 1.0, 0.0).astype(BF16)
    hi = la.astype(BF16)
    r1 = la - hi.astype(F32)
    mid = r1.astype(BF16)
    lo = (r1 - mid.astype(F32)).astype(BF16)
    b = _dot(tri, hi) + _dot(tri, mid) + _dot(tri, lo)

    st = s_scr[...]
    o = _dot_nt((q * jnp.exp(b)).astype(BF16), st.astype(BF16))
    b_last = b[C - 1:C, :]
    kdec = (k * jnp.exp(b_last - b)).astype(BF16)
    s_scr[...] = st * jnp.exp(b_last) + _dot_tn(vb, kdec)

    row = lax.broadcasted_iota(jnp.int32, (C, GLA_DK), 0)
    blocks = [jnp.zeros((SUB, C), F32)]
    for i in range(1, C // SUB):
        beta = b[SUB * i - 1:SUB * i, :]
        qi = (q[SUB * i:SUB * (i + 1)] * jnp.exp(b[SUB * i:SUB * (i + 1)] - beta)).astype(BF16)
        ki = (k * jnp.exp(jnp.where(row < SUB * i, beta - b, -jnp.inf))).astype(BF16)
        blocks.append(_dot_nt(qi, ki))
    a = jnp.concatenate(blocks, axis=0)

    kp_scr[SUB:SUB + C, :] = k
    bp_scr[SUB:SUB + C, :] = b
    tmod = ri % SUB
    for j in range(SUB):
        ks = kp_scr[SUB - j:SUB - j + C, :]
        bs = bp_scr[SUB - j:SUB - j + C, :]
        aj = jnp.sum(q * ks * jnp.exp(b - bs), axis=-1, keepdims=True)
        a = a + jnp.where((ci == ri - j) & (tmod >= j), aj, 0.0)
    o = o + _dot(a.astype(BF16), vb)

    og_ref[...] = _gla_out(o, gn_ref[...], r_ref[...])

    @pl.when(c == pl.num_programs(2) - 1)
    def _():
        st_ref[0, 0] = s_scr[...].T


def _gla_prompt(gla, small, w_gate_pack, b_gate, g_norm, batch, seq, chunk=128):
    n = seq // chunk
    assert seq % chunk == 0 and chunk % GLA_SUB == 0
    H = GLA_HEADS
    rows = lambda b, h, c: b * n + c
    kern = functools.partial(_gla_chunk_kernel, chunk=chunk)
    return pl.pallas_call(
        kern,
        out_shape=(jax.ShapeDtypeStruct((batch * seq, H * GLA_DV), BF16),
                   jax.ShapeDtypeStruct((batch, H, GLA_DK, GLA_DV), F32)),
        grid=(batch, H, n),
        in_specs=[
            pl.BlockSpec((chunk, GLA_DK), lambda b, h, c: (rows(b, h, c), h)),
            pl.BlockSpec((chunk, GLA_DK), lambda b, h, c: (rows(b, h, c), H + h)),
            pl.BlockSpec((chunk, GLA_DV), lambda b, h, c: (rows(b, h, c), H + h)),
            pl.BlockSpec((chunk, GLA_DV), lambda b, h, c: (rows(b, h, c), 2 * H + h)),
            pl.BlockSpec((chunk, _SMALL_W), lambda b, h, c: (rows(b, h, c), 0)),
            pl.BlockSpec((_SMALL_W, GLA_DK), lambda b, h, c: (0, h)),
            pl.BlockSpec((1, GLA_DK), lambda b, h, c: (0, h)),
            pl.BlockSpec((1, GLA_DV), lambda b, h, c: (0, h)),
        ],
        out_specs=(pl.BlockSpec((chunk, GLA_DV), lambda b, h, c: (rows(b, h, c), h)),
                   pl.BlockSpec((1, 1, GLA_DK, GLA_DV), lambda b, h, c: (b, h, 0, 0))),
        scratch_shapes=[pltpu.VMEM((GLA_DV, GLA_DK), F32),
                        pltpu.VMEM((GLA_SUB + chunk, GLA_DK), F32),
                        pltpu.VMEM((GLA_SUB + chunk, GLA_DK), F32)],
        compiler_params=_cparams("parallel", "parallel", "arbitrary"), name="gla_chunk",
    )(gla, gla, gla, gla, small, w_gate_pack, b_gate.reshape(1, -1), g_norm.reshape(1, -1))


def _pack_w_cmp1(w1):
    w = w1.reshape(2, 2, CMP_STRIDE, NSA_DH, CMP_HIDDEN)
    per_c = jnp.concatenate([w[:, 0], w[:, 1]], axis=-1)
    z = jnp.zeros_like(per_c[0])
    return jnp.concatenate([jnp.concatenate([per_c[0], z], axis=-1),
                            jnp.concatenate([z, per_c[1]], axis=-1)], axis=1).astype(BF16)


def _compress_kernel(pt_ref, kv_hbm, wp_ref, w1_ref, pef_ref, b1_ref, w2_ref, b2_ref, out_ref,
                     buf, tok_scr, hb_scr, sem, *, n_pages, pages_per_row):
    bidx = pl.program_id(0)
    n_chunk = n_pages * (PAGE_SIZE // CMP_STRIDE)
    GW = 2 * NSA_DH

    def page_copy(p):
        page = pt_ref[bidx, p]
        off = pl.multiple_of((page % pages_per_row) * PAGE_SIZE, PAGE_SIZE)
        src = kv_hbm.at[page // pages_per_row, :, :, :, pl.ds(off, PAGE_SIZE)]
        return pltpu.make_async_copy(src, buf.at[:, :, :, pl.ds(pl.multiple_of(p * PAGE_SIZE, PAGE_SIZE), PAGE_SIZE)], sem)

    def start(p, carry):
        page_copy(p).start()
        return carry

    def wait(p, carry):
        page_copy(p).wait()
        return carry

    lax.fori_loop(0, n_pages, start, 0)
    lax.fori_loop(0, n_pages, wait, 0)

    HID = CMP_HIDDEN
    hb_scr[n_chunk:n_chunk + 8, :] = jnp.zeros((8, HID), F32)
    const = [_dot(pef_ref[c].astype(BF16), w1_ref[c]) + b1_ref[c] for c in range(2)]
    for g in range(NSA_KV_GROUPS):

        def to_token_major(p, carry):
            off = pl.multiple_of(p * PAGE_SIZE, PAGE_SIZE)
            tok_scr[pl.ds(off, PAGE_SIZE), :] = buf[g, :, :, pl.ds(off, PAGE_SIZE)].reshape(GW, PAGE_SIZE).T
            return carry

        lax.fori_loop(0, n_pages, to_token_major, 0)
        acc = jnp.zeros((n_chunk, 4 * HID), F32)
        for p in range(CMP_STRIDE):
            xs = tok_scr[pl.ds(p, n_chunk, stride=CMP_STRIDE), :]
            acc = acc + _dot(xs.astype(BF16), wp_ref[p])
        for c in range(2):
            hb_scr[0:n_chunk, :] = acc[:, (2 * c + 1) * HID:(2 * c + 2) * HID]
            hid = acc[:, 2 * c * HID:(2 * c + 1) * HID] + hb_scr[pl.ds(1, n_chunk), :] + const[c]
            out_ref[0, g * 2 + c] = _dot(_gelu_tanh(hid).astype(BF16), w2_ref[c]) + b2_ref[c]


def _compress(page_table, kv_t, wp, w1b, pe_flat, b1, w2b, b2):
    batch, n_pages = page_table.shape
    n_chunk = n_pages * (PAGE_SIZE // CMP_STRIDE)
    assert kv_t.shape[-1] % PAGE_SIZE == 0
    kern = functools.partial(_compress_kernel, n_pages=n_pages, pages_per_row=kv_t.shape[-1] // PAGE_SIZE)
    grid_spec = pltpu.PrefetchScalarGridSpec(
        num_scalar_prefetch=1, grid=(batch,),
        in_specs=[pl.BlockSpec(memory_space=pl.ANY),
                  _const_spec((CMP_STRIDE, 2 * NSA_DH, 4 * CMP_HIDDEN)),
                  _const_spec((2, CMP_LEN * NSA_DH, CMP_HIDDEN)),
                  _const_spec((2, 1, CMP_LEN * NSA_DH)),
                  _const_spec((2, 1, CMP_HIDDEN)),
                  _const_spec((2, CMP_HIDDEN, NSA_DH)),
                  _const_spec((2, 1, NSA_DH))],
        out_specs=pl.BlockSpec((1, 2 * NSA_KV_GROUPS, n_chunk, NSA_DH), lambda b, pt: (b, 0, 0, 0)),
        scratch_shapes=[pltpu.VMEM((NSA_KV_GROUPS, 2, NSA_DH, n_pages * PAGE_SIZE), F32),
                        pltpu.VMEM((n_pages * PAGE_SIZE, 2 * NSA_DH), F32),
                        pltpu.VMEM((n_chunk + 8, CMP_HIDDEN), F32),
                        pltpu.SemaphoreType.DMA(())])
    return pl.pallas_call(
        kern, out_shape=jax.ShapeDtypeStruct((batch, 2 * NSA_KV_GROUPS, n_chunk, NSA_DH), F32),
        grid_spec=grid_spec, compiler_params=_cparams("arbitrary"), name="compress",
    )(page_table, kv_t, wp, w1b, pe_flat, b1, w2b, b2)


def _feature_major(kv):
    return kv.transpose(0, 2, 3, 4, 1)


def _token_major(kv_t, batch, seq):
    return kv_t.reshape(batch, NSA_KV_GROUPS, 2, NSA_DH, seq).transpose(0, 4, 1, 2, 3)


def _permute_cmp(kvc):
    b, gc, n, d = kvc.shape
    return kvc.reshape(b, gc, n // 4, 4, d).transpose(0, 1, 3, 2, 4).reshape(b, gc, n, d)


def _group_queries(qall):
    return jnp.concatenate([qall[:, r * NSA_DH:(r + 1) * NSA_DH] for r in range(NSA_GROUP_HEADS)], axis=0)


def _block_importance(psum, n_slc):
    p0, p1, p2, p3 = (psum[:, m * n_slc:(m + 1) * n_slc] for m in range(4))
    lane = lax.broadcasted_iota(jnp.int32, p3.shape, 1)
    p3s = jnp.where(lane == 0, 0.0, pltpu.roll(p3, 1, axis=1))
    return ((((((p0 + p3s) + p1) + p0) + p2) + p1) + p3) + p2


def _top_blocks(score, count):
    lane = lax.broadcasted_iota(jnp.int32, score.shape, 1)
    n = score.shape[-1]
    sel = jnp.zeros(score.shape, F32)
    for _ in range(count):
        mx = jnp.max(score, axis=-1, keepdims=True)
        idx = jnp.min(jnp.where(score == mx, lane, n), axis=-1, keepdims=True)
        pick = (lane == idx) & (mx > -jnp.inf)
        sel = jnp.where(pick, 1.0, sel)
        score = jnp.where(pick, -jnp.inf, score)
    return sel


def _cmp_topk_kernel(q_ref, kc_ref, vc_ref, small_ref, oc_ref, sel_ref, *, tq, n_slc):
    g = pl.program_id(1)
    q0 = pl.program_id(2) * tq
    rows = NSA_GROUP_HEADS * tq
    n_cmp = 4 * n_slc
    q = _group_queries(q_ref[...])
    s = _dot_nt(q, kc_ref[0, 0].astype(BF16))
    lane = lax.broadcasted_iota(jnp.int32, (1, n_cmp), 1)
    blk_end = (4 * (lane % n_slc) + lane // n_slc) * CMP_STRIDE + (CMP_LEN - 1)
    t_rows = q0 + lax.broadcasted_iota(jnp.int32, (rows, 1), 0) % tq
    p = _masked_softmax(s, blk_end <= t_rows)
    oc = _dot(p.astype(BF16), vc_ref[0, 0].astype(BF16))

    psum = ((p[0:tq] + p[tq:2 * tq]) + p[2 * tq:3 * tq]) + p[3 * tq:4 * tq]
    imp = _block_importance(psum, n_slc)
    t = q0 + lax.broadcasted_iota(jnp.int32, (tq, 1), 0)
    cur = t // SLC_BLOCK
    blk = lax.broadcasted_iota(jnp.int32, (1, n_slc), 1)
    forced = (blk == 0) | (blk == cur) | (blk == cur - 1)
    score = jnp.where(blk <= cur, jnp.where(forced, FORCE_SCORE, imp), -jnp.inf)
    sel = _top_blocks(score, min(SLC_TOPK, n_slc))
    sel_ref[0] = (sel - 1.0).astype(BF16)

    sig = _sigmoid(small_ref[...])
    outs = []
    for r in range(NSA_GROUP_HEADS):
        gate = _lane_column(sig, _NG_LANE0 + (g * NSA_GROUP_HEADS + r) * 3)
        outs.append(gate * oc[r * tq:(r + 1) * tq])
    oc_ref[...] = jnp.concatenate(outs, axis=1)


def _cmp_topk_prompt(nq, kvc_perm, small, batch, seq, tq=128):
    G = NSA_KV_GROUPS
    n_slc = seq // SLC_BLOCK
    assert kvc_perm.shape[2] == 4 * n_slc and seq % tq == 0
    nqt = seq // tq
    gw = NSA_GROUP_HEADS * NSA_DH
    kern = functools.partial(_cmp_topk_kernel, tq=tq, n_slc=n_slc)
    return pl.pallas_call(
        kern,
        out_shape=(jax.ShapeDtypeStruct((batch * seq, NSA_HEADS * NSA_DH), F32),
                   jax.ShapeDtypeStruct((batch * G, seq, n_slc), BF16)),
        grid=(batch, G, nqt),
        in_specs=[pl.BlockSpec((tq, gw), lambda b, g, i: (b * nqt + i, g)),
                  pl.BlockSpec((1, 1, 4 * n_slc, NSA_DH), lambda b, g, i: (b, 2 * g, 0, 0)),
                  pl.BlockSpec((1, 1, 4 * n_slc, NSA_DH), lambda b, g, i: (b, 2 * g + 1, 0, 0)),
                  pl.BlockSpec((tq, _SMALL_W), lambda b, g, i: (b * nqt + i, 0))],
        out_specs=(pl.BlockSpec((tq, gw), lambda b, g, i: (b * nqt + i, g)),
                   pl.BlockSpec((1, tq, n_slc), lambda b, g, i: (b * G + g, i, 0))),
        compiler_params=_cparams("parallel", "parallel", "parallel"), name="cmp_topk",
    )(nq, kvc_perm, kvc_perm, small)


def _slc_win_kernel(q_ref, sel_ref, ks_ref, kw_ref, small_ref, oc_ref, on_ref, *, tq, tk, n_slc):
    g = pl.program_id(1)
    q0 = pl.program_id(2) * tq
    rows = NSA_GROUP_HEADS * tq
    DH = NSA_DH
    q = _group_queries(q_ref[...])
    qpad = jnp.concatenate([q, jnp.zeros((rows, DH), BF16)], axis=1)
    sel4 = jnp.concatenate([sel_ref[0]] * NSA_GROUP_HEADS, axis=0)
    qaug = jnp.concatenate([qpad, sel4], axis=1)
    t_rows = q0 + lax.broadcasted_iota(jnp.int32, (rows, 1), 0) % tq

    is_k = lax.broadcasted_iota(jnp.int32, (2 * DH, tk), 0) < DH
    blk_r = lax.broadcasted_iota(jnp.int32, (n_slc, tk), 0)
    tok_c = lax.broadcasted_iota(jnp.int32, (n_slc, tk), 1)
    tok_l = lax.broadcasted_iota(jnp.int32, (1, tk), 1)

    def tile(kt, m, acc, causal):
        k0 = pl.multiple_of(kt * tk, tk)
        kv = ks_ref[0, :, pl.ds(k0, tk)]
        onehot = jnp.where(blk_r == (k0 + tok_c) // SLC_BLOCK, _MASK_BIG, 0.0).astype(BF16)
        kaug = jnp.concatenate([jnp.where(is_k, kv, jnp.zeros_like(kv)), onehot], axis=0)
        s = _dot(qaug, kaug)
        if causal:
            s = jnp.where(k0 + tok_l <= t_rows, s, -_MASK_BIG)
        m_new = jnp.maximum(m, jnp.max(s, axis=-1, keepdims=True))
        p = jnp.exp(s - m_new)
        vaug = jnp.where(is_k, jnp.ones_like(kv), kv)
        acc = jnp.exp(m - m_new) * acc + _dot_nt(p.astype(BF16), vaug)
        return m_new, acc

    last = (q0 + tq - 1) // tk
    m0 = jnp.full((rows, 1), -1e30, F32)
    acc0 = jnp.zeros((rows, 2 * DH), F32)
    m, acc = lax.fori_loop(0, last, lambda kt, c: tile(kt, c[0], c[1], False), (m0, acc0))
    m, acc = tile(last, m, acc, True)
    o_s = acc[:, DH:] / acc[:, 0:DH]

    wlen = WINDOW + tq
    w0 = pl.multiple_of(jnp.maximum(q0 - WINDOW, 0), tq)
    kvw = kw_ref[0, :, pl.ds(w0, wlen)]
    row_w = lax.broadcasted_iota(jnp.int32, (2 * DH, wlen), 0)
    s_w = _dot(qpad, jnp.where(row_w < DH, kvw, jnp.zeros_like(kvw)))
    pos = w0 + lax.broadcasted_iota(jnp.int32, (1, wlen), 1)
    p_w = _masked_softmax(s_w, (pos <= t_rows) & (pos > t_rows - WINDOW))
    o_w = _dot_nt(p_w.astype(BF16), kvw)[:, DH:]

    sig = _sigmoid(small_ref[...])
    oc = oc_ref[...]
    outs = []
    for r in range(NSA_GROUP_HEADS):
        lane0 = _NG_LANE0 + (g * NSA_GROUP_HEADS + r) * 3
        g_s = _lane_column(sig, lane0 + 1)
        g_w = _lane_column(sig, lane0 + 2)
        rs = slice(r * tq, (r + 1) * tq)
        outs.append(oc[:, r * DH:(r + 1) * DH] + g_s * o_s[rs] + g_w * o_w[rs])
    on_ref[...] = jnp.concatenate(outs, axis=1).astype(BF16)


def _slc_win_prompt(nq, selm, nkb, small, oc, batch, seq, tq=128, tk=256):
    G = NSA_KV_GROUPS
    n_slc = seq // SLC_BLOCK
    assert seq % tk == 0 and seq % tq == 0 and tk % tq == 0 and seq >= WINDOW + tq
    nqt = seq // tq
    gw = NSA_GROUP_HEADS * NSA_DH
    kvw = 2 * NSA_DH
    kern = functools.partial(_slc_win_kernel, tq=tq, tk=tk, n_slc=n_slc)
    return pl.pallas_call(
        kern, out_shape=jax.ShapeDtypeStruct((batch * seq, NSA_HEADS * NSA_DH), BF16),
        grid=(batch, G, nqt),
        in_specs=[pl.BlockSpec((tq, gw), lambda b, g, i: (b * nqt + i, g)),
                  pl.BlockSpec((1, tq, n_slc), lambda b, g, i: (b * G + g, i, 0)),
                  pl.BlockSpec((1, kvw, seq), lambda b, g, i: (b, G + g, 0)),
                  pl.BlockSpec((1, kvw, seq), lambda b, g, i: (b, 2 * G + g, 0)),
                  pl.BlockSpec((tq, _SMALL_W), lambda b, g, i: (b * nqt + i, 0)),
                  pl.BlockSpec((tq, gw), lambda b, g, i: (b * nqt + i, g))],
        out_specs=pl.BlockSpec((tq, gw), lambda b, g, i: (b * nqt + i, g)),
        compiler_params=_cparams("parallel", "parallel", "parallel"), name="slc_win",
    )(nq, selm, nkb, nkb, small, oc)


def _mix_out_kernel(og_ref, on_ref, mg_ref, x_ref, wgo_ref, wno_ref, wo_ref, gpost_ref, y_ref):
    yg = _dot(og_ref[...], wgo_ref[...])
    yn = _dot(on_ref[...], wno_ref[...])
    mg = mg_ref[...]
    mix = _sigmoid(mg[:, 0:D_MODEL]) * yg + _sigmoid(mg[:, D_MODEL:]) * yn
    z = _dot(mix.astype(BF16), wo_ref[...])
    y_ref[...] = x_ref[...] + _rms(z, gpost_ref[...])


def _mix_out(og, on, mg, x, wgo, wno, wo, g_post, tm):
    m = x.shape[0]
    row = lambda w: pl.BlockSpec((tm, w), lambda i: (i, 0))
    wspec = _const_spec((D_MODEL, D_MODEL))
    return pl.pallas_call(
        _mix_out_kernel, out_shape=jax.ShapeDtypeStruct((m, D_MODEL), F32), grid=(m // tm,),
        in_specs=[row(D_MODEL), row(D_MODEL), row(2 * D_MODEL), row(D_MODEL), wspec, wspec, wspec,
                  _const_spec((1, D_MODEL))],
        out_specs=row(D_MODEL), compiler_params=_cparams("parallel"), name="mix_out",
    )(og, on, mg, x, wgo, wno, wo, g_post.reshape(1, D_MODEL))


_FFN_HALO = 16


def _ffn_tail(x, conv, bgate, p, wdn_ref, gpost_ref, gple_ref, wpg_ref, wpp_ref):
    y = _dot((_gelu_tanh(conv) * bgate).astype(BF16), wdn_ref[...])
    x2 = x + _rms(y, gpost_ref[...])
    gate = _sigmoid(_dot(_rms(x2, gple_ref[...]).astype(BF16), wpg_ref[...]))
    return x2 + gate * _dot(p.astype(BF16), wpp_ref[...])


def _ffn_prompt_kernel(x_ref, xh_ref, p_ref, gpre_ref, wup_ref, cw_ref, cb_ref, wdn_ref, gpost_ref, gple_ref,
                       wpg_ref, wpp_ref, y_ref, alast_ref, h_scr, a_scr, *, tm, tiles_per_seq):
    H = _FFN_HALO
    x = x_ref[...]
    h_scr[0:H, :] = _rms(xh_ref[...], gpre_ref[...]).astype(BF16)
    h_scr[H:H + tm, :] = _rms(x, gpre_ref[...]).astype(BF16)
    ab = _dot(h_scr[...], wup_ref[...])
    keep = jnp.where(pl.program_id(0) % tiles_per_seq == 0, 0.0, 1.0)
    a_scr[0:H, :] = ab[0:H, 0:D_FF] * keep
    a_scr[H:H + tm, :] = ab[H:, 0:D_FF]
    cw = cw_ref[...]
    conv = (a_scr[H - 2:H - 2 + tm, :] * cw[0:1] + a_scr[H - 1:H - 1 + tm, :] * cw[1:2]
            + a_scr[H:H + tm, :] * cw[2:3]) + cb_ref[...]
    y_ref[...] = _ffn_tail(x, conv, ab[H:, D_FF:], p_ref[...], wdn_ref, gpost_ref, gple_ref, wpg_ref, wpp_ref)
    alast_ref[0] = a_scr[H + tm - 8:H + tm, :]


def _ffn_weight_specs():
    return [_const_spec((1, D_MODEL)), _const_spec((D_MODEL, 2 * D_FF)), _const_spec((CONV_W, D_FF)),
            _const_spec((1, D_FF)), _const_spec((D_FF, D_MODEL)), _const_spec((1, D_MODEL)),
            _const_spec((1, D_MODEL)), _const_spec((D_MODEL, D_MODEL)), _const_spec((PLE_DIM, D_MODEL))]


def _ffn_prompt(x, p, fw, batch, seq, tm=256):
    assert seq % tm == 0 and tm % _FFN_HALO == 0
    tps = seq // tm
    hb = tm // _FFN_HALO
    kern = functools.partial(_ffn_prompt_kernel, tm=tm, tiles_per_seq=tps)
    return pl.pallas_call(
        kern,
        out_shape=(jax.ShapeDtypeStruct((batch * seq, D_MODEL), F32), jax.ShapeDtypeStruct((batch, 8, D_FF), F32)),
        grid=(batch * tps,),
        in_specs=[pl.BlockSpec((tm, D_MODEL), lambda i: (i, 0)),
                  pl.BlockSpec((_FFN_HALO, D_MODEL), lambda i: (jnp.maximum(i * hb - 1, 0), 0)),
                  pl.BlockSpec((tm, PLE_DIM), lambda i: (i, 0))] + _ffn_weight_specs(),
        out_specs=(pl.BlockSpec((tm, D_MODEL), lambda i: (i, 0)),
                   pl.BlockSpec((1, 8, D_FF), lambda i: (i // tps, 0, 0))),
        scratch_shapes=[pltpu.VMEM((_FFN_HALO + tm, D_MODEL), BF16), pltpu.VMEM((_FFN_HALO + tm, D_FF), F32)],
        compiler_params=_cparams("arbitrary"), name="ffn_prompt",
    )(x, x, p, *fw)


def _layer_prompt(x, p, w, tm=256):
    batch, seq, _ = x.shape
    m = batch * seq
    x2 = x.reshape(m, D_MODEL)
    gla, small, nq, kc_t, ks_t, kw_t, nkb_t, mg = _proj_in(x2, w["g_mix_pre"], w["w_in_pack"], tm, w["w_kv_t"], seq)
    og, gla_state = _gla_prompt(gla, small, w["w_gate_pack"], w["b_gla_gate"], w["g_gla_norm"], batch, seq)

    n_pages = seq // PAGE_SIZE
    ident_pt = jnp.arange(batch * n_pages, dtype=jnp.int32).reshape(batch, n_pages)
    kvc = _compress(ident_pt, kc_t.reshape(batch, NSA_KV_GROUPS, 2, NSA_DH, seq), *w["cmp"])
    oc, selm = _cmp_topk_prompt(nq, _permute_cmp(kvc), small, batch, seq)
    on = _slc_win_prompt(nq, selm, nkb_t, small, oc, batch, seq)

    x1 = _mix_out(og, on, mg, x2, w["w_gla_out"], w["w_nsa_out"], w["w_out"], w["g_mix_post"], tm)
    y, alast = _ffn_prompt(x1, p.reshape(m, PLE_DIM), w["ffn"], batch, seq, tm)

    win = min(WINDOW, seq)
    return (y.reshape(batch, seq, D_MODEL), _token_major(kc_t, batch, seq), _token_major(ks_t, batch, seq),
            _token_major(kw_t[:, :, seq - win:], batch, win), gla_state, alast[:, 8 - (CONV_W - 1):])


def _row_to_col(row):
    n = row.shape[-1]
    ri = lax.broadcasted_iota(jnp.int32, (n, n), 0)
    ci = lax.broadcasted_iota(jnp.int32, (n, n), 1)
    return jnp.sum(jnp.where(ri == ci, jnp.broadcast_to(row, (n, n)), 0.0), axis=-1, keepdims=True)


def _gla_step_kernel(gla_ref, small_ref, s0_ref, wg_ref, bg_ref, gn_ref, og_ref, s1_ref):
    row = gla_ref[0]
    small = small_ref[0]
    H, DK, DV = GLA_HEADS, GLA_DK, GLA_DV
    outs = []
    for h in range(H):
        q = row[:, h * DK:(h + 1) * DK] * (DK ** -0.5)
        k = row[:, H * DK + h * DK:H * DK + (h + 1) * DK]
        v = row[:, 2 * H * DK + h * DV:2 * H * DK + (h + 1) * DV]
        r = row[:, 2 * H * DK + H * DV + h * DV:2 * H * DK + H * DV + (h + 1) * DV]
        alpha = jnp.exp(_log_decay(small, wg_ref[:, h * DK:(h + 1) * DK], bg_ref[:, h * DK:(h + 1) * DK]))
        s0 = s0_ref[0, h]
        s1_ref[0, h] = _row_to_col(alpha) * s0 + _row_to_col(k) * v
        o = _dot((q * alpha).astype(BF16), s0.astype(BF16)) + jnp.sum(q * k, axis=-1, keepdims=True) * v
        outs.append(_gla_out(o, gn_ref[:, h * DV:(h + 1) * DV], r))
    og_ref[0] = jnp.concatenate(outs, axis=1)


def _gla_sample(gla, small, state, w_gate_pack, b_gate, g_norm):
    nb = gla.shape[0]
    H = GLA_HEADS
    og, s1 = pl.pallas_call(
        _gla_step_kernel,
        out_shape=(jax.ShapeDtypeStruct((nb, 1, H * GLA_DV), BF16), jax.ShapeDtypeStruct(state.shape, F32)),
        grid=(nb,),
        in_specs=[pl.BlockSpec((1, 1, _GLA_W), lambda b: (b, 0, 0)),
                  pl.BlockSpec((1, 1, _SMALL_W), lambda b: (b, 0, 0)),
                  pl.BlockSpec((1, H, GLA_DK, GLA_DV), lambda b: (b, 0, 0, 0)),
                  _const_spec((_SMALL_W, H * GLA_DK)), _const_spec((1, H * GLA_DK)), _const_spec((1, H * GLA_DV))],
        out_specs=(pl.BlockSpec((1, 1, H * GLA_DV), lambda b: (b, 0, 0)),
                   pl.BlockSpec((1, H, GLA_DK, GLA_DV), lambda b: (b, 0, 0, 0))),
        compiler_params=_cparams("parallel"), name="gla_step",
    )(gla.reshape(nb, 1, _GLA_W), small.reshape(nb, 1, _SMALL_W), state, w_gate_pack,
      b_gate.reshape(1, -1), g_norm.reshape(1, -1))
    return og.reshape(nb, H * GLA_DV), s1


def _head_group_rows(x_groups):
    grp = lax.broadcasted_iota(jnp.int32, x_groups[0].shape, 0) // NSA_GROUP_HEADS
    out = jnp.zeros(x_groups[0].shape, F32)
    for g, xg in enumerate(x_groups):
        out = jnp.where(grp == g, xg, out)
    return out


def _cmp_topk_sample_kernel(q_ref, kvc_ref, oc_ref, idx_ref, *, t, n_lane, n_pick):
    q = q_ref[0]
    n_cmp = kvc_ref.shape[2]
    lane = lax.broadcasted_iota(jnp.int32, (1, n_cmp), 1)
    blk_end = (4 * (lane % n_lane) + lane // n_lane) * CMP_STRIDE + (CMP_LEN - 1)
    ocs, imps = [], []
    for g in range(NSA_KV_GROUPS):
        s = _dot_nt(q, kvc_ref[0, 2 * g].astype(BF16))
        p = _masked_softmax(s, jnp.broadcast_to(blk_end <= t, s.shape))
        ocs.append(_dot(p.astype(BF16), kvc_ref[0, 2 * g + 1].astype(BF16)))
        r0 = NSA_GROUP_HEADS * g
        psum = ((p[r0:r0 + 1] + p[r0 + 1:r0 + 2]) + p[r0 + 2:r0 + 3]) + p[r0 + 3:r0 + 4]
        imps.append(_block_importance(psum, n_lane))
    oc_ref[0] = _head_group_rows(ocs)

    imp = jnp.concatenate(imps + [jnp.zeros((8 - NSA_KV_GROUPS, n_lane), F32)], axis=0)
    blk = lax.broadcasted_iota(jnp.int32, imp.shape, 1)
    cur = t // SLC_BLOCK
    forced = (blk == 0) | (blk == cur) | (blk == cur - 1)
    score = jnp.where(blk <= cur, jnp.where(forced, FORCE_SCORE, imp), -jnp.inf)
    idx_out = jnp.zeros(imp.shape, jnp.int32)
    for i in range(n_pick):
        mx = jnp.max(score, axis=-1, keepdims=True)
        idx = jnp.min(jnp.where(score == mx, blk, n_lane), axis=-1, keepdims=True)
        idx_out = jnp.where(blk == i, idx, idx_out)
        score = jnp.where(blk == idx, -jnp.inf, score)
    idx_ref[0] = idx_out


def _slc_win_sample_kernel(idx_ref, pt_ref, kv0_ref, kv1_ref, kv2_ref, kv3_ref, q_ref, ksn_ref, kwn_ref, cw_ref,
                           small_ref, oc_ref, on_ref, m_scr, acc_scr, *, t, win):
    b = pl.program_id(0)
    k = pl.program_id(1)
    DH = NSA_DH
    G = NSA_KV_GROUPS
    n_pick = pl.num_programs(1)
    q = q_ref[0]
    qf = q.astype(F32)
    qpad = jnp.concatenate([q, jnp.zeros_like(q)], axis=1)
    kv_refs = (kv0_ref, kv1_ref, kv2_ref, kv3_ref)
    is_k = lax.broadcasted_iota(jnp.int32, (2 * DH, PAGE_SIZE), 0) < DH
    half_of_lane = lax.broadcasted_iota(jnp.int32, (1, PAGE_SIZE), 1) // SLC_BLOCK
    lane_o = lax.broadcasted_iota(jnp.int32, (NSA_HEADS, 2 * DH), 1)

    @pl.when(k == 0)
    def _():
        m_scr[...] = jnp.full(m_scr.shape, -1e30, F32)
        acc_scr[...] = jnp.zeros_like(acc_scr)

    for g in range(G):
        kv = kv_refs[g][0, 0].reshape(2 * DH, PAGE_SIZE).astype(BF16)
        half = idx_ref[b, g * n_pick + k] % (PAGE_SIZE // SLC_BLOCK)
        s = _dot(qpad, jnp.where(is_k, kv, jnp.zeros_like(kv)))
        s = jnp.where(half_of_lane == half, s, -_MASK_BIG)
        m = m_scr[g]
        m_new = jnp.maximum(m, jnp.max(s, axis=-1, keepdims=True))
        p = jnp.exp(s - m_new)
        acc_scr[g] = jnp.exp(m - m_new) * acc_scr[g] + _dot_nt(p.astype(BF16), jnp.where(is_k, jnp.ones_like(kv), kv))
        m_scr[g] = m_new

    @pl.when(k == pl.num_programs(1) - 1)
    def _():
        ksn = ksn_ref[0]
        kwn = kwn_ref[0]
        o_s, o_w = [], []
        row_w = lax.broadcasted_iota(jnp.int32, (2 * DH, win), 0)
        pos = (t - win) + lax.broadcasted_iota(jnp.int32, (1, win), 1)
        w_valid = (pos <= t) & (pos > t - WINDOW) & (pos >= 0)
        for g in range(G):
            kn = ksn[:, g * 2 * DH:(g + 1) * 2 * DH]
            s_n = jnp.sum(qf * kn[:, 0:DH].astype(BF16).astype(F32), axis=-1, keepdims=True)
            m = m_scr[g]
            m_new = jnp.maximum(m, s_n)
            acc = jnp.exp(m - m_new) * acc_scr[g] + jnp.exp(s_n - m_new) * jnp.where(lane_o < DH, 1.0, kn)
            o_s.append(acc[:, DH:] / acc[:, 0:DH])
            kvw = cw_ref[0, g].reshape(2 * DH, win).astype(BF16)
            s_w = _dot(qpad, jnp.where(row_w < DH, kvw, jnp.zeros_like(kvw)))
            s_w = jnp.where(w_valid, s_w, -jnp.inf)
            wn = kwn[:, g * 2 * DH:(g + 1) * 2 * DH]
            s_wn = jnp.sum(qf * wn[:, 0:DH].astype(BF16).astype(F32), axis=-1, keepdims=True)
            mw = jnp.maximum(jnp.max(s_w, axis=-1, keepdims=True), s_wn)
            e = jnp.exp(s_w - mw)
            en = jnp.exp(s_wn - mw)
            num = _dot_nt(e.astype(BF16), kvw)[:, DH:] + en * wn[:, DH:]
            o_w.append(num / (jnp.sum(e, axis=-1, keepdims=True) + en))
        o_s = _head_group_rows(o_s)
        o_w = _head_group_rows(o_w)
        sig = jnp.broadcast_to(_sigmoid(small_ref[0]), (NSA_HEADS, _SMALL_W))
        lane = lax.broadcasted_iota(jnp.int32, sig.shape, 1)
        head = lax.broadcasted_iota(jnp.int32, sig.shape, 0)
        gate = [jnp.sum(jnp.where(lane == _NG_LANE0 + 3 * head + x, sig, 0.0), axis=-1, keepdims=True) for x in range(3)]
        on_ref[0] = (gate[0] * oc_ref[0] + gate[1] * o_s + gate[2] * o_w).astype(BF16)


def _nsa_sample(nq, ks_new, kw_new, small, kvc_perm, cache_slc, cache_win, page_table):
    nb, n_pages = page_table.shape
    past = n_pages * PAGE_SIZE
    t = past
    n_lane = past // SLC_BLOCK
    n_pick = SLC_TOPK - 1
    assert kvc_perm.shape[2] == 4 * n_lane and n_lane >= SLC_TOPK and n_lane % 128 == 0
    win = cache_win.shape[1]
    q3 = nq.reshape(nb, NSA_HEADS, NSA_DH)
    G = NSA_KV_GROUPS

    oc, idx = pl.pallas_call(
        functools.partial(_cmp_topk_sample_kernel, t=t, n_lane=n_lane, n_pick=n_pick),
        out_shape=(jax.ShapeDtypeStruct((nb, NSA_HEADS, NSA_DH), F32), jax.ShapeDtypeStruct((nb, 8, n_lane), jnp.int32)),
        grid=(nb,),
        in_specs=[pl.BlockSpec((1, NSA_HEADS, NSA_DH), lambda b: (b, 0, 0)),
                  pl.BlockSpec((1, 2 * G, 4 * n_lane, NSA_DH), lambda b: (b, 0, 0, 0))],
        out_specs=(pl.BlockSpec((1, NSA_HEADS, NSA_DH), lambda b: (b, 0, 0)),
                   pl.BlockSpec((1, 8, n_lane), lambda b: (b, 0, 0))),
        compiler_params=_cparams("parallel"), name="cmp_topk_sample",
    )(q3, kvc_perm)
    blk_idx = idx[:, :G, :n_pick].reshape(nb, G * n_pick)

    half = PAGE_SIZE // SLC_BLOCK
    slc_t = _feature_major(cache_slc)
    win_t = _feature_major(cache_win)

    def kv_spec(g):
        def index(b, k, idx_ref, pt_ref):
            return (pt_ref[b, idx_ref[b, g * n_pick + k] // half], g, 0, 0, 0)
        return pl.BlockSpec((1, 1, 2, NSA_DH, PAGE_SIZE), index)

    per_seq = lambda shape: pl.BlockSpec((1,) + shape, lambda b, k, *_: (b,) + (0,) * len(shape))
    grid_spec = pltpu.PrefetchScalarGridSpec(
        num_scalar_prefetch=2, grid=(nb, n_pick),
        in_specs=[kv_spec(g) for g in range(G)] + [
            per_seq((NSA_HEADS, NSA_DH)), per_seq((1, NSA_KV_W)), per_seq((1, NSA_KV_W)),
            per_seq((G, 2, NSA_DH, win)), per_seq((1, _SMALL_W)), per_seq((NSA_HEADS, NSA_DH))],
        out_specs=per_seq((NSA_HEADS, NSA_DH)),
        scratch_shapes=[pltpu.VMEM((G, NSA_HEADS, 1), F32), pltpu.VMEM((G, NSA_HEADS, 2 * NSA_DH), F32)])
    on = pl.pallas_call(
        functools.partial(_slc_win_sample_kernel, t=t, win=win),
        out_shape=jax.ShapeDtypeStruct((nb, NSA_HEADS, NSA_DH), BF16), grid_spec=grid_spec,
        compiler_params=_cparams("parallel", "arbitrary"), name="slc_win_sample",
    )(blk_idx, page_table, slc_t, slc_t, slc_t, slc_t, q3,
      ks_new.reshape(nb, 1, NSA_KV_W), kw_new.reshape(nb, 1, NSA_KV_W),
      win_t, small.reshape(nb, 1, _SMALL_W), oc)
    return on.reshape(nb, NSA_HEADS * NSA_DH)


def _ffn_sample_kernel(x_ref, s0_ref, s1_ref, p_ref, gpre_ref, wup_ref, cw_ref, cb_ref, wdn_ref, gpost_ref,
                       gple_ref, wpg_ref, wpp_ref, y_ref, a_ref):
    x = x_ref[...]
    ab = _dot(_rms(x, gpre_ref[...]).astype(BF16), wup_ref[...])
    a = ab[:, 0:D_FF]
    cw = cw_ref[...]
    conv = (s0_ref[...] * cw[0:1] + s1_ref[...] * cw[1:2] + a * cw[2:3]) + cb_ref[...]
    y_ref[...] = _ffn_tail(x, conv, ab[:, D_FF:], p_ref[...], wdn_ref, gpost_ref, gple_ref, wpg_ref, wpp_ref)
    a_ref[...] = a


def _ffn_sample(x, conv_state, p, fw):
    nb = x.shape[0]
    full = lambda w: pl.BlockSpec((nb, w), lambda i: (0, 0))
    return pl.pallas_call(
        _ffn_sample_kernel,
        out_shape=(jax.ShapeDtypeStruct((nb, D_MODEL), F32), jax.ShapeDtypeStruct((nb, D_FF), F32)),
        grid=(1,),
        in_specs=[full(D_MODEL), full(D_FF), full(D_FF), full(PLE_DIM)] + _ffn_weight_specs(),
        out_specs=(full(D_MODEL), full(D_FF)),
        compiler_params=_cparams("arbitrary"), name="ffn_sample",
    )(x, conv_state[:, 0], conv_state[:, 1], p, *fw)


def _layer_sample(x, p, cache_cmp, cache_slc, cache_win, state_gla, state_conv, page_table, w):
    nb, s, _ = x.shape
    assert s == 1
    x2 = x.reshape(nb, D_MODEL)
    gla, small, nq, kc, ks, kw, mg = _proj_in(x2, w["g_mix_pre"], w["w_in_pack"], nb)
    og, gla_state = _gla_sample(gla, small, state_gla, w["w_gate_pack"], w["b_gla_gate"], w["g_gla_norm"])

    kvc = _compress(page_table, _feature_major(cache_cmp), *w["cmp"])
    on = _nsa_sample(nq, ks, kw, small, _permute_cmp(kvc), cache_slc, cache_win, page_table)

    x1 = _mix_out(og, on, mg, x2, w["w_gla_out"], w["w_nsa_out"], w["w_out"], w["g_mix_post"], nb)
    y, a = _ffn_sample(x1, state_conv, p.reshape(nb, PLE_DIM), w["ffn"])
    kv_shape = (nb, 1, NSA_KV_GROUPS, 2, NSA_DH)
    conv_new = jnp.stack([state_conv[:, 1], a], axis=1)
    return (y.reshape(nb, 1, D_MODEL), kc.reshape(kv_shape), ks.reshape(kv_shape), kw.reshape(kv_shape),
            gla_state, conv_new)


def _prep_weights(g_mix_pre, g_mix_post, g_ffn_pre, g_ffn_post, g_ple, w_in, w_gla_gate_up, b_gla_gate,
                  g_gla_norm, w_gla_out, w_cmp1, b_cmp1, w_cmp2, b_cmp2, pe_cmp, w_nsa_out, w_out, w_ffn_up,
                  conv_ffn_w, b_conv_ffn, w_ffn_down, w_ple_proj, w_ple_gate):
    pe_flat = pe_cmp.transpose(1, 0, 2).reshape(2, 1, CMP_LEN * NSA_DH)
    row = lambda v: v.reshape(1, -1)
    return dict(
        g_mix_pre=g_mix_pre, g_mix_post=g_mix_post, w_in_pack=_pack_w_in(w_in), w_kv_t=_pack_w_kv_t(w_in),
        w_gate_pack=_pack_w_gate(w_gla_gate_up), b_gla_gate=b_gla_gate, g_gla_norm=g_gla_norm,
        w_gla_out=w_gla_out.astype(BF16), w_nsa_out=w_nsa_out.astype(BF16), w_out=w_out.astype(BF16),
        cmp=(_pack_w_cmp1(w_cmp1), w_cmp1.astype(BF16), pe_flat, b_cmp1.reshape(2, 1, CMP_HIDDEN),
             w_cmp2.astype(BF16), b_cmp2.reshape(2, 1, NSA_DH)),
        ffn=(row(g_ffn_pre), w_ffn_up.astype(BF16), conv_ffn_w, row(b_conv_ffn), w_ffn_down.astype(BF16),
             row(g_ffn_post), row(g_ple), w_ple_gate.astype(BF16), w_ple_proj.astype(BF16)),
    )


def kernel(x_prompt, x_sample, cache_cmp_kv, cache_slc_kv, cache_win_kv, state_gla, state_ffn_conv, page_table,
           p_prompt, p_sample, g_mix_pre, g_mix_post, g_ffn_pre, g_ffn_post, g_ple, w_in, w_gla_gate_up,
           b_gla_gate, g_gla_norm, w_gla_out, w_cmp1, b_cmp1, w_cmp2, b_cmp2, pe_cmp, w_nsa_out, w_out, w_ffn_up,
           conv_ffn_w, b_conv_ffn, w_ffn_down, w_ple_proj, w_ple_gate):
    layer_weights = (g_mix_pre, g_mix_post, g_ffn_pre, g_ffn_post, g_ple, w_in, w_gla_gate_up, b_gla_gate,
                     g_gla_norm, w_gla_out, w_cmp1, b_cmp1, w_cmp2, b_cmp2, pe_cmp, w_nsa_out, w_out, w_ffn_up,
                     conv_ffn_w, b_conv_ffn, w_ffn_down, w_ple_proj, w_ple_gate)
    depth = w_in.shape[0]
    yp, ys = x_prompt, x_sample
    extras_p, extras_s = [], []
    for i in range(depth):
        w = _prep_weights(*[v[i] for v in layer_weights])
        yp, *ep = _layer_prompt(yp, p_prompt[i], w)
        ys, *es = _layer_sample(ys, p_sample[i], cache_cmp_kv[i], cache_slc_kv[i], cache_win_kv[i], state_gla[i],
                                state_ffn_conv[i], page_table, w)
        extras_p.append(ep)
        extras_s.append(es)
    stack = lambda rows, j: jnp.stack([r[j] for r in rows])
    outs = [yp, ys]
    for j in range(5):
        outs += [stack(extras_p, j), stack(extras_s, j)]
    return tuple(outs)
```

```python
import functools

import numpy as np
import jax
import jax.numpy as jnp
from jax import lax
from jax.experimental import pallas as pl
from jax.experimental.pallas import tpu as pltpu

F32 = jnp.float32
BF16 = jnp.bfloat16

D_MODEL = 1024
PAGE_SIZE = 128
GLA_HEADS = 4
GLA_DK = 128
GLA_DV = 256
GLA_GATE_RANK = 16
GLA_TAU = 16.0
GLA_SUB = 16
NSA_HEADS = 16
NSA_KV_GROUPS = 4
NSA_GROUP_HEADS = 4
NSA_DH = 64
NSA_KV_W = 2 * NSA_KV_GROUPS * NSA_DH
CMP_LEN = 32
CMP_STRIDE = 16
CMP_HIDDEN = 128
SLC_BLOCK = 64
SLC_TOPK = 16
WINDOW = 512
FORCE_SCORE = 1.0e4
D_FF = 2816
CONV_W = 3
PLE_DIM = 256
EPS = 1e-6

V7X_VMEM_BYTES = 64 * 1024 * 1024
VMEM_LIMIT = V7X_VMEM_BYTES - 8 * 1024 * 1024

_GLA_OFF, _GLA_W = 0, 2 * GLA_HEADS * GLA_DK + 2 * GLA_HEADS * GLA_DV
_SMALL_OFF, _SMALL_W = _GLA_OFF + _GLA_W, 128
_NQ_OFF, _NQ_W = _SMALL_OFF + _SMALL_W, NSA_HEADS * NSA_DH
_NK_OFF, _NK_W = _NQ_OFF + _NQ_W, 3 * NSA_KV_W
_MG_OFF, _MG_W = _NK_OFF + _NK_W, 2 * D_MODEL
_PACK_W = _MG_OFF + _MG_W
_NG_LANE0 = GLA_GATE_RANK

_Q_SCALE = float(NSA_DH ** -0.5 * np.log2(np.e))

_MASK_BIG = float(2.0 ** 100)


def _cparams(*sem):
    return pltpu.CompilerParams(dimension_semantics=sem, vmem_limit_bytes=VMEM_LIMIT)


def _const_spec(shape):
    nd = len(shape)
    return pl.BlockSpec(shape, lambda *_: (0,) * nd, pipeline_mode=pl.Buffered(1))


def _rms(x, g):
    return x * lax.rsqrt(jnp.mean(x * x, axis=-1, keepdims=True) + EPS) * g


def _gelu_tanh(x):
    return 0.5 * x * (1.0 + jnp.tanh(np.sqrt(2.0 / np.pi).astype(np.float32) * (x + 0.044715 * (x * x * x))))


def _sigmoid(x):
    return 1.0 / (1.0 + jnp.exp(-x))


def _dot(a, b):
    return jnp.dot(a, b, preferred_element_type=F32)


def _dot_nt(a, b):
    return lax.dot_general(a, b, (((1,), (1,)), ((), ())), preferred_element_type=F32)


def _dot_tn(a, b):
    return lax.dot_general(a, b, (((0,), (0,)), ((), ())), preferred_element_type=F32)


def _masked_softmax(s, mask):
    s = jnp.where(mask, s, -jnp.inf)
    m = jnp.max(s, axis=-1, keepdims=True)
    m = jnp.where(m > -jnp.inf, m, 0.0)
    e = jnp.exp2(s - m)
    d = jnp.sum(e, axis=-1, keepdims=True)
    return e / jnp.where(d > 0, d, 1.0)


def _lane_column(x, col):
    lane = lax.broadcasted_iota(jnp.int32, x.shape, 1)
    return jnp.sum(jnp.where(lane == col, x, 0.0), axis=-1, keepdims=True)


def _pack_w_in(w_in):
    sizes = (512, 512, 1024, 1024, GLA_GATE_RANK, 1024, NSA_KV_W, NSA_KV_W, NSA_KV_W, NSA_HEADS * 3, 2 * D_MODEL)
    pts = [int(v) for v in np.cumsum(sizes)[:-1]]
    gq, gk, gv, gr, ga, nq, nkc, nks, nkw, ng, mg = jnp.split(w_in, pts, axis=1)
    small = jnp.concatenate([ga, ng, jnp.zeros((D_MODEL, _SMALL_W - GLA_GATE_RANK - NSA_HEADS * 3), w_in.dtype)], axis=1)
    return jnp.concatenate([gq, gk, gv, gr, small, nq, nkc, nks, nkw, mg], axis=1).astype(BF16)


def _pack_w_kv_t(w_in):
    lo = 512 + 512 + 1024 + 1024 + GLA_GATE_RANK + NSA_HEADS * NSA_DH
    return w_in[:, lo:lo + _NK_W].T.astype(BF16)


def _proj_in_kernel(x_ref, g_ref, w_ref, *refs, kv_major):
    h = _rms(x_ref[...], g_ref[...]).astype(BF16)

    def seg(off, width):
        return _dot(h, w_ref[:, off:off + width])

    if kv_major:
        wkv_ref, gla_ref, small_ref, nq_ref, kc_ref, ks_ref, kw_ref, nkb_ref, mg_ref = refs
        nk = _dot_nt(wkv_ref[...], h)
        kc_ref[0] = nk[0:NSA_KV_W]
        ks_ref[0] = nk[NSA_KV_W:2 * NSA_KV_W]
        kw_ref[0] = nk[2 * NSA_KV_W:3 * NSA_KV_W]
        nkb_ref[0] = nk.astype(BF16)
    else:
        gla_ref, small_ref, nq_ref, kc_ref, ks_ref, kw_ref, mg_ref = refs
        nk = seg(_NK_OFF, _NK_W)
        kc_ref[...] = nk[:, 0:NSA_KV_W]
        ks_ref[...] = nk[:, NSA_KV_W:2 * NSA_KV_W]
        kw_ref[...] = nk[:, 2 * NSA_KV_W:3 * NSA_KV_W]
    gla_ref[...] = seg(_GLA_OFF, _GLA_W)
    small_ref[...] = seg(_SMALL_OFF, _SMALL_W)
    nq_ref[...] = (seg(_NQ_OFF, _NQ_W) * _Q_SCALE).astype(BF16)
    mg_ref[...] = seg(_MG_OFF, _MG_W)


def _proj_in(x, g_pre, w_pack, tm, w_kv_t=None, seq=None):
    m = x.shape[0]
    assert m % tm == 0
    kv_major = w_kv_t is not None
    row = lambda w: pl.BlockSpec((tm, w), lambda i: (i, 0))
    sds = jax.ShapeDtypeStruct
    in_specs = [row(D_MODEL), _const_spec((1, D_MODEL)), _const_spec((D_MODEL, _PACK_W))]
    args = [x, g_pre.reshape(1, D_MODEL), w_pack]
    head = [(sds((m, _GLA_W), F32), row(_GLA_W)), (sds((m, _SMALL_W), F32), row(_SMALL_W)),
            (sds((m, _NQ_W), BF16), row(_NQ_W))]
    if kv_major:
        assert seq % tm == 0
        tps = seq // tm
        col = lambda w: pl.BlockSpec((1, w, tm), lambda i: (i // tps, 0, i % tps))
        in_specs.append(_const_spec((_NK_W, D_MODEL)))
        args.append(w_kv_t)
        kv = [(sds((m // seq, NSA_KV_W, seq), F32), col(NSA_KV_W))] * 3 + [(sds((m // seq, _NK_W, seq), BF16), col(_NK_W))]
    else:
        kv = [(sds((m, NSA_KV_W), F32), row(NSA_KV_W))] * 3
    outs = head + kv + [(sds((m, _MG_W), F32), row(_MG_W))]
    return pl.pallas_call(
        functools.partial(_proj_in_kernel, kv_major=kv_major),
        out_shape=tuple(o[0] for o in outs), grid=(m // tm,), in_specs=in_specs,
        out_specs=tuple(o[1] for o in outs),
        compiler_params=_cparams("parallel"), name="proj_in",
    )(*args)


def _log_decay(small, wg, bg):
    x = _dot(small.astype(BF16), wg) + bg
    return (jnp.minimum(x, 0.0) - jnp.log1p(jnp.exp(-jnp.abs(x)))) * (1.0 / GLA_TAU)


def _pack_w_gate(w_gate_up):
    pad = jnp.zeros((_SMALL_W - GLA_GATE_RANK, GLA_HEADS * GLA_DK), w_gate_up.dtype)
    return jnp.concatenate([w_gate_up, pad], axis=0).astype(BF16)


def _gla_out(o, gn, r):
    o = o * lax.rsqrt(jnp.mean(o * o, axis=-1, keepdims=True) + EPS) * gn
    return (o * (r * _sigmoid(r))).astype(BF16)


def _gla_chunk_kernel(q_ref, k_ref, v_ref, r_ref, small_ref, wg_ref, bg_ref, gn_ref, og_ref, st_ref,
                      s_scr, kp_scr, bp_scr, *, chunk):
    c = pl.program_id(2)
    C, SUB = chunk, GLA_SUB

    @pl.when(c == 0)
    def _():
        s_scr[...] = jnp.zeros_like(s_scr)
        kp_scr[0:SUB, :] = jnp.zeros((SUB, GLA_DK), F32)
        bp_scr[0:SUB, :] = jnp.zeros((SUB, GLA_DK), F32)

    q = q_ref[...] * (GLA_DK ** -0.5)
    k = k_ref[...]
    vb = v_ref[...].astype(BF16)
    la = _log_decay(small_ref[...], wg_ref[...], bg_ref[...])

    ri = lax.broadcasted_iota(jnp.int32, (C, C), 0)
    ci = lax.broadcasted_iota(jnp.int32, (C, C), 1)
    tri = jnp.where(ci <= ri, 1.0, 0.0).astype(BF16)
    hi = la.astype(BF16)
    r1 = la - hi.astype(F32)
    mid = r1.astype(BF16)
    lo = (r1 - mid.astype(F32)).astype(BF16)
    b = _dot(tri, hi) + _dot(tri, mid) + _dot(tri, lo)

    st = s_scr[...]
    o = _dot_nt((q * jnp.exp(b)).astype(BF16), st.astype(BF16))
    b_last = b[C - 1:C, :]
    kdec = (k * jnp.exp(b_last - b)).astype(BF16)
    s_scr[...] = st * jnp.exp(b_last) + _dot_tn(vb, kdec)

    row = lax.broadcasted_iota(jnp.int32, (C, GLA_DK), 0)
    blocks = [jnp.zeros((SUB, C), F32)]
    for i in range(1, C // SUB):
        beta = b[SUB * i - 1:SUB * i, :]
        qi = (q[SUB * i:SUB * (i + 1)] * jnp.exp(b[SUB * i:SUB * (i + 1)] - beta)).astype(BF16)
        ki = (k * jnp.exp(jnp.where(row < SUB * i, beta - b, -jnp.inf))).astype(BF16)
        blocks.append(_dot_nt(qi, ki))
    a = jnp.concatenate(blocks, axis=0)

    kp_scr[SUB:SUB + C, :] = k
    bp_scr[SUB:SUB + C, :] = b
    tmod = ri % SUB
    for j in range(SUB):
        ks = kp_scr[SUB - j:SUB - j + C, :]
        bs = bp_scr[SUB - j:SUB - j + C, :]
        aj = jnp.sum(q * ks * jnp.exp(b - bs), axis=-1, keepdims=True)
        a = a + jnp.where((ci == ri - j) & (tmod >= j), aj, 0.0)
    o = o + _dot(a.astype(BF16), vb)

    og_ref[...] = _gla_out(o, gn_ref[...], r_ref[...])

    @pl.when(c == pl.num_programs(2) - 1)
    def _():
        st_ref[0, 0] = s_scr[...].T


def _gla_prompt(gla, small, w_gate_pack, b_gate, g_norm, batch, seq, chunk=128):
    n = seq // chunk
    assert seq % chunk == 0 and chunk % GLA_SUB == 0
    H = GLA_HEADS
    rows = lambda b, h, c: b * n + c
    kern = functools.partial(_gla_chunk_kernel, chunk=chunk)
    return pl.pallas_call(
        kern,
        out_shape=(jax.ShapeDtypeStruct((batch * seq, H * GLA_DV), BF16),
                   jax.ShapeDtypeStruct((batch, H, GLA_DK, GLA_DV), F32)),
        grid=(batch, H, n),
        in_specs=[
            pl.BlockSpec((chunk, GLA_DK), lambda b, h, c: (rows(b, h, c), h)),
            pl.BlockSpec((chunk, GLA_DK), lambda b, h, c: (rows(b, h, c), H + h)),
            pl.BlockSpec((chunk, GLA_DV), lambda b, h, c: (rows(b, h, c), H + h)),
            pl.BlockSpec((chunk, GLA_DV), lambda b, h, c: (rows(b, h, c), 2 * H + h)),
            pl.BlockSpec((chunk, _SMALL_W), lambda b, h, c: (rows(b, h, c), 0)),
            pl.BlockSpec((_SMALL_W, GLA_DK), lambda b, h, c: (0, h)),
            pl.BlockSpec((1, GLA_DK), lambda b, h, c: (0, h)),
            pl.BlockSpec((1, GLA_DV), lambda b, h, c: (0, h)),
        ],
        out_specs=(pl.BlockSpec((chunk, GLA_DV), lambda b, h, c: (rows(b, h, c), h)),
                   pl.BlockSpec((1, 1, GLA_DK, GLA_DV), lambda b, h, c: (b, h, 0, 0))),
        scratch_shapes=[pltpu.VMEM((GLA_DV, GLA_DK), F32),
                        pltpu.VMEM((GLA_SUB + chunk, GLA_DK), F32),
                        pltpu.VMEM((GLA_SUB + chunk, GLA_DK), F32)],
        compiler_params=_cparams("parallel", "parallel", "arbitrary"), name="gla_chunk",
    )(gla, gla, gla, gla, small, w_gate_pack, b_gate.reshape(1, -1), g_norm.reshape(1, -1))


def _pack_w_cmp1(w1):
    w = w1.reshape(2, 2, CMP_STRIDE, NSA_DH, CMP_HIDDEN)
    per_c = jnp.concatenate([w[:, 0], w[:, 1]], axis=-1)
    z = jnp.zeros_like(per_c[0])
    w16 = jnp.concatenate([jnp.concatenate([per_c[0], z], axis=-1),
                           jnp.concatenate([z, per_c[1]], axis=-1)], axis=1)
    return w16.reshape(CMP_STRIDE // 2, 4 * NSA_DH, 4 * CMP_HIDDEN).astype(BF16)


def _compress_kernel(pt_ref, kv_hbm, wp_ref, w1_ref, pef_ref, b1_ref, w2_ref, b2_ref, out_ref,
                     buf, tok_scr, hb_scr, sem, *, n_pages, pages_per_row):
    bidx = pl.program_id(0)
    n_chunk = n_pages * (PAGE_SIZE // CMP_STRIDE)
    GW = 2 * NSA_DH

    def page_copy(p):
        page = pt_ref[bidx, p]
        off = pl.multiple_of((page % pages_per_row) * PAGE_SIZE, PAGE_SIZE)
        src = kv_hbm.at[page // pages_per_row, :, :, :, pl.ds(off, PAGE_SIZE)]
        return pltpu.make_async_copy(src, buf.at[:, :, :, pl.ds(pl.multiple_of(p * PAGE_SIZE, PAGE_SIZE), PAGE_SIZE)], sem)

    def start(p, carry):
        page_copy(p).start()
        return carry

    def wait(p, carry):
        page_copy(p).wait()
        return carry

    lax.fori_loop(0, n_pages, start, 0)
    lax.fori_loop(0, n_pages, wait, 0)

    HID = CMP_HIDDEN
    hb_scr[n_chunk:n_chunk + 8, :] = jnp.zeros((8, HID), F32)
    const = [_dot(pef_ref[c].astype(BF16), w1_ref[c]) + b1_ref[c] for c in range(2)]
    for g in range(NSA_KV_GROUPS):

        def to_token_major(p, carry):
            off = pl.multiple_of(p * PAGE_SIZE, PAGE_SIZE)
            tok_scr[pl.ds(off, PAGE_SIZE), :] = buf[g, :, :, pl.ds(off, PAGE_SIZE)].reshape(GW, PAGE_SIZE).T
            return carry

        lax.fori_loop(0, n_pages, to_token_major, 0, unroll=8)
        acc = jnp.zeros((n_chunk, 4 * HID), F32)
        for p in range(0, CMP_STRIDE, 2):
            xs = [tok_scr[pl.ds(p + i, n_chunk, stride=CMP_STRIDE), :].astype(BF16) for i in range(2)]
            acc = acc + _dot(jnp.concatenate(xs, axis=1), wp_ref[p // 2])
        for c in range(2):
            hb_scr[0:n_chunk, :] = acc[:, (2 * c + 1) * HID:(2 * c + 2) * HID]
            hid = acc[:, 2 * c * HID:(2 * c + 1) * HID] + hb_scr[pl.ds(1, n_chunk), :] + const[c]
            out_ref[0, g * 2 + c] = _dot(_gelu_tanh(hid).astype(BF16), w2_ref[c]) + b2_ref[c]


def _compress(page_table, kv_t, wp, w1b, pe_flat, b1, w2b, b2):
    batch, n_pages = page_table.shape
    n_chunk = n_pages * (PAGE_SIZE // CMP_STRIDE)
    assert kv_t.shape[-1] % PAGE_SIZE == 0
    kern = functools.partial(_compress_kernel, n_pages=n_pages, pages_per_row=kv_t.shape[-1] // PAGE_SIZE)
    grid_spec = pltpu.PrefetchScalarGridSpec(
        num_scalar_prefetch=1, grid=(batch,),
        in_specs=[pl.BlockSpec(memory_space=pl.ANY),
                  _const_spec((CMP_STRIDE // 2, 4 * NSA_DH, 4 * CMP_HIDDEN)),
                  _const_spec((2, CMP_LEN * NSA_DH, CMP_HIDDEN)),
                  _const_spec((2, 1, CMP_LEN * NSA_DH)),
                  _const_spec((2, 1, CMP_HIDDEN)),
                  _const_spec((2, CMP_HIDDEN, NSA_DH)),
                  _const_spec((2, 1, NSA_DH))],
        out_specs=pl.BlockSpec((1, 2 * NSA_KV_GROUPS, n_chunk, NSA_DH), lambda b, pt: (b, 0, 0, 0)),
        scratch_shapes=[pltpu.VMEM((NSA_KV_GROUPS, 2, NSA_DH, n_pages * PAGE_SIZE), F32),
                        pltpu.VMEM((n_pages * PAGE_SIZE, 2 * NSA_DH), F32),
                        pltpu.VMEM((n_chunk + 8, CMP_HIDDEN), F32),
                        pltpu.SemaphoreType.DMA(())])
    return pl.pallas_call(
        kern, out_shape=jax.ShapeDtypeStruct((batch, 2 * NSA_KV_GROUPS, n_chunk, NSA_DH), F32),
        grid_spec=grid_spec, compiler_params=_cparams("arbitrary"), name="compress",
    )(page_table, kv_t, wp, w1b, pe_flat, b1, w2b, b2)


def _feature_major(kv):
    return kv.transpose(0, 2, 3, 4, 1)


def _token_major(kv_t, batch, seq):
    return kv_t.reshape(batch, NSA_KV_GROUPS, 2, NSA_DH, seq).transpose(0, 4, 1, 2, 3)


def _permute_cmp(kvc):
    b, gc, n, d = kvc.shape
    return kvc.reshape(b, gc, n // 4, 4, d).transpose(0, 1, 3, 2, 4).reshape(b, gc, n, d)


def _group_queries(qall):
    return jnp.concatenate([qall[:, r * NSA_DH:(r + 1) * NSA_DH] for r in range(NSA_GROUP_HEADS)], axis=0)


def _block_importance(psum, n_slc):
    p0, p1, p2, p3 = (psum[:, m * n_slc:(m + 1) * n_slc] for m in range(4))
    lane = lax.broadcasted_iota(jnp.int32, p3.shape, 1)
    p3s = jnp.where(lane == 0, 0.0, pltpu.roll(p3, 1, axis=1))
    return ((((((p0 + p3s) + p1) + p0) + p2) + p1) + p3) + p2


def _top_blocks(score, count, sel):
    lane = lax.broadcasted_iota(jnp.int32, score.shape, 1)
    n = score.shape[-1]
    for _ in range(count):
        mx = jnp.max(score, axis=-1, keepdims=True)
        idx = jnp.min(jnp.where(score == mx, lane, n), axis=-1, keepdims=True)
        pick = (lane == idx) & (mx > -jnp.inf)
        sel = jnp.where(pick, 1.0, sel)
        score = jnp.where(pick, -jnp.inf, score)
    return sel


_N_FORCED = 3


def _cmp_topk_kernel(q_ref, kc_ref, vc_ref, small_ref, oc_ref, sel_ref, imp_scr, *, tq, sub, n_slc):
    g = pl.program_id(1)
    q0 = pl.program_id(2) * tq
    rows = NSA_GROUP_HEADS * sub
    n_cmp = 4 * n_slc
    kc = kc_ref[0, 0].astype(BF16)
    vc = vc_ref[0, 0].astype(BF16)
    lane = lax.broadcasted_iota(jnp.int32, (1, n_cmp), 1)
    blk_end = (4 * (lane % n_slc) + lane // n_slc) * CMP_STRIDE + (CMP_LEN - 1)
    for j in range(tq // sub):
        js = slice(j * sub, (j + 1) * sub)
        q = _group_queries(q_ref[js, :])
        s = _dot_nt(q, kc)
        t_rows = (q0 + j * sub) + lax.broadcasted_iota(jnp.int32, (rows, 1), 0) % sub
        p = _masked_softmax(s, blk_end <= t_rows)
        oc = _dot(p.astype(BF16), vc)
        psum = ((p[0:sub] + p[sub:2 * sub]) + p[2 * sub:3 * sub]) + p[3 * sub:4 * sub]
        imp_scr[js, :] = _block_importance(psum, n_slc)
        sig = _sigmoid(small_ref[js, :])
        outs = []
        for r in range(NSA_GROUP_HEADS):
            gate = _lane_column(sig, _NG_LANE0 + (g * NSA_GROUP_HEADS + r) * 3)
            outs.append(gate * oc[r * sub:(r + 1) * sub])
        oc_ref[js, :] = jnp.concatenate(outs, axis=1)

    cur = (q0 + lax.broadcasted_iota(jnp.int32, (tq, 1), 0)) // SLC_BLOCK
    blk = lax.broadcasted_iota(jnp.int32, (1, n_slc), 1)
    forced = (blk == 0) | (blk == cur) | (blk == cur - 1)
    valid = blk <= cur
    score = jnp.where(valid & jnp.logical_not(forced), imp_scr[...], -jnp.inf)
    sel = _top_blocks(score, min(SLC_TOPK, n_slc) - _N_FORCED, jnp.where(valid & forced, 1.0, 0.0))
    sel_ref[0] = (sel - 1.0).astype(BF16)


def _cmp_topk_prompt(nq, kvc_perm, small, batch, seq, tq=512, sub=128):
    G = NSA_KV_GROUPS
    n_slc = seq // SLC_BLOCK
    assert kvc_perm.shape[2] == 4 * n_slc and seq % tq == 0 and tq % sub == 0 and n_slc >= _N_FORCED
    nqt = seq // tq
    gw = NSA_GROUP_HEADS * NSA_DH
    kern = functools.partial(_cmp_topk_kernel, tq=tq, sub=sub, n_slc=n_slc)
    return pl.pallas_call(
        kern,
        out_shape=(jax.ShapeDtypeStruct((batch * seq, NSA_HEADS * NSA_DH), F32),
                   jax.ShapeDtypeStruct((batch * G, seq, n_slc), BF16)),
        grid=(batch, G, nqt),
        in_specs=[pl.BlockSpec((tq, gw), lambda b, g, i: (b * nqt + i, g)),
                  pl.BlockSpec((1, 1, 4 * n_slc, NSA_DH), lambda b, g, i: (b, 2 * g, 0, 0)),
                  pl.BlockSpec((1, 1, 4 * n_slc, NSA_DH), lambda b, g, i: (b, 2 * g + 1, 0, 0)),
                  pl.BlockSpec((tq, _SMALL_W), lambda b, g, i: (b * nqt + i, 0))],
        out_specs=(pl.BlockSpec((tq, gw), lambda b, g, i: (b * nqt + i, g)),
                   pl.BlockSpec((1, tq, n_slc), lambda b, g, i: (b * G + g, i, 0))),
        scratch_shapes=[pltpu.VMEM((tq, n_slc), F32)],
        compiler_params=_cparams("parallel", "parallel", "parallel"), name="cmp_topk",
    )(nq, kvc_perm, kvc_perm, small)


def _slc_win_kernel(q_ref, sel_ref, ks_ref, kw_ref, small_ref, oc_ref, on_ref, *, tq, tk, n_slc):
    g = pl.program_id(1)
    q0 = pl.program_id(2) * tq
    rows = NSA_GROUP_HEADS * tq
    DH = NSA_DH
    q = _group_queries(q_ref[...])
    qpad = jnp.concatenate([q, jnp.zeros((rows, DH), BF16)], axis=1)
    sel4 = jnp.concatenate([sel_ref[0]] * NSA_GROUP_HEADS, axis=0)
    qaug = jnp.concatenate([qpad, sel4], axis=1)
    t_rows = q0 + lax.broadcasted_iota(jnp.int32, (rows, 1), 0) % tq

    is_k = lax.broadcasted_iota(jnp.int32, (2 * DH, tk), 0) < DH
    blk_r = lax.broadcasted_iota(jnp.int32, (n_slc, tk), 0)
    tok_c = lax.broadcasted_iota(jnp.int32, (n_slc, tk), 1)
    tok_l = lax.broadcasted_iota(jnp.int32, (1, tk), 1)

    def tile(kt, m, acc, causal):
        k0 = pl.multiple_of(kt * tk, tk)
        kv = ks_ref[0, :, pl.ds(k0, tk)]
        onehot = jnp.where(blk_r == (k0 + tok_c) // SLC_BLOCK, _MASK_BIG, 0.0).astype(BF16)
        kaug = jnp.concatenate([jnp.where(is_k, kv, jnp.zeros_like(kv)), onehot], axis=0)
        s = _dot(qaug, kaug)
        if causal:
            s = jnp.where(k0 + tok_l <= t_rows, s, -_MASK_BIG)
        m_new = jnp.maximum(m, jnp.max(s, axis=-1, keepdims=True))
        p = jnp.exp2(s - m_new)
        vaug = jnp.where(is_k, jnp.ones_like(kv), kv)
        acc = jnp.exp2(m - m_new) * acc + _dot_nt(p.astype(BF16), vaug)
        return m_new, acc

    last = (q0 + tq - 1) // tk
    m0 = jnp.full((rows, 1), -1e30, F32)
    acc0 = jnp.zeros((rows, 2 * DH), F32)
    m, acc = lax.fori_loop(0, last, lambda kt, c: tile(kt, c[0], c[1], False), (m0, acc0))
    m, acc = tile(last, m, acc, True)
    o_s = acc[:, DH:] / acc[:, 0:DH]

    wlen = WINDOW + tq
    w0 = pl.multiple_of(jnp.maximum(q0 - WINDOW, 0), tq)
    kvw = kw_ref[0, :, pl.ds(w0, wlen)]
    row_w = lax.broadcasted_iota(jnp.int32, (2 * DH, wlen), 0)
    s_w = _dot(qpad, jnp.where(row_w < DH, kvw, jnp.zeros_like(kvw)))
    pos = w0 + lax.broadcasted_iota(jnp.int32, (1, wlen), 1)
    p_w = _masked_softmax(s_w, (pos <= t_rows) & (pos > t_rows - WINDOW))
    o_w = _dot_nt(p_w.astype(BF16), kvw)[:, DH:]

    sig = _sigmoid(small_ref[...])
    oc = oc_ref[...]
    outs = []
    for r in range(NSA_GROUP_HEADS):
        lane0 = _NG_LANE0 + (g * NSA_GROUP_HEADS + r) * 3
        g_s = _lane_column(sig, lane0 + 1)
        g_w = _lane_column(sig, lane0 + 2)
        rs = slice(r * tq, (r + 1) * tq)
        outs.append(oc[:, r * DH:(r + 1) * DH] + g_s * o_s[rs] + g_w * o_w[rs])
    on_ref[...] = jnp.concatenate(outs, axis=1).astype(BF16)


def _slc_win_prompt(nq, selm, nkb, small, oc, batch, seq, tq=256, tk=512):
    G = NSA_KV_GROUPS
    n_slc = seq // SLC_BLOCK
    assert seq % tk == 0 and seq % tq == 0 and tk % tq == 0 and seq >= WINDOW + tq
    nqt = seq // tq
    gw = NSA_GROUP_HEADS * NSA_DH
    kvw = 2 * NSA_DH
    kern = functools.partial(_slc_win_kernel, tq=tq, tk=tk, n_slc=n_slc)
    return pl.pallas_call(
        kern, out_shape=jax.ShapeDtypeStruct((batch * seq, NSA_HEADS * NSA_DH), BF16),
        grid=(batch, G, nqt),
        in_specs=[pl.BlockSpec((tq, gw), lambda b, g, i: (b * nqt + i, g)),
                  pl.BlockSpec((1, tq, n_slc), lambda b, g, i: (b * G + g, i, 0)),
                  pl.BlockSpec((1, kvw, seq), lambda b, g, i: (b, G + g, 0)),
                  pl.BlockSpec((1, kvw, seq), lambda b, g, i: (b, 2 * G + g, 0)),
                  pl.BlockSpec((tq, _SMALL_W), lambda b, g, i: (b * nqt + i, 0)),
                  pl.BlockSpec((tq, gw), lambda b, g, i: (b * nqt + i, g))],
        out_specs=pl.BlockSpec((tq, gw), lambda b, g, i: (b * nqt + i, g)),
        compiler_params=_cparams("parallel", "parallel", "parallel"), name="slc_win",
    )(nq, selm, nkb, nkb, small, oc)


def _mix_out_kernel(og_ref, on_ref, mg_ref, x_ref, wgo_ref, wno_ref, wo_ref, gpost_ref, y_ref):
    yg = _dot(og_ref[...], wgo_ref[...])
    yn = _dot(on_ref[...], wno_ref[...])
    mg = mg_ref[...]
    mix = _sigmoid(mg[:, 0:D_MODEL]) * yg + _sigmoid(mg[:, D_MODEL:]) * yn
    z = _dot(mix.astype(BF16), wo_ref[...])
    y_ref[...] = x_ref[...] + _rms(z, gpost_ref[...])


def _mix_out(og, on, mg, x, wgo, wno, wo, g_post, tm):
    m = x.shape[0]
    row = lambda w: pl.BlockSpec((tm, w), lambda i: (i, 0))
    wspec = _const_spec((D_MODEL, D_MODEL))
    return pl.pallas_call(
        _mix_out_kernel, out_shape=jax.ShapeDtypeStruct((m, D_MODEL), F32), grid=(m // tm,),
        in_specs=[row(D_MODEL), row(D_MODEL), row(2 * D_MODEL), row(D_MODEL), wspec, wspec, wspec,
                  _const_spec((1, D_MODEL))],
        out_specs=row(D_MODEL), compiler_params=_cparams("parallel"), name="mix_out",
    )(og, on, mg, x, wgo, wno, wo, g_post.reshape(1, D_MODEL))


_FFN_HALO = 16


def _ffn_tail(x, conv, bgate, p, wdn_ref, gpost_ref, gple_ref, wpg_ref, wpp_ref):
    y = _dot((_gelu_tanh(conv) * bgate).astype(BF16), wdn_ref[...])
    x2 = x + _rms(y, gpost_ref[...])
    gate = _sigmoid(_dot(_rms(x2, gple_ref[...]).astype(BF16), wpg_ref[...]))
    return x2 + gate * _dot(p.astype(BF16), wpp_ref[...])


def _ffn_prompt_kernel(x_ref, xh_ref, p_ref, gpre_ref, wup_ref, cw_ref, cb_ref, wdn_ref, gpost_ref, gple_ref,
                       wpg_ref, wpp_ref, y_ref, alast_ref, h_scr, a_scr, *, tm, tiles_per_seq):
    H = _FFN_HALO
    x = x_ref[...]
    h_scr[0:H, :] = _rms(xh_ref[...], gpre_ref[...]).astype(BF16)
    h_scr[H:H + tm, :] = _rms(x, gpre_ref[...]).astype(BF16)
    ab = _dot(h_scr[...], wup_ref[...])
    keep = jnp.where(pl.program_id(0) % tiles_per_seq == 0, 0.0, 1.0)
    a_scr[0:H, :] = ab[0:H, 0:D_FF] * keep
    a_scr[H:H + tm, :] = ab[H:, 0:D_FF]
    cw = cw_ref[...]
    conv = (a_scr[H - 2:H - 2 + tm, :] * cw[0:1] + a_scr[H - 1:H - 1 + tm, :] * cw[1:2]
            + a_scr[H:H + tm, :] * cw[2:3]) + cb_ref[...]
    y_ref[...] = _ffn_tail(x, conv, ab[H:, D_FF:], p_ref[...], wdn_ref, gpost_ref, gple_ref, wpg_ref, wpp_ref)
    alast_ref[0] = a_scr[H + tm - 8:H + tm, :]


def _ffn_weight_specs():
    return [_const_spec((1, D_MODEL)), _const_spec((D_MODEL, 2 * D_FF)), _const_spec((CONV_W, D_FF)),
            _const_spec((1, D_FF)), _const_spec((D_FF, D_MODEL)), _const_spec((1, D_MODEL)),
            _const_spec((1, D_MODEL)), _const_spec((D_MODEL, D_MODEL)), _const_spec((PLE_DIM, D_MODEL))]


def _ffn_prompt(x, p, fw, batch, seq, tm=256):
    assert seq % tm == 0 and tm % _FFN_HALO == 0
    tps = seq // tm
    hb = tm // _FFN_HALO
    kern = functools.partial(_ffn_prompt_kernel, tm=tm, tiles_per_seq=tps)
    return pl.pallas_call(
        kern,
        out_shape=(jax.ShapeDtypeStruct((batch * seq, D_MODEL), F32), jax.ShapeDtypeStruct((batch, 8, D_FF), F32)),
        grid=(batch * tps,),
        in_specs=[pl.BlockSpec((tm, D_MODEL), lambda i: (i, 0)),
                  pl.BlockSpec((_FFN_HALO, D_MODEL), lambda i: (jnp.maximum(i * hb - 1, 0), 0)),
                  pl.BlockSpec((tm, PLE_DIM), lambda i: (i, 0))] + _ffn_weight_specs(),
        out_specs=(pl.BlockSpec((tm, D_MODEL), lambda i: (i, 0)),
                   pl.BlockSpec((1, 8, D_FF), lambda i: (i // tps, 0, 0))),
        scratch_shapes=[pltpu.VMEM((_FFN_HALO + tm, D_MODEL), BF16), pltpu.VMEM((_FFN_HALO + tm, D_FF), F32)],
        compiler_params=_cparams("arbitrary"), name="ffn_prompt",
    )(x, x, p, *fw)


def _layer_prompt(x, p, w, tm=256):
    batch, seq, _ = x.shape
    m = batch * seq
    x2 = x.reshape(m, D_MODEL)
    gla, small, nq, kc_t, ks_t, kw_t, nkb_t, mg = _proj_in(x2, w["g_mix_pre"], w["w_in_pack"], tm, w["w_kv_t"], seq)
    og, gla_state = _gla_prompt(gla, small, w["w_gate_pack"], w["b_gla_gate"], w["g_gla_norm"], batch, seq)

    n_pages = seq // PAGE_SIZE
    ident_pt = jnp.arange(batch * n_pages, dtype=jnp.int32).reshape(batch, n_pages)
    kvc = _compress(ident_pt, kc_t.reshape(batch, NSA_KV_GROUPS, 2, NSA_DH, seq), *w["cmp"])
    oc, selm = _cmp_topk_prompt(nq, _permute_cmp(kvc), small, batch, seq)
    on = _slc_win_prompt(nq, selm, nkb_t, small, oc, batch, seq)

    x1 = _mix_out(og, on, mg, x2, w["w_gla_out"], w["w_nsa_out"], w["w_out"], w["g_mix_post"], tm)
    y, alast = _ffn_prompt(x1, p.reshape(m, PLE_DIM), w["ffn"], batch, seq, tm)

    win = min(WINDOW, seq)
    return (y.reshape(batch, seq, D_MODEL), _token_major(kc_t, batch, seq), _token_major(ks_t, batch, seq),
            _token_major(kw_t[:, :, seq - win:], batch, win), gla_state, alast[:, 8 - (CONV_W - 1):])


def _row_to_col(row):
    n = row.shape[-1]
    ri = lax.broadcasted_iota(jnp.int32, (n, n), 0)
    ci = lax.broadcasted_iota(jnp.int32, (n, n), 1)
    return jnp.sum(jnp.where(ri == ci, jnp.broadcast_to(row, (n, n)), 0.0), axis=-1, keepdims=True)


def _gla_step_kernel(gla_ref, small_ref, s0_ref, wg_ref, bg_ref, gn_ref, og_ref, s1_ref):
    row = gla_ref[0]
    small = small_ref[0]
    H, DK, DV = GLA_HEADS, GLA_DK, GLA_DV
    outs = []
    for h in range(H):
        q = row[:, h * DK:(h + 1) * DK] * (DK ** -0.5)
        k = row[:, H * DK + h * DK:H * DK + (h + 1) * DK]
        v = row[:, 2 * H * DK + h * DV:2 * H * DK + (h + 1) * DV]
        r = row[:, 2 * H * DK + H * DV + h * DV:2 * H * DK + H * DV + (h + 1) * DV]
        alpha = jnp.exp(_log_decay(small, wg_ref[:, h * DK:(h + 1) * DK], bg_ref[:, h * DK:(h + 1) * DK]))
        s0 = s0_ref[0, h]
        s1_ref[0, h] = _row_to_col(alpha) * s0 + _row_to_col(k) * v
        o = _dot((q * alpha).astype(BF16), s0.astype(BF16)) + jnp.sum(q * k, axis=-1, keepdims=True) * v
        outs.append(_gla_out(o, gn_ref[:, h * DV:(h + 1) * DV], r))
    og_ref[0] = jnp.concatenate(outs, axis=1)


def _gla_sample(gla, small, state, w_gate_pack, b_gate, g_norm):
    nb = gla.shape[0]
    H = GLA_HEADS
    og, s1 = pl.pallas_call(
        _gla_step_kernel,
        out_shape=(jax.ShapeDtypeStruct((nb, 1, H * GLA_DV), BF16), jax.ShapeDtypeStruct(state.shape, F32)),
        grid=(nb,),
        in_specs=[pl.BlockSpec((1, 1, _GLA_W), lambda b: (b, 0, 0)),
                  pl.BlockSpec((1, 1, _SMALL_W), lambda b: (b, 0, 0)),
                  pl.BlockSpec((1, H, GLA_DK, GLA_DV), lambda b: (b, 0, 0, 0)),
                  _const_spec((_SMALL_W, H * GLA_DK)), _const_spec((1, H * GLA_DK)), _const_spec((1, H * GLA_DV))],
        out_specs=(pl.BlockSpec((1, 1, H * GLA_DV), lambda b: (b, 0, 0)),
                   pl.BlockSpec((1, H, GLA_DK, GLA_DV), lambda b: (b, 0, 0, 0))),
        compiler_params=_cparams("parallel"), name="gla_step",
    )(gla.reshape(nb, 1, _GLA_W), small.reshape(nb, 1, _SMALL_W), state, w_gate_pack,
      b_gate.reshape(1, -1), g_norm.reshape(1, -1))
    return og.reshape(nb, H * GLA_DV), s1


def _head_group_rows(x_groups):
    grp = lax.broadcasted_iota(jnp.int32, x_groups[0].shape, 0) // NSA_GROUP_HEADS
    out = jnp.zeros(x_groups[0].shape, F32)
    for g, xg in enumerate(x_groups):
        out = jnp.where(grp == g, xg, out)
    return out


def _cmp_topk_sample_kernel(q_ref, kvc_ref, oc_ref, idx_ref, *, t, n_lane, n_pick):
    q = q_ref[0]
    n_cmp = kvc_ref.shape[2]
    lane = lax.broadcasted_iota(jnp.int32, (1, n_cmp), 1)
    blk_end = (4 * (lane % n_lane) + lane // n_lane) * CMP_STRIDE + (CMP_LEN - 1)
    ocs, imps = [], []
    for g in range(NSA_KV_GROUPS):
        s = _dot_nt(q, kvc_ref[0, 2 * g].astype(BF16))
        p = _masked_softmax(s, jnp.broadcast_to(blk_end <= t, s.shape))
        ocs.append(_dot(p.astype(BF16), kvc_ref[0, 2 * g + 1].astype(BF16)))
        r0 = NSA_GROUP_HEADS * g
        psum = ((p[r0:r0 + 1] + p[r0 + 1:r0 + 2]) + p[r0 + 2:r0 + 3]) + p[r0 + 3:r0 + 4]
        imps.append(_block_importance(psum, n_lane))
    oc_ref[0] = _head_group_rows(ocs)

    imp = jnp.concatenate(imps + [jnp.zeros((8 - NSA_KV_GROUPS, n_lane), F32)], axis=0)
    blk = lax.broadcasted_iota(jnp.int32, imp.shape, 1)
    cur = t // SLC_BLOCK
    forced = (blk == 0) | (blk == cur) | (blk == cur - 1)
    score = jnp.where(blk <= cur, jnp.where(forced, FORCE_SCORE, imp), -jnp.inf)
    idx_out = jnp.zeros(imp.shape, jnp.int32)
    for i in range(n_pick):
        mx = jnp.max(score, axis=-1, keepdims=True)
        idx = jnp.min(jnp.where(score == mx, blk, n_lane), axis=-1, keepdims=True)
        idx_out = jnp.where(blk == i, idx, idx_out)
        score = jnp.where(blk == idx, -jnp.inf, score)
    idx_ref[0] = idx_out


def _slc_win_sample_kernel(idx_ref, pt_ref, kv0_ref, kv1_ref, kv2_ref, kv3_ref, q_ref, ksn_ref, kwn_ref, cw_ref,
                           small_ref, oc_ref, on_ref, m_scr, acc_scr, *, t, win):
    b = pl.program_id(0)
    k = pl.program_id(1)
    DH = NSA_DH
    G = NSA_KV_GROUPS
    n_pick = pl.num_programs(1)
    q = q_ref[0]
    qf = q.astype(F32)
    qpad = jnp.concatenate([q, jnp.zeros_like(q)], axis=1)
    kv_refs = (kv0_ref, kv1_ref, kv2_ref, kv3_ref)
    is_k = lax.broadcasted_iota(jnp.int32, (2 * DH, PAGE_SIZE), 0) < DH
    half_of_lane = lax.broadcasted_iota(jnp.int32, (1, PAGE_SIZE), 1) // SLC_BLOCK
    lane_o = lax.broadcasted_iota(jnp.int32, (NSA_HEADS, 2 * DH), 1)

    @pl.when(k == 0)
    def _():
        m_scr[...] = jnp.full(m_scr.shape, -1e30, F32)
        acc_scr[...] = jnp.zeros_like(acc_scr)

    for g in range(G):
        kv = kv_refs[g][0, 0].reshape(2 * DH, PAGE_SIZE).astype(BF16)
        half = idx_ref[b, g * n_pick + k] % (PAGE_SIZE // SLC_BLOCK)
        s = _dot(qpad, jnp.where(is_k, kv, jnp.zeros_like(kv)))
        s = jnp.where(half_of_lane == half, s, -_MASK_BIG)
        m = m_scr[g]
        m_new = jnp.maximum(m, jnp.max(s, axis=-1, keepdims=True))
        p = jnp.exp2(s - m_new)
        acc_scr[g] = jnp.exp2(m - m_new) * acc_scr[g] + _dot_nt(p.astype(BF16), jnp.where(is_k, jnp.ones_like(kv), kv))
        m_scr[g] = m_new

    @pl.when(k == pl.num_programs(1) - 1)
    def _():
        ksn = ksn_ref[0]
        kwn = kwn_ref[0]
        o_s, o_w = [], []
        row_w = lax.broadcasted_iota(jnp.int32, (2 * DH, win), 0)
        pos = (t - win) + lax.broadcasted_iota(jnp.int32, (1, win), 1)
        w_valid = (pos <= t) & (pos > t - WINDOW) & (pos >= 0)
        for g in range(G):
            kn = ksn[:, g * 2 * DH:(g + 1) * 2 * DH]
            s_n = jnp.sum(qf * kn[:, 0:DH].astype(BF16).astype(F32), axis=-1, keepdims=True)
            m = m_scr[g]
            m_new = jnp.maximum(m, s_n)
            acc = jnp.exp2(m - m_new) * acc_scr[g] + jnp.exp2(s_n - m_new) * jnp.where(lane_o < DH, 1.0, kn)
            o_s.append(acc[:, DH:] / acc[:, 0:DH])
            kvw = cw_ref[0, g].reshape(2 * DH, win).astype(BF16)
            s_w = _dot(qpad, jnp.where(row_w < DH, kvw, jnp.zeros_like(kvw)))
            s_w = jnp.where(w_valid, s_w, -jnp.inf)
            wn = kwn[:, g * 2 * DH:(g + 1) * 2 * DH]
            s_wn = jnp.sum(qf * wn[:, 0:DH].astype(BF16).astype(F32), axis=-1, keepdims=True)
            mw = jnp.maximum(jnp.max(s_w, axis=-1, keepdims=True), s_wn)
            e = jnp.exp2(s_w - mw)
            en = jnp.exp2(s_wn - mw)
            num = _dot_nt(e.astype(BF16), kvw)[:, DH:] + en * wn[:, DH:]
            o_w.append(num / (jnp.sum(e, axis=-1, keepdims=True) + en))
        o_s = _head_group_rows(o_s)
        o_w = _head_group_rows(o_w)
        sig = jnp.broadcast_to(_sigmoid(small_ref[0]), (NSA_HEADS, _SMALL_W))
        lane = lax.broadcasted_iota(jnp.int32, sig.shape, 1)
        head = lax.broadcasted_iota(jnp.int32, sig.shape, 0)
        gate = [jnp.sum(jnp.where(lane == _NG_LANE0 + 3 * head + x, sig, 0.0), axis=-1, keepdims=True) for x in range(3)]
        on_ref[0] = (gate[0] * oc_ref[0] + gate[1] * o_s + gate[2] * o_w).astype(BF16)


def _nsa_sample(nq, ks_new, kw_new, small, kvc_perm, cache_slc, cache_win, page_table):
    nb, n_pages = page_table.shape
    past = n_pages * PAGE_SIZE
    t = past
    n_lane = past // SLC_BLOCK
    n_pick = SLC_TOPK - 1
    assert kvc_perm.shape[2] == 4 * n_lane and n_lane >= SLC_TOPK and n_lane % 128 == 0
    win = cache_win.shape[1]
    q3 = nq.reshape(nb, NSA_HEADS, NSA_DH)
    G = NSA_KV_GROUPS

    oc, idx = pl.pallas_call(
        functools.partial(_cmp_topk_sample_kernel, t=t, n_lane=n_lane, n_pick=n_pick),
        out_shape=(jax.ShapeDtypeStruct((nb, NSA_HEADS, NSA_DH), F32), jax.ShapeDtypeStruct((nb, 8, n_lane), jnp.int32)),
        grid=(nb,),
        in_specs=[pl.BlockSpec((1, NSA_HEADS, NSA_DH), lambda b: (b, 0, 0)),
                  pl.BlockSpec((1, 2 * G, 4 * n_lane, NSA_DH), lambda b: (b, 0, 0, 0))],
        out_specs=(pl.BlockSpec((1, NSA_HEADS, NSA_DH), lambda b: (b, 0, 0)),
                   pl.BlockSpec((1, 8, n_lane), lambda b: (b, 0, 0))),
        compiler_params=_cparams("parallel"), name="cmp_topk_sample",
    )(q3, kvc_perm)
    blk_idx = idx[:, :G, :n_pick].reshape(nb, G * n_pick)

    half = PAGE_SIZE // SLC_BLOCK
    slc_t = _feature_major(cache_slc)
    win_t = _feature_major(cache_win)

    def kv_spec(g):
        def index(b, k, idx_ref, pt_ref):
            return (pt_ref[b, idx_ref[b, g * n_pick + k] // half], g, 0, 0, 0)
        return pl.BlockSpec((1, 1, 2, NSA_DH, PAGE_SIZE), index)

    per_seq = lambda shape: pl.BlockSpec((1,) + shape, lambda b, k, *_: (b,) + (0,) * len(shape))
    grid_spec = pltpu.PrefetchScalarGridSpec(
        num_scalar_prefetch=2, grid=(nb, n_pick),
        in_specs=[kv_spec(g) for g in range(G)] + [
            per_seq((NSA_HEADS, NSA_DH)), per_seq((1, NSA_KV_W)), per_seq((1, NSA_KV_W)),
            per_seq((G, 2, NSA_DH, win)), per_seq((1, _SMALL_W)), per_seq((NSA_HEADS, NSA_DH))],
        out_specs=per_seq((NSA_HEADS, NSA_DH)),
        scratch_shapes=[pltpu.VMEM((G, NSA_HEADS, 1), F32), pltpu.VMEM((G, NSA_HEADS, 2 * NSA_DH), F32)])
    on = pl.pallas_call(
        functools.partial(_slc_win_sample_kernel, t=t, win=win),
        out_shape=jax.ShapeDtypeStruct((nb, NSA_HEADS, NSA_DH), BF16), grid_spec=grid_spec,
        compiler_params=_cparams("parallel", "arbitrary"), name="slc_win_sample",
    )(blk_idx, page_table, slc_t, slc_t, slc_t, slc_t, q3,
      ks_new.reshape(nb, 1, NSA_KV_W), kw_new.reshape(nb, 1, NSA_KV_W),
      win_t, small.reshape(nb, 1, _SMALL_W), oc)
    return on.reshape(nb, NSA_HEADS * NSA_DH)


def _ffn_sample_kernel(x_ref, s0_ref, s1_ref, p_ref, gpre_ref, wup_ref, cw_ref, cb_ref, wdn_ref, gpost_ref,
                       gple_ref, wpg_ref, wpp_ref, y_ref, a_ref):
    x = x_ref[...]
    ab = _dot(_rms(x, gpre_ref[...]).astype(BF16), wup_ref[...])
    a = ab[:, 0:D_FF]
    cw = cw_ref[...]
    conv = (s0_ref[...] * cw[0:1] + s1_ref[...] * cw[1:2] + a * cw[2:3]) + cb_ref[...]
    y_ref[...] = _ffn_tail(x, conv, ab[:, D_FF:], p_ref[...], wdn_ref, gpost_ref, gple_ref, wpg_ref, wpp_ref)
    a_ref[...] = a


def _ffn_sample(x, conv_state, p, fw):
    nb = x.shape[0]
    full = lambda w: pl.BlockSpec((nb, w), lambda i: (0, 0))
    return pl.pallas_call(
        _ffn_sample_kernel,
        out_shape=(jax.ShapeDtypeStruct((nb, D_MODEL), F32), jax.ShapeDtypeStruct((nb, D_FF), F32)),
        grid=(1,),
        in_specs=[full(D_MODEL), full(D_FF), full(D_FF), full(PLE_DIM)] + _ffn_weight_specs(),
        out_specs=(full(D_MODEL), full(D_FF)),
        compiler_params=_cparams("arbitrary"), name="ffn_sample",
    )(x, conv_state[:, 0], conv_state[:, 1], p, *fw)


def _layer_sample(x, p, cache_cmp, cache_slc, cache_win, state_gla, state_conv, page_table, w):
    nb, s, _ = x.shape
    assert s == 1
    x2 = x.reshape(nb, D_MODEL)
    gla, small, nq, kc, ks, kw, mg = _proj_in(x2, w["g_mix_pre"], w["w_in_pack"], nb)
    og, gla_state = _gla_sample(gla, small, state_gla, w["w_gate_pack"], w["b_gla_gate"], w["g_gla_norm"])

    kvc = _compress(page_table, _feature_major(cache_cmp), *w["cmp"])
    on = _nsa_sample(nq, ks, kw, small, _permute_cmp(kvc), cache_slc, cache_win, page_table)

    x1 = _mix_out(og, on, mg, x2, w["w_gla_out"], w["w_nsa_out"], w["w_out"], w["g_mix_post"], nb)
    y, a = _ffn_sample(x1, state_conv, p.reshape(nb, PLE_DIM), w["ffn"])
    kv_shape = (nb, 1, NSA_KV_GROUPS, 2, NSA_DH)
    conv_new = jnp.stack([state_conv[:, 1], a], axis=1)
    return (y.reshape(nb, 1, D_MODEL), kc.reshape(kv_shape), ks.reshape(kv_shape), kw.reshape(kv_shape),
            gla_state, conv_new)


def _prep_weights(g_mix_pre, g_mix_post, g_ffn_pre, g_ffn_post, g_ple, w_in, w_gla_gate_up, b_gla_gate,
                  g_gla_norm, w_gla_out, w_cmp1, b_cmp1, w_cmp2, b_cmp2, pe_cmp, w_nsa_out, w_out, w_ffn_up,
                  conv_ffn_w, b_conv_ffn, w_ffn_down, w_ple_proj, w_ple_gate):
    pe_flat = pe_cmp.transpose(1, 0, 2).reshape(2, 1, CMP_LEN * NSA_DH)
    row = lambda v: v.reshape(1, -1)
    return dict(
        g_mix_pre=g_mix_pre, g_mix_post=g_mix_post, w_in_pack=_pack_w_in(w_in), w_kv_t=_pack_w_kv_t(w_in),
        w_gate_pack=_pack_w_gate(w_gla_gate_up), b_gla_gate=b_gla_gate, g_gla_norm=g_gla_norm,
        w_gla_out=w_gla_out.astype(BF16), w_nsa_out=w_nsa_out.astype(BF16), w_out=w_out.astype(BF16),
        cmp=(_pack_w_cmp1(w_cmp1), w_cmp1.astype(BF16), pe_flat, b_cmp1.reshape(2, 1, CMP_HIDDEN),
             w_cmp2.astype(BF16), b_cmp2.reshape(2, 1, NSA_DH)),
        ffn=(row(g_ffn_pre), w_ffn_up.astype(BF16), conv_ffn_w, row(b_conv_ffn), w_ffn_down.astype(BF16),
             row(g_ffn_post), row(g_ple), w_ple_gate.astype(BF16), w_ple_proj.astype(BF16)),
    )


def kernel(x_prompt, x_sample, cache_cmp_kv, cache_slc_kv, cache_win_kv, state_gla, state_ffn_conv, page_table,
           p_prompt, p_sample, g_mix_pre, g_mix_post, g_ffn_pre, g_ffn_post, g_ple, w_in, w_gla_gate_up,
           b_gla_gate, g_gla_norm, w_gla_out, w_cmp1, b_cmp1, w_cmp2, b_cmp2, pe_cmp, w_nsa_out, w_out, w_ffn_up,
           conv_ffn_w, b_conv_ffn, w_ffn_down, w_ple_proj, w_ple_gate):
    layer_weights = (g_mix_pre, g_mix_post, g_ffn_pre, g_ffn_post, g_ple, w_in, w_gla_gate_up, b_gla_gate,
                     g_gla_norm, w_gla_out, w_cmp1, b_cmp1, w_cmp2, b_cmp2, pe_cmp, w_nsa_out, w_out, w_ffn_up,
                     conv_ffn_w, b_conv_ffn, w_ffn_down, w_ple_proj, w_ple_gate)
    depth = w_in.shape[0]
    yp, ys = x_prompt, x_sample
    extras_p, extras_s = [], []
    for i in range(depth):
        w = _prep_weights(*[v[i] for v in layer_weights])
        yp, *ep = _layer_prompt(yp, p_prompt[i], w)
        ys, *es = _layer_sample(ys, p_sample[i], cache_cmp_kv[i], cache_slc_kv[i], cache_win_kv[i], state_gla[i],
                                state_ffn_conv[i], page_table, w)
        extras_p.append(ep)
        extras_s.append(es)
    stack = lambda rows, j: jnp.stack([r[j] for r in rows])
    outs = [yp, ys]
    for j in range(5):
        outs += [stack(extras_p, j), stack(extras_s, j)]
    return tuple(outs)
```

```python
import functools

import numpy as np
import jax
import jax.numpy as jnp
from jax import lax
from jax.experimental import pallas as pl
from jax.experimental.pallas import tpu as pltpu

F32 = jnp.float32
BF16 = jnp.bfloat16

D_MODEL = 1024
PAGE_SIZE = 128
GLA_HEADS = 4
GLA_DK = 128
GLA_DV = 256
GLA_GATE_RANK = 16
GLA_TAU = 16.0
GLA_SUB = 16
NSA_HEADS = 16
NSA_KV_GROUPS = 4
NSA_GROUP_HEADS = 4
NSA_DH = 64
NSA_KV_W = 2 * NSA_KV_GROUPS * NSA_DH
CMP_LEN = 32
CMP_STRIDE = 16
CMP_HIDDEN = 128
SLC_BLOCK = 64
SLC_TOPK = 16
WINDOW = 512
FORCE_SCORE = 1.0e4
D_FF = 2816
CONV_W = 3
PLE_DIM = 256
EPS = 1e-6

V7X_VMEM_BYTES = 64 * 1024 * 1024
VMEM_LIMIT = V7X_VMEM_BYTES - 8 * 1024 * 1024

_GLA_OFF, _GLA_W = 0, 2 * GLA_HEADS * GLA_DK + 2 * GLA_HEADS * GLA_DV
_SMALL_OFF, _SMALL_W = _GLA_OFF + _GLA_W, 128
_NQ_OFF, _NQ_W = _SMALL_OFF + _SMALL_W, NSA_HEADS * NSA_DH
_NK_OFF, _NK_W = _NQ_OFF + _NQ_W, 3 * NSA_KV_W
_MG_OFF, _MG_W = _NK_OFF + _NK_W, 2 * D_MODEL
_PACK_W = _MG_OFF + _MG_W
_NG_LANE0 = GLA_GATE_RANK

_Q_SCALE = float(NSA_DH ** -0.5 * np.log2(np.e))

_MASK_BIG = float(2.0 ** 100)


def _cparams(*sem):
    return pltpu.CompilerParams(dimension_semantics=sem, vmem_limit_bytes=VMEM_LIMIT)


def _const_spec(shape):
    nd = len(shape)
    return pl.BlockSpec(shape, lambda *_: (0,) * nd, pipeline_mode=pl.Buffered(1))


def _rms(x, g):
    return x * lax.rsqrt(jnp.mean(x * x, axis=-1, keepdims=True) + EPS) * g


def _gelu_tanh(x):
    return 0.5 * x * (1.0 + jnp.tanh(np.sqrt(2.0 / np.pi).astype(np.float32) * (x + 0.044715 * (x * x * x))))


def _sigmoid(x):
    return 1.0 / (1.0 + jnp.exp(-x))


def _dot(a, b):
    return jnp.dot(a, b, preferred_element_type=F32)


def _dot_nt(a, b):
    return lax.dot_general(a, b, (((1,), (1,)), ((), ())), preferred_element_type=F32)


def _dot_tn(a, b):
    return lax.dot_general(a, b, (((0,), (0,)), ((), ())), preferred_element_type=F32)


def _masked_softmax(s, mask):
    s = jnp.where(mask, s, -jnp.inf)
    m = jnp.max(s, axis=-1, keepdims=True)
    m = jnp.where(m > -jnp.inf, m, 0.0)
    e = jnp.exp2(s - m)
    d = jnp.sum(e, axis=-1, keepdims=True)
    return e / jnp.where(d > 0, d, 1.0)


def _lane_column(x, col):
    lane = lax.broadcasted_iota(jnp.int32, x.shape, 1)
    return jnp.sum(jnp.where(lane == col, x, 0.0), axis=-1, keepdims=True)


def _pack_w_in(w_in):
    sizes = (512, 512, 1024, 1024, GLA_GATE_RANK, 1024, NSA_KV_W, NSA_KV_W, NSA_KV_W, NSA_HEADS * 3, 2 * D_MODEL)
    pts = [int(v) for v in np.cumsum(sizes)[:-1]]
    gq, gk, gv, gr, ga, nq, nkc, nks, nkw, ng, mg = jnp.split(w_in, pts, axis=1)
    small = jnp.concatenate([ga, ng, jnp.zeros((D_MODEL, _SMALL_W - GLA_GATE_RANK - NSA_HEADS * 3), w_in.dtype)], axis=1)
    return jnp.concatenate([gq, gk, gv, gr, small, nq, nkc, nks, nkw, mg], axis=1).astype(BF16)


def _pack_w_kv_t(w_in):
    lo = 512 + 512 + 1024 + 1024 + GLA_GATE_RANK + NSA_HEADS * NSA_DH
    return w_in[:, lo:lo + _NK_W].T.astype(BF16)


def _proj_in_kernel(x_ref, g_ref, w_ref, *refs, kv_major):
    h = _rms(x_ref[...], g_ref[...]).astype(BF16)

    def seg(off, width):
        return _dot(h, w_ref[:, off:off + width])

    if kv_major:
        wkv_ref, gla_ref, small_ref, nq_ref, kc_ref, ks_ref, kw_ref, nkb_ref, mg_ref = refs
        nk = _dot_nt(wkv_ref[...], h)
        kc_ref[0] = nk[0:NSA_KV_W]
        ks_ref[0] = nk[NSA_KV_W:2 * NSA_KV_W]
        kw_ref[0] = nk[2 * NSA_KV_W:3 * NSA_KV_W]
        nkb_ref[0] = nk.astype(BF16)
    else:
        gla_ref, small_ref, nq_ref, kc_ref, ks_ref, kw_ref, mg_ref = refs
        nk = seg(_NK_OFF, _NK_W)
        kc_ref[...] = nk[:, 0:NSA_KV_W]
        ks_ref[...] = nk[:, NSA_KV_W:2 * NSA_KV_W]
        kw_ref[...] = nk[:, 2 * NSA_KV_W:3 * NSA_KV_W]
    gla_ref[...] = seg(_GLA_OFF, _GLA_W)
    small_ref[...] = seg(_SMALL_OFF, _SMALL_W)
    nq_ref[...] = (seg(_NQ_OFF, _NQ_W) * _Q_SCALE).astype(BF16)
    mg_ref[...] = seg(_MG_OFF, _MG_W)


def _proj_in(x, g_pre, w_pack, tm, w_kv_t=None, seq=None):
    m = x.shape[0]
    assert m % tm == 0
    kv_major = w_kv_t is not None
    row = lambda w: pl.BlockSpec((tm, w), lambda i: (i, 0))
    sds = jax.ShapeDtypeStruct
    in_specs = [row(D_MODEL), _const_spec((1, D_MODEL)), _const_spec((D_MODEL, _PACK_W))]
    args = [x, g_pre.reshape(1, D_MODEL), w_pack]
    head = [(sds((m, _GLA_W), F32), row(_GLA_W)), (sds((m, _SMALL_W), F32), row(_SMALL_W)),
            (sds((m, _NQ_W), BF16), row(_NQ_W))]
    if kv_major:
        assert seq % tm == 0
        tps = seq // tm
        col = lambda w: pl.BlockSpec((1, w, tm), lambda i: (i // tps, 0, i % tps))
        in_specs.append(_const_spec((_NK_W, D_MODEL)))
        args.append(w_kv_t)
        kv = [(sds((m // seq, NSA_KV_W, seq), F32), col(NSA_KV_W))] * 3 + [(sds((m // seq, _NK_W, seq), BF16), col(_NK_W))]
    else:
        kv = [(sds((m, NSA_KV_W), F32), row(NSA_KV_W))] * 3
    outs = head + kv + [(sds((m, _MG_W), F32), row(_MG_W))]
    return pl.pallas_call(
        functools.partial(_proj_in_kernel, kv_major=kv_major),
        out_shape=tuple(o[0] for o in outs), grid=(m // tm,), in_specs=in_specs,
        out_specs=tuple(o[1] for o in outs),
        compiler_params=_cparams("parallel"), name="proj_in",
    )(*args)


def _log_decay(small, wg, bg):
    x = _dot(small.astype(BF16), wg) + bg
    return (jnp.minimum(x, 0.0) - jnp.log1p(jnp.exp(-jnp.abs(x)))) * (1.0 / GLA_TAU)


def _pack_w_gate(w_gate_up):
    pad = jnp.zeros((_SMALL_W - GLA_GATE_RANK, GLA_HEADS * GLA_DK), w_gate_up.dtype)
    return jnp.concatenate([w_gate_up, pad], axis=0).astype(BF16)


def _gla_out(o, gn, r):
    o = o * lax.rsqrt(jnp.mean(o * o, axis=-1, keepdims=True) + EPS) * gn
    return (o * (r * _sigmoid(r))).astype(BF16)


def _gla_chunk_kernel(q_ref, k_ref, v_ref, r_ref, small_ref, wg_ref, bg_ref, gn_ref, og_ref, st_ref,
                      s_scr, kp_scr, bp_scr, *, chunk):
    c = pl.program_id(1)
    SUB, DK, DV = GLA_SUB, GLA_DK, GLA_DV

    @pl.when(c == 0)
    def _():
        s_scr[...] = jnp.zeros_like(s_scr)
        kp_scr[:, 0:SUB, :] = jnp.zeros((GLA_HEADS, SUB, DK), F32)
        bp_scr[:, 0:SUB, :] = jnp.zeros((GLA_HEADS, SUB, DK), F32)

    small = small_ref[...]
    for h in range(GLA_HEADS):
        kh = slice(h * DK, (h + 1) * DK)
        vh = slice(h * DV, (h + 1) * DV)
        la = _log_decay(small, wg_ref[:, kh], bg_ref[:, kh])
        o, st_new = _gla_head_chunk(q_ref[:, kh] * (DK ** -0.5), k_ref[:, kh], v_ref[:, vh].astype(BF16), la,
                                    s_scr[h], kp_scr.at[h], bp_scr.at[h], chunk)
        s_scr[h] = st_new
        og_ref[:, vh] = _gla_out(o, gn_ref[:, vh], r_ref[:, vh])

    @pl.when(c == pl.num_programs(1) - 1)
    def _():
        for h in range(GLA_HEADS):
            st_ref[0, h] = s_scr[h].T


def _gla_head_chunk(q, k, vb, la, st, kp_ref, bp_ref, chunk):
    C, SUB = chunk, GLA_SUB
    ri = lax.broadcasted_iota(jnp.int32, (C, C), 0)
    ci = lax.broadcasted_iota(jnp.int32, (C, C), 1)
    tri = jnp.where(ci <= ri, 1.0, 0.0).astype(BF16)
    hi = la.astype(BF16)
    r1 = la - hi.astype(F32)
    mid = r1.astype(BF16)
    lo = (r1 - mid.astype(F32)).astype(BF16)
    b = _dot(tri, hi) + _dot(tri, mid) + _dot(tri, lo)

    o = _dot_nt((q * jnp.exp(b)).astype(BF16), st.astype(BF16))
    b_last = b[C - 1:C, :]
    kdec = (k * jnp.exp(b_last - b)).astype(BF16)
    st_new = st * jnp.exp(b_last) + _dot_tn(vb, kdec)

    row = lax.broadcasted_iota(jnp.int32, (C, GLA_DK), 0)
    blocks = [jnp.zeros((SUB, C), F32)]
    for i in range(1, C // SUB):
        beta = b[SUB * i - 1:SUB * i, :]
        qi = (q[SUB * i:SUB * (i + 1)] * jnp.exp(b[SUB * i:SUB * (i + 1)] - beta)).astype(BF16)
        ki = (k * jnp.exp(jnp.where(row < SUB * i, beta - b, -jnp.inf))).astype(BF16)
        blocks.append(_dot_nt(qi, ki))
    a = jnp.concatenate(blocks, axis=0)

    kp_ref[SUB:SUB + C, :] = k
    bp_ref[SUB:SUB + C, :] = b
    tmod = ri % SUB
    for j in range(SUB):
        ks = kp_ref[SUB - j:SUB - j + C, :]
        bs = bp_ref[SUB - j:SUB - j + C, :]
        aj = jnp.sum(q * ks * jnp.exp(b - bs), axis=-1, keepdims=True)
        a = a + jnp.where((ci == ri - j) & (tmod >= j), aj, 0.0)
    return o + _dot(a.astype(BF16), vb), st_new


def _gla_prompt(gla, small, w_gate_pack, b_gate, g_norm, batch, seq, chunk=128):
    n = seq // chunk
    assert seq % chunk == 0 and chunk % GLA_SUB == 0
    H = GLA_HEADS
    kw, vw = H * GLA_DK, H * GLA_DV
    assert vw == 2 * kw
    kern = functools.partial(_gla_chunk_kernel, chunk=chunk)
    return pl.pallas_call(
        kern,
        out_shape=(jax.ShapeDtypeStruct((batch * seq, vw), BF16),
                   jax.ShapeDtypeStruct((batch, H, GLA_DK, GLA_DV), F32)),
        grid=(batch, n),
        in_specs=[
            pl.BlockSpec((chunk, kw), lambda b, c: (b * n + c, 0)),
            pl.BlockSpec((chunk, kw), lambda b, c: (b * n + c, 1)),
            pl.BlockSpec((chunk, vw), lambda b, c: (b * n + c, 1)),
            pl.BlockSpec((chunk, vw), lambda b, c: (b * n + c, 2)),
            pl.BlockSpec((chunk, _SMALL_W), lambda b, c: (b * n + c, 0)),
            _const_spec((_SMALL_W, kw)), _const_spec((1, kw)), _const_spec((1, vw)),
        ],
        out_specs=(pl.BlockSpec((chunk, vw), lambda b, c: (b * n + c, 0)),
                   pl.BlockSpec((1, H, GLA_DK, GLA_DV), lambda b, c: (b, 0, 0, 0))),
        scratch_shapes=[pltpu.VMEM((H, GLA_DV, GLA_DK), F32),
                        pltpu.VMEM((H, GLA_SUB + chunk, GLA_DK), F32),
                        pltpu.VMEM((H, GLA_SUB + chunk, GLA_DK), F32)],
        compiler_params=_cparams("parallel", "arbitrary"), name="gla_chunk",
    )(gla, gla, gla, gla, small, w_gate_pack, b_gate.reshape(1, -1), g_norm.reshape(1, -1))


def _pack_w_cmp1(w1):
    w = w1.reshape(2, 2, CMP_STRIDE, NSA_DH, CMP_HIDDEN)
    per_c = jnp.concatenate([w[:, 0], w[:, 1]], axis=-1)
    z = jnp.zeros_like(per_c[0])
    w16 = jnp.concatenate([jnp.concatenate([per_c[0], z], axis=-1),
                           jnp.concatenate([z, per_c[1]], axis=-1)], axis=1)
    return w16.reshape(CMP_STRIDE // 2, 4 * NSA_DH, 4 * CMP_HIDDEN).astype(BF16)


def _compress_kernel(pt_ref, kv_hbm, wp_ref, w1_ref, pef_ref, b1_ref, w2_ref, b2_ref, out_ref,
                     buf, tok_scr, hb_scr, sem, *, n_pages, pages_per_row):
    bidx = pl.program_id(0)
    n_chunk = n_pages * (PAGE_SIZE // CMP_STRIDE)
    GW = 2 * NSA_DH

    def page_copy(p):
        page = pt_ref[bidx, p]
        off = pl.multiple_of((page % pages_per_row) * PAGE_SIZE, PAGE_SIZE)
        src = kv_hbm.at[page // pages_per_row, :, :, :, pl.ds(off, PAGE_SIZE)]
        return pltpu.make_async_copy(src, buf.at[:, :, :, pl.ds(pl.multiple_of(p * PAGE_SIZE, PAGE_SIZE), PAGE_SIZE)], sem)

    def start(p, carry):
        page_copy(p).start()
        return carry

    def wait(p, carry):
        page_copy(p).wait()
        return carry

    lax.fori_loop(0, n_pages, start, 0)
    lax.fori_loop(0, n_pages, wait, 0)

    HID = CMP_HIDDEN
    hb_scr[n_chunk:n_chunk + 8, :] = jnp.zeros((8, HID), F32)
    const = [_dot(pef_ref[c].astype(BF16), w1_ref[c]) + b1_ref[c] for c in range(2)]
    for g in range(NSA_KV_GROUPS):

        def to_token_major(p, carry):
            off = pl.multiple_of(p * PAGE_SIZE, PAGE_SIZE)
            tok_scr[pl.ds(off, PAGE_SIZE), :] = buf[g, :, :, pl.ds(off, PAGE_SIZE)].reshape(GW, PAGE_SIZE).T
            return carry

        lax.fori_loop(0, n_pages, to_token_major, 0, unroll=8)
        acc = jnp.zeros((n_chunk, 4 * HID), F32)
        for p in range(0, CMP_STRIDE, 2):
            xs = [tok_scr[pl.ds(p + i, n_chunk, stride=CMP_STRIDE), :].astype(BF16) for i in range(2)]
            acc = acc + _dot(jnp.concatenate(xs, axis=1), wp_ref[p // 2])
        for c in range(2):
            hb_scr[0:n_chunk, :] = acc[:, (2 * c + 1) * HID:(2 * c + 2) * HID]
            hid = acc[:, 2 * c * HID:(2 * c + 1) * HID] + hb_scr[pl.ds(1, n_chunk), :] + const[c]
            out_ref[0, g * 2 + c] = _dot(_gelu_tanh(hid).astype(BF16), w2_ref[c]) + b2_ref[c]


def _compress(page_table, kv_t, wp, w1b, pe_flat, b1, w2b, b2):
    batch, n_pages = page_table.shape
    n_chunk = n_pages * (PAGE_SIZE // CMP_STRIDE)
    assert kv_t.shape[-1] % PAGE_SIZE == 0
    kern = functools.partial(_compress_kernel, n_pages=n_pages, pages_per_row=kv_t.shape[-1] // PAGE_SIZE)
    grid_spec = pltpu.PrefetchScalarGridSpec(
        num_scalar_prefetch=1, grid=(batch,),
        in_specs=[pl.BlockSpec(memory_space=pl.ANY),
                  _const_spec((CMP_STRIDE // 2, 4 * NSA_DH, 4 * CMP_HIDDEN)),
                  _const_spec((2, CMP_LEN * NSA_DH, CMP_HIDDEN)),
                  _const_spec((2, 1, CMP_LEN * NSA_DH)),
                  _const_spec((2, 1, CMP_HIDDEN)),
                  _const_spec((2, CMP_HIDDEN, NSA_DH)),
                  _const_spec((2, 1, NSA_DH))],
        out_specs=pl.BlockSpec((1, 2 * NSA_KV_GROUPS, n_chunk, NSA_DH), lambda b, pt: (b, 0, 0, 0)),
        scratch_shapes=[pltpu.VMEM((NSA_KV_GROUPS, 2, NSA_DH, n_pages * PAGE_SIZE), F32),
                        pltpu.VMEM((n_pages * PAGE_SIZE, 2 * NSA_DH), F32),
                        pltpu.VMEM((n_chunk + 8, CMP_HIDDEN), F32),
                        pltpu.SemaphoreType.DMA(())])
    return pl.pallas_call(
        kern, out_shape=jax.ShapeDtypeStruct((batch, 2 * NSA_KV_GROUPS, n_chunk, NSA_DH), F32),
        grid_spec=grid_spec, compiler_params=_cparams("arbitrary"), name="compress",
    )(page_table, kv_t, wp, w1b, pe_flat, b1, w2b, b2)


def _feature_major(kv):
    return kv.transpose(0, 2, 3, 4, 1)


def _token_major(kv_t, batch, seq):
    return kv_t.reshape(batch, NSA_KV_GROUPS, 2, NSA_DH, seq).transpose(0, 4, 1, 2, 3)


def _permute_cmp(kvc):
    b, gc, n, d = kvc.shape
    return kvc.reshape(b, gc, n // 4, 4, d).transpose(0, 1, 3, 2, 4).reshape(b, gc, n, d)


def _group_queries(qall):
    return jnp.concatenate([qall[:, r * NSA_DH:(r + 1) * NSA_DH] for r in range(NSA_GROUP_HEADS)], axis=0)


def _block_importance(psum, n_slc):
    p0, p1, p2, p3 = (psum[:, m * n_slc:(m + 1) * n_slc] for m in range(4))
    lane = lax.broadcasted_iota(jnp.int32, p3.shape, 1)
    p3s = jnp.where(lane == 0, 0.0, pltpu.roll(p3, 1, axis=1))
    return ((((((p0 + p3s) + p1) + p0) + p2) + p1) + p3) + p2


def _top_blocks_t(score_t, count):
    blk = lax.broadcasted_iota(jnp.int32, score_t.shape, 0).astype(F32)
    n = float(score_t.shape[0])
    finite = score_t > -jnp.inf
    picked = jnp.zeros(score_t.shape, F32)
    for _ in range(count):
        mx = jnp.max(score_t, axis=0, keepdims=True)
        idx = jnp.min(jnp.where(score_t == mx, blk, n), axis=0, keepdims=True)
        pick = blk == idx
        picked = jnp.where(pick, 1.0, picked)
        score_t = jnp.where(pick, -jnp.inf, score_t)
    return jnp.where(finite, picked, 0.0)


_N_FORCED = 3


def _cmp_topk_kernel(q_ref, kc_ref, vc_ref, small_ref, oc_ref, sel_ref, imp_scr, *, tq, sub, n_slc):
    g = pl.program_id(1)
    q0 = pl.program_id(2) * tq
    rows = NSA_GROUP_HEADS * sub
    n_cmp = 4 * n_slc
    kc = kc_ref[0, 0].astype(BF16)
    vc = vc_ref[0, 0].astype(BF16)
    lane = lax.broadcasted_iota(jnp.int32, (1, n_cmp), 1)
    blk_end = (4 * (lane % n_slc) + lane // n_slc) * CMP_STRIDE + (CMP_LEN - 1)
    for j in range(tq // sub):
        js = slice(j * sub, (j + 1) * sub)
        q = _group_queries(q_ref[js, :])
        s = _dot_nt(q, kc)
        t_rows = (q0 + j * sub) + lax.broadcasted_iota(jnp.int32, (rows, 1), 0) % sub
        p = _masked_softmax(s, blk_end <= t_rows)
        oc = _dot(p.astype(BF16), vc)
        psum = ((p[0:sub] + p[sub:2 * sub]) + p[2 * sub:3 * sub]) + p[3 * sub:4 * sub]
        imp_scr[js, :] = _block_importance(psum, n_slc)
        sig = _sigmoid(small_ref[js, :])
        outs = []
        for r in range(NSA_GROUP_HEADS):
            gate = _lane_column(sig, _NG_LANE0 + (g * NSA_GROUP_HEADS + r) * 3)
            outs.append(gate * oc[r * sub:(r + 1) * sub])
        oc_ref[js, :] = jnp.concatenate(outs, axis=1)

    cur = (q0 + lax.broadcasted_iota(jnp.int32, (1, tq), 1)) // SLC_BLOCK
    blk = lax.broadcasted_iota(jnp.int32, (n_slc, 1), 0)
    forced = (blk == 0) | (blk == cur) | (blk == cur - 1)
    valid = blk <= cur
    score_t = jnp.where(valid & jnp.logical_not(forced), imp_scr[...].T, -jnp.inf)
    sel_t = jnp.where(valid & forced, 1.0, _top_blocks_t(score_t, min(SLC_TOPK, n_slc) - _N_FORCED))
    sel_ref[0] = (sel_t.T - 1.0).astype(BF16)


def _cmp_topk_prompt(nq, kvc_perm, small, batch, seq, tq=512, sub=128):
    G = NSA_KV_GROUPS
    n_slc = seq // SLC_BLOCK
    assert kvc_perm.shape[2] == 4 * n_slc and seq % tq == 0 and tq % sub == 0 and n_slc >= _N_FORCED
    nqt = seq // tq
    gw = NSA_GROUP_HEADS * NSA_DH
    kern = functools.partial(_cmp_topk_kernel, tq=tq, sub=sub, n_slc=n_slc)
    return pl.pallas_call(
        kern,
        out_shape=(jax.ShapeDtypeStruct((batch * seq, NSA_HEADS * NSA_DH), F32),
                   jax.ShapeDtypeStruct((batch * G, seq, n_slc), BF16)),
        grid=(batch, G, nqt),
        in_specs=[pl.BlockSpec((tq, gw), lambda b, g, i: (b * nqt + i, g)),
                  pl.BlockSpec((1, 1, 4 * n_slc, NSA_DH), lambda b, g, i: (b, 2 * g, 0, 0)),
                  pl.BlockSpec((1, 1, 4 * n_slc, NSA_DH), lambda b, g, i: (b, 2 * g + 1, 0, 0)),
                  pl.BlockSpec((tq, _SMALL_W), lambda b, g, i: (b * nqt + i, 0))],
        out_specs=(pl.BlockSpec((tq, gw), lambda b, g, i: (b * nqt + i, g)),
                   pl.BlockSpec((1, tq, n_slc), lambda b, g, i: (b * G + g, i, 0))),
        scratch_shapes=[pltpu.VMEM((tq, n_slc), F32)],
        compiler_params=_cparams("parallel", "parallel", "parallel"), name="cmp_topk",
    )(nq, kvc_perm, kvc_perm, small)


def _slc_win_kernel(q_ref, sel_ref, ks_ref, kw_ref, small_ref, oc_ref, on_ref, *, tq, tk, n_slc):
    g = pl.program_id(1)
    q0 = pl.program_id(2) * tq
    rows = NSA_GROUP_HEADS * tq
    DH = NSA_DH
    q = _group_queries(q_ref[...])
    qpad = jnp.concatenate([q, jnp.zeros((rows, DH), BF16)], axis=1)
    sel4 = jnp.concatenate([sel_ref[0]] * NSA_GROUP_HEADS, axis=0)
    qaug = jnp.concatenate([qpad, sel4], axis=1)
    t_rows = q0 + lax.broadcasted_iota(jnp.int32, (rows, 1), 0) % tq

    is_k = lax.broadcasted_iota(jnp.int32, (2 * DH, tk), 0) < DH
    blk_r = lax.broadcasted_iota(jnp.int32, (n_slc, tk), 0)
    tok_c = lax.broadcasted_iota(jnp.int32, (n_slc, tk), 1)
    tok_l = lax.broadcasted_iota(jnp.int32, (1, tk), 1)

    def tile(kt, m, acc, causal):
        k0 = pl.multiple_of(kt * tk, tk)
        kv = ks_ref[0, :, pl.ds(k0, tk)]
        onehot = jnp.where(blk_r == (k0 + tok_c) // SLC_BLOCK, _MASK_BIG, 0.0).astype(BF16)
        kaug = jnp.concatenate([jnp.where(is_k, kv, jnp.zeros_like(kv)), onehot], axis=0)
        s = _dot(qaug, kaug)
        if causal:
            s = jnp.where(k0 + tok_l <= t_rows, s, -_MASK_BIG)
        m_new = jnp.maximum(m, jnp.max(s, axis=-1, keepdims=True))
        p = jnp.exp2(s - m_new)
        vaug = jnp.where(is_k, jnp.ones_like(kv), kv)
        acc = jnp.exp2(m - m_new) * acc + _dot_nt(p.astype(BF16), vaug)
        return m_new, acc

    last = (q0 + tq - 1) // tk
    m0 = jnp.full((rows, 1), -1e30, F32)
    acc0 = jnp.zeros((rows, 2 * DH), F32)
    m, acc = lax.fori_loop(0, last, lambda kt, c: tile(kt, c[0], c[1], False), (m0, acc0))
    m, acc = tile(last, m, acc, True)
    o_s = acc[:, DH:] / acc[:, 0:DH]

    wlen = WINDOW + tq
    w0 = pl.multiple_of(jnp.maximum(q0 - WINDOW, 0), tq)
    kvw = kw_ref[0, :, pl.ds(w0, wlen)]
    row_w = lax.broadcasted_iota(jnp.int32, (2 * DH, wlen), 0)
    s_w = _dot(qpad, jnp.where(row_w < DH, kvw, jnp.zeros_like(kvw)))
    pos = w0 + lax.broadcasted_iota(jnp.int32, (1, wlen), 1)
    p_w = _masked_softmax(s_w, (pos <= t_rows) & (pos > t_rows - WINDOW))
    o_w = _dot_nt(p_w.astype(BF16), kvw)[:, DH:]

    sig = _sigmoid(small_ref[...])
    oc = oc_ref[...]
    outs = []
    for r in range(NSA_GROUP_HEADS):
        lane0 = _NG_LANE0 + (g * NSA_GROUP_HEADS + r) * 3
        g_s = _lane_column(sig, lane0 + 1)
        g_w = _lane_column(sig, lane0 + 2)
        rs = slice(r * tq, (r + 1) * tq)
        outs.append(oc[:, r * DH:(r + 1) * DH] + g_s * o_s[rs] + g_w * o_w[rs])
    on_ref[...] = jnp.concatenate(outs, axis=1).astype(BF16)


def _slc_win_prompt(nq, selm, nkb, small, oc, batch, seq, tq=256, tk=512):
    G = NSA_KV_GROUPS
    n_slc = seq // SLC_BLOCK
    assert seq % tk == 0 and seq % tq == 0 and tk % tq == 0 and seq >= WINDOW + tq
    nqt = seq // tq
    gw = NSA_GROUP_HEADS * NSA_DH
    kvw = 2 * NSA_DH
    kern = functools.partial(_slc_win_kernel, tq=tq, tk=tk, n_slc=n_slc)
    return pl.pallas_call(
        kern, out_shape=jax.ShapeDtypeStruct((batch * seq, NSA_HEADS * NSA_DH), BF16),
        grid=(batch, G, nqt),
        in_specs=[pl.BlockSpec((tq, gw), lambda b, g, i: (b * nqt + i, g)),
                  pl.BlockSpec((1, tq, n_slc), lambda b, g, i: (b * G + g, i, 0)),
                  pl.BlockSpec((1, kvw, seq), lambda b, g, i: (b, G + g, 0)),
                  pl.BlockSpec((1, kvw, seq), lambda b, g, i: (b, 2 * G + g, 0)),
                  pl.BlockSpec((tq, _SMALL_W), lambda b, g, i: (b * nqt + i, 0)),
                  pl.BlockSpec((tq, gw), lambda b, g, i: (b * nqt + i, g))],
        out_specs=pl.BlockSpec((tq, gw), lambda b, g, i: (b * nqt + i, g)),
        compiler_params=_cparams("parallel", "parallel", "parallel"), name="slc_win",
    )(nq, selm, nkb, nkb, small, oc)


def _mix_out_kernel(og_ref, on_ref, mg_ref, x_ref, wgo_ref, wno_ref, wo_ref, gpost_ref, y_ref):
    yg = _dot(og_ref[...], wgo_ref[...])
    yn = _dot(on_ref[...], wno_ref[...])
    mg = mg_ref[...]
    mix = _sigmoid(mg[:, 0:D_MODEL]) * yg + _sigmoid(mg[:, D_MODEL:]) * yn
    z = _dot(mix.astype(BF16), wo_ref[...])
    y_ref[...] = x_ref[...] + _rms(z, gpost_ref[...])


def _mix_out(og, on, mg, x, wgo, wno, wo, g_post, tm):
    m = x.shape[0]
    row = lambda w: pl.BlockSpec((tm, w), lambda i: (i, 0))
    wspec = _const_spec((D_MODEL, D_MODEL))
    return pl.pallas_call(
        _mix_out_kernel, out_shape=jax.ShapeDtypeStruct((m, D_MODEL), F32), grid=(m // tm,),
        in_specs=[row(D_MODEL), row(D_MODEL), row(2 * D_MODEL), row(D_MODEL), wspec, wspec, wspec,
                  _const_spec((1, D_MODEL))],
        out_specs=row(D_MODEL), compiler_params=_cparams("parallel"), name="mix_out",
    )(og, on, mg, x, wgo, wno, wo, g_post.reshape(1, D_MODEL))


_FFN_HALO = 16


def _ffn_tail(x, conv, bgate, p, wdn_ref, gpost_ref, gple_ref, wpg_ref, wpp_ref):
    y = _dot((_gelu_tanh(conv) * bgate).astype(BF16), wdn_ref[...])
    x2 = x + _rms(y, gpost_ref[...])
    gate = _sigmoid(_dot(_rms(x2, gple_ref[...]).astype(BF16), wpg_ref[...]))
    return x2 + gate * _dot(p.astype(BF16), wpp_ref[...])


def _ffn_prompt_kernel(x_ref, xh_ref, p_ref, gpre_ref, wup_ref, cw_ref, cb_ref, wdn_ref, gpost_ref, gple_ref,
                       wpg_ref, wpp_ref, y_ref, alast_ref, h_scr, a_scr, *, tm, tiles_per_seq):
    H = _FFN_HALO
    x = x_ref[...]
    h_scr[0:H, :] = _rms(xh_ref[...], gpre_ref[...]).astype(BF16)
    h_scr[H:H + tm, :] = _rms(x, gpre_ref[...]).astype(BF16)
    ab = _dot(h_scr[...], wup_ref[...])
    keep = jnp.where(pl.program_id(0) % tiles_per_seq == 0, 0.0, 1.0)
    a_scr[0:H, :] = ab[0:H, 0:D_FF] * keep
    a_scr[H:H + tm, :] = ab[H:, 0:D_FF]
    cw = cw_ref[...]
    conv = (a_scr[H - 2:H - 2 + tm, :] * cw[0:1] + a_scr[H - 1:H - 1 + tm, :] * cw[1:2]
            + a_scr[H:H + tm, :] * cw[2:3]) + cb_ref[...]
    y_ref[...] = _ffn_tail(x, conv, ab[H:, D_FF:], p_ref[...], wdn_ref, gpost_ref, gple_ref, wpg_ref, wpp_ref)
    alast_ref[0] = a_scr[H + tm - 8:H + tm, :]


def _ffn_weight_specs():
    return [_const_spec((1, D_MODEL)), _const_spec((D_MODEL, 2 * D_FF)), _const_spec((CONV_W, D_FF)),
            _const_spec((1, D_FF)), _const_spec((D_FF, D_MODEL)), _const_spec((1, D_MODEL)),
            _const_spec((1, D_MODEL)), _const_spec((D_MODEL, D_MODEL)), _const_spec((PLE_DIM, D_MODEL))]


def _ffn_prompt(x, p, fw, batch, seq, tm=256):
    assert seq % tm == 0 and tm % _FFN_HALO == 0
    tps = seq // tm
    hb = tm // _FFN_HALO
    kern = functools.partial(_ffn_prompt_kernel, tm=tm, tiles_per_seq=tps)
    return pl.pallas_call(
        kern,
        out_shape=(jax.ShapeDtypeStruct((batch * seq, D_MODEL), F32), jax.ShapeDtypeStruct((batch, 8, D_FF), F32)),
        grid=(batch * tps,),
        in_specs=[pl.BlockSpec((tm, D_MODEL), lambda i: (i, 0)),
                  pl.BlockSpec((_FFN_HALO, D_MODEL), lambda i: (jnp.maximum(i * hb - 1, 0), 0)),
                  pl.BlockSpec((tm, PLE_DIM), lambda i: (i, 0))] + _ffn_weight_specs(),
        out_specs=(pl.BlockSpec((tm, D_MODEL), lambda i: (i, 0)),
                   pl.BlockSpec((1, 8, D_FF), lambda i: (i // tps, 0, 0))),
        scratch_shapes=[pltpu.VMEM((_FFN_HALO + tm, D_MODEL), BF16), pltpu.VMEM((_FFN_HALO + tm, D_FF), F32)],
        compiler_params=_cparams("arbitrary"), name="ffn_prompt",
    )(x, x, p, *fw)


def _layer_prompt(x, p, w, tm=256):
    batch, seq, _ = x.shape
    m = batch * seq
    x2 = x.reshape(m, D_MODEL)
    gla, small, nq, kc_t, ks_t, kw_t, nkb_t, mg = _proj_in(x2, w["g_mix_pre"], w["w_in_pack"], tm, w["w_kv_t"], seq)
    og, gla_state = _gla_prompt(gla, small, w["w_gate_pack"], w["b_gla_gate"], w["g_gla_norm"], batch, seq)

    n_pages = seq // PAGE_SIZE
    ident_pt = jnp.arange(batch * n_pages, dtype=jnp.int32).reshape(batch, n_pages)
    kvc = _compress(ident_pt, kc_t.reshape(batch, NSA_KV_GROUPS, 2, NSA_DH, seq), *w["cmp"])
    oc, selm = _cmp_topk_prompt(nq, _permute_cmp(kvc), small, batch, seq)
    on = _slc_win_prompt(nq, selm, nkb_t, small, oc, batch, seq)

    x1 = _mix_out(og, on, mg, x2, w["w_gla_out"], w["w_nsa_out"], w["w_out"], w["g_mix_post"], tm)
    y, alast = _ffn_prompt(x1, p.reshape(m, PLE_DIM), w["ffn"], batch, seq, tm)

    win = min(WINDOW, seq)
    return (y.reshape(batch, seq, D_MODEL), _token_major(kc_t, batch, seq), _token_major(ks_t, batch, seq),
            _token_major(kw_t[:, :, seq - win:], batch, win), gla_state, alast[:, 8 - (CONV_W - 1):])


def _row_to_col(row):
    n = row.shape[-1]
    ri = lax.broadcasted_iota(jnp.int32, (n, n), 0)
    ci = lax.broadcasted_iota(jnp.int32, (n, n), 1)
    return jnp.sum(jnp.where(ri == ci, jnp.broadcast_to(row, (n, n)), 0.0), axis=-1, keepdims=True)


def _gla_step_kernel(gla_ref, small_ref, s0_ref, wg_ref, bg_ref, gn_ref, og_ref, s1_ref):
    row = gla_ref[0]
    small = small_ref[0]
    H, DK, DV = GLA_HEADS, GLA_DK, GLA_DV
    outs = []
    for h in range(H):
        q = row[:, h * DK:(h + 1) * DK] * (DK ** -0.5)
        k = row[:, H * DK + h * DK:H * DK + (h + 1) * DK]
        v = row[:, 2 * H * DK + h * DV:2 * H * DK + (h + 1) * DV]
        r = row[:, 2 * H * DK + H * DV + h * DV:2 * H * DK + H * DV + (h + 1) * DV]
        alpha = jnp.exp(_log_decay(small, wg_ref[:, h * DK:(h + 1) * DK], bg_ref[:, h * DK:(h + 1) * DK]))
        s0 = s0_ref[0, h]
        s1_ref[0, h] = _row_to_col(alpha) * s0 + _row_to_col(k) * v
        o = _dot((q * alpha).astype(BF16), s0.astype(BF16)) + jnp.sum(q * k, axis=-1, keepdims=True) * v
        outs.append(_gla_out(o, gn_ref[:, h * DV:(h + 1) * DV], r))
    og_ref[0] = jnp.concatenate(outs, axis=1)


def _gla_sample(gla, small, state, w_gate_pack, b_gate, g_norm):
    nb = gla.shape[0]
    H = GLA_HEADS
    og, s1 = pl.pallas_call(
        _gla_step_kernel,
        out_shape=(jax.ShapeDtypeStruct((nb, 1, H * GLA_DV), BF16), jax.ShapeDtypeStruct(state.shape, F32)),
        grid=(nb,),
        in_specs=[pl.BlockSpec((1, 1, _GLA_W), lambda b: (b, 0, 0)),
                  pl.BlockSpec((1, 1, _SMALL_W), lambda b: (b, 0, 0)),
                  pl.BlockSpec((1, H, GLA_DK, GLA_DV), lambda b: (b, 0, 0, 0)),
                  _const_spec((_SMALL_W, H * GLA_DK)), _const_spec((1, H * GLA_DK)), _const_spec((1, H * GLA_DV))],
        out_specs=(pl.BlockSpec((1, 1, H * GLA_DV), lambda b: (b, 0, 0)),
                   pl.BlockSpec((1, H, GLA_DK, GLA_DV), lambda b: (b, 0, 0, 0))),
        compiler_params=_cparams("parallel"), name="gla_step",
    )(gla.reshape(nb, 1, _GLA_W), small.reshape(nb, 1, _SMALL_W), state, w_gate_pack,
      b_gate.reshape(1, -1), g_norm.reshape(1, -1))
    return og.reshape(nb, H * GLA_DV), s1


def _head_group_rows(x_groups):
    grp = lax.broadcasted_iota(jnp.int32, x_groups[0].shape, 0) // NSA_GROUP_HEADS
    out = jnp.zeros(x_groups[0].shape, F32)
    for g, xg in enumerate(x_groups):
        out = jnp.where(grp == g, xg, out)
    return out


def _cmp_topk_sample_kernel(q_ref, kvc_ref, oc_ref, idx_ref, *, t, n_lane, n_pick):
    q = q_ref[0]
    n_cmp = kvc_ref.shape[2]
    lane = lax.broadcasted_iota(jnp.int32, (1, n_cmp), 1)
    blk_end = (4 * (lane % n_lane) + lane // n_lane) * CMP_STRIDE + (CMP_LEN - 1)
    ocs, imps = [], []
    for g in range(NSA_KV_GROUPS):
        s = _dot_nt(q, kvc_ref[0, 2 * g].astype(BF16))
        p = _masked_softmax(s, jnp.broadcast_to(blk_end <= t, s.shape))
        ocs.append(_dot(p.astype(BF16), kvc_ref[0, 2 * g + 1].astype(BF16)))
        r0 = NSA_GROUP_HEADS * g
        psum = ((p[r0:r0 + 1] + p[r0 + 1:r0 + 2]) + p[r0 + 2:r0 + 3]) + p[r0 + 3:r0 + 4]
        imps.append(_block_importance(psum, n_lane))
    oc_ref[0] = _head_group_rows(ocs)

    imp = jnp.concatenate(imps + [jnp.zeros((8 - NSA_KV_GROUPS, n_lane), F32)], axis=0)
    blk = lax.broadcasted_iota(jnp.int32, imp.shape, 1)
    cur = t // SLC_BLOCK
    forced = (blk == 0) | (blk == cur) | (blk == cur - 1)
    score = jnp.where(blk <= cur, jnp.where(forced, FORCE_SCORE, imp), -jnp.inf)
    idx_out = jnp.zeros(imp.shape, jnp.int32)
    for i in range(n_pick):
        mx = jnp.max(score, axis=-1, keepdims=True)
        idx = jnp.min(jnp.where(score == mx, blk, n_lane), axis=-1, keepdims=True)
        idx_out = jnp.where(blk == i, idx, idx_out)
        score = jnp.where(blk == idx, -jnp.inf, score)
    idx_ref[0] = idx_out


def _attend_with_new_token(qpad, qf, kv, valid, new_row):
    DH = NSA_DH
    is_k = lax.broadcasted_iota(jnp.int32, kv.shape, 0) < DH
    s = jnp.where(valid, _dot(qpad, jnp.where(is_k, kv, jnp.zeros_like(kv))), -jnp.inf)
    s_n = jnp.sum(qf * new_row[:, 0:DH].astype(BF16).astype(F32), axis=-1, keepdims=True)
    m = jnp.maximum(jnp.max(s, axis=-1, keepdims=True), s_n)
    e = jnp.exp2(s - m)
    en = jnp.exp2(s_n - m)
    num = _dot_nt(e.astype(BF16), kv)[:, DH:] + en * new_row[:, DH:]
    return num / (jnp.sum(e, axis=-1, keepdims=True) + en)


def _slc_win_sample_kernel(idx_ref, pt_ref, *refs, t, win, n_pick):
    G = NSA_KV_GROUPS
    GW = 2 * NSA_DH
    kv_refs = refs[:G * n_pick]
    q_ref, ksn_ref, kwn_ref, cw_ref, small_ref, oc_ref, on_ref = refs[G * n_pick:]
    b = pl.program_id(0)
    q = q_ref[0]
    qf = q.astype(F32)
    qpad = jnp.concatenate([q, jnp.zeros_like(q)], axis=1)
    ksn = ksn_ref[0]
    kwn = kwn_ref[0]
    blocks_per_page = PAGE_SIZE // SLC_BLOCK
    lane_blk = (lax.broadcasted_iota(jnp.int32, (1, n_pick * PAGE_SIZE), 1) // SLC_BLOCK) % blocks_per_page
    pos = (t - win) + lax.broadcasted_iota(jnp.int32, (1, win), 1)
    w_valid = (pos <= t) & (pos > t - WINDOW) & (pos >= 0)
    o_s, o_w = [], []
    for g in range(G):
        ks = range(g * n_pick, (g + 1) * n_pick)
        kv = jnp.concatenate([kv_refs[i][0, 0].reshape(GW, PAGE_SIZE) for i in ks], axis=1).astype(BF16)
        want = jnp.concatenate([jnp.full((1, PAGE_SIZE), idx_ref[b, i] % blocks_per_page, jnp.int32) for i in ks], axis=1)
        o_s.append(_attend_with_new_token(qpad, qf, kv, lane_blk == want, ksn[:, g * GW:(g + 1) * GW]))
        kvw = cw_ref[0, g].reshape(GW, win).astype(BF16)
        o_w.append(_attend_with_new_token(qpad, qf, kvw, w_valid, kwn[:, g * GW:(g + 1) * GW]))
    o_s = _head_group_rows(o_s)
    o_w = _head_group_rows(o_w)
    sig = jnp.broadcast_to(_sigmoid(small_ref[0]), (NSA_HEADS, _SMALL_W))
    lane = lax.broadcasted_iota(jnp.int32, sig.shape, 1)
    head = lax.broadcasted_iota(jnp.int32, sig.shape, 0)
    gate = [jnp.sum(jnp.where(lane == _NG_LANE0 + 3 * head + x, sig, 0.0), axis=-1, keepdims=True) for x in range(3)]
    on_ref[0] = (gate[0] * oc_ref[0] + gate[1] * o_s + gate[2] * o_w).astype(BF16)


def _nsa_sample(nq, ks_new, kw_new, small, kvc_perm, cache_slc, cache_win, page_table):
    nb, n_pages = page_table.shape
    past = n_pages * PAGE_SIZE
    t = past
    n_lane = past // SLC_BLOCK
    n_pick = SLC_TOPK - 1
    assert kvc_perm.shape[2] == 4 * n_lane and n_lane >= SLC_TOPK and n_lane % 128 == 0
    win = cache_win.shape[1]
    q3 = nq.reshape(nb, NSA_HEADS, NSA_DH)
    G = NSA_KV_GROUPS

    oc, idx = pl.pallas_call(
        functools.partial(_cmp_topk_sample_kernel, t=t, n_lane=n_lane, n_pick=n_pick),
        out_shape=(jax.ShapeDtypeStruct((nb, NSA_HEADS, NSA_DH), F32), jax.ShapeDtypeStruct((nb, 8, n_lane), jnp.int32)),
        grid=(nb,),
        in_specs=[pl.BlockSpec((1, NSA_HEADS, NSA_DH), lambda b: (b, 0, 0)),
                  pl.BlockSpec((1, 2 * G, 4 * n_lane, NSA_DH), lambda b: (b, 0, 0, 0))],
        out_specs=(pl.BlockSpec((1, NSA_HEADS, NSA_DH), lambda b: (b, 0, 0)),
                   pl.BlockSpec((1, 8, n_lane), lambda b: (b, 0, 0))),
        compiler_params=_cparams("parallel"), name="cmp_topk_sample",
    )(q3, kvc_perm)
    blk_idx = idx[:, :G, :n_pick].reshape(nb, G * n_pick)

    half = PAGE_SIZE // SLC_BLOCK
    slc_t = _feature_major(cache_slc)
    win_t = _feature_major(cache_win)

    def kv_spec(i):
        def index(b, idx_ref, pt_ref):
            return (pt_ref[b, idx_ref[b, i] // half], i // n_pick, 0, 0, 0)
        return pl.BlockSpec((1, 1, 2, NSA_DH, PAGE_SIZE), index)

    per_seq = lambda shape: pl.BlockSpec((1,) + shape, lambda b, *_: (b,) + (0,) * len(shape))
    grid_spec = pltpu.PrefetchScalarGridSpec(
        num_scalar_prefetch=2, grid=(nb,),
        in_specs=[kv_spec(i) for i in range(G * n_pick)] + [
            per_seq((NSA_HEADS, NSA_DH)), per_seq((1, NSA_KV_W)), per_seq((1, NSA_KV_W)),
            per_seq((G, 2, NSA_DH, win)), per_seq((1, _SMALL_W)), per_seq((NSA_HEADS, NSA_DH))],
        out_specs=per_seq((NSA_HEADS, NSA_DH)))
    on = pl.pallas_call(
        functools.partial(_slc_win_sample_kernel, t=t, win=win, n_pick=n_pick),
        out_shape=jax.ShapeDtypeStruct((nb, NSA_HEADS, NSA_DH), BF16), grid_spec=grid_spec,
        compiler_params=_cparams("parallel"), name="slc_win_sample",
    )(blk_idx, page_table, *([slc_t] * (G * n_pick)), q3,
      ks_new.reshape(nb, 1, NSA_KV_W), kw_new.reshape(nb, 1, NSA_KV_W),
      win_t, small.reshape(nb, 1, _SMALL_W), oc)
    return on.reshape(nb, NSA_HEADS * NSA_DH)


def _ffn_sample_kernel(x_ref, s0_ref, s1_ref, p_ref, gpre_ref, wup_ref, cw_ref, cb_ref, wdn_ref, gpost_ref,
                       gple_ref, wpg_ref, wpp_ref, y_ref, a_ref):
    x = x_ref[...]
    ab = _dot(_rms(x, gpre_ref[...]).astype(BF16), wup_ref[...])
    a = ab[:, 0:D_FF]
    cw = cw_ref[...]
    conv = (s0_ref[...] * cw[0:1] + s1_ref[...] * cw[1:2] + a * cw[2:3]) + cb_ref[...]
    y_ref[...] = _ffn_tail(x, conv, ab[:, D_FF:], p_ref[...], wdn_ref, gpost_ref, gple_ref, wpg_ref, wpp_ref)
    a_ref[...] = a


def _ffn_sample(x, conv_state, p, fw):
    nb = x.shape[0]
    full = lambda w: pl.BlockSpec((nb, w), lambda i: (0, 0))
    return pl.pallas_call(
        _ffn_sample_kernel,
        out_shape=(jax.ShapeDtypeStruct((nb, D_MODEL), F32), jax.ShapeDtypeStruct((nb, D_FF), F32)),
        grid=(1,),
        in_specs=[full(D_MODEL), full(D_FF), full(D_FF), full(PLE_DIM)] + _ffn_weight_specs(),
        out_specs=(full(D_MODEL), full(D_FF)),
        compiler_params=_cparams("arbitrary"), name="ffn_sample",
    )(x, conv_state[:, 0], conv_state[:, 1], p, *fw)


def _layer_sample(x, p, cache_cmp, cache_slc, cache_win, state_gla, state_conv, page_table, w):
    nb, s, _ = x.shape
    assert s == 1
    x2 = x.reshape(nb, D_MODEL)
    gla, small, nq, kc, ks, kw, mg = _proj_in(x2, w["g_mix_pre"], w["w_in_pack"], nb)
    og, gla_state = _gla_sample(gla, small, state_gla, w["w_gate_pack"], w["b_gla_gate"], w["g_gla_norm"])

    kvc = _compress(page_table, _feature_major(cache_cmp), *w["cmp"])
    on = _nsa_sample(nq, ks, kw, small, _permute_cmp(kvc), cache_slc, cache_win, page_table)

    x1 = _mix_out(og, on, mg, x2, w["w_gla_out"], w["w_nsa_out"], w["w_out"], w["g_mix_post"], nb)
    y, a = _ffn_sample(x1, state_conv, p.reshape(nb, PLE_DIM), w["ffn"])
    kv_shape = (nb, 1, NSA_KV_GROUPS, 2, NSA_DH)
    conv_new = jnp.stack([state_conv[:, 1], a], axis=1)
    return (y.reshape(nb, 1, D_MODEL), kc.reshape(kv_shape), ks.reshape(kv_shape), kw.reshape(kv_shape),
            gla_state, conv_new)


def _prep_weights(g_mix_pre, g_mix_post, g_ffn_pre, g_ffn_post, g_ple, w_in, w_gla_gate_up, b_gla_gate,
                  g_gla_norm, w_gla_out, w_cmp1, b_cmp1, w_cmp2, b_cmp2, pe_cmp, w_nsa_out, w_out, w_ffn_up,
                  conv_ffn_w, b_conv_ffn, w_ffn_down, w_ple_proj, w_ple_gate):
    pe_flat = pe_cmp.transpose(1, 0, 2).reshape(2, 1, CMP_LEN * NSA_DH)
    row = lambda v: v.reshape(1, -1)
    return dict(
        g_mix_pre=g_mix_pre, g_mix_post=g_mix_post, w_in_pack=_pack_w_in(w_in), w_kv_t=_pack_w_kv_t(w_in),
        w_gate_pack=_pack_w_gate(w_gla_gate_up), b_gla_gate=b_gla_gate, g_gla_norm=g_gla_norm,
        w_gla_out=w_gla_out.astype(BF16), w_nsa_out=w_nsa_out.astype(BF16), w_out=w_out.astype(BF16),
        cmp=(_pack_w_cmp1(w_cmp1), w_cmp1.astype(BF16), pe_flat, b_cmp1.reshape(2, 1, CMP_HIDDEN),
             w_cmp2.astype(BF16), b_cmp2.reshape(2, 1, NSA_DH)),
        ffn=(row(g_ffn_pre), w_ffn_up.astype(BF16), conv_ffn_w, row(b_conv_ffn), w_ffn_down.astype(BF16),
             row(g_ffn_post), row(g_ple), w_ple_gate.astype(BF16), w_ple_proj.astype(BF16)),
    )


def kernel(x_prompt, x_sample, cache_cmp_kv, cache_slc_kv, cache_win_kv, state_gla, state_ffn_conv, page_table,
           p_prompt, p_sample, g_mix_pre, g_mix_post, g_ffn_pre, g_ffn_post, g_ple, w_in, w_gla_gate_up,
           b_gla_gate, g_gla_norm, w_gla_out, w_cmp1, b_cmp1, w_cmp2, b_cmp2, pe_cmp, w_nsa_out, w_out, w_ffn_up,
           conv_ffn_w, b_conv_ffn, w_ffn_down, w_ple_proj, w_ple_gate):
    layer_weights = (g_mix_pre, g_mix_post, g_ffn_pre, g_ffn_post, g_ple, w_in, w_gla_gate_up, b_gla_gate,
                     g_gla_norm, w_gla_out, w_cmp1, b_cmp1, w_cmp2, b_cmp2, pe_cmp, w_nsa_out, w_out, w_ffn_up,
                     conv_ffn_w, b_conv_ffn, w_ffn_down, w_ple_proj, w_ple_gate)
    depth = w_in.shape[0]
    yp, ys = x_prompt, x_sample
    extras_p, extras_s = [], []
    for i in range(depth):
        w = _prep_weights(*[v[i] for v in layer_weights])
        yp, *ep = _layer_prompt(yp, p_prompt[i], w)
        ys, *es = _layer_sample(ys, p_sample[i], cache_cmp_kv[i], cache_slc_kv[i], cache_win_kv[i], state_gla[i],
                                state_ffn_conv[i], page_table, w)
        extras_p.append(ep)
        extras_s.append(es)
    stack = lambda rows, j: jnp.stack([r[j] for r in rows])
    outs = [yp, ys]
    for j in range(5):
        outs += [stack(extras_p, j), stack(extras_s, j)]
    return tuple(outs)
```

```python
import functools

import numpy as np
import jax
import jax.numpy as jnp
from jax import lax
from jax.experimental import pallas as pl
from jax.experimental.pallas import tpu as pltpu

F32 = jnp.float32
BF16 = jnp.bfloat16

D_MODEL = 1024
PAGE_SIZE = 128
GLA_HEADS = 4
GLA_DK = 128
GLA_DV = 256
GLA_GATE_RANK = 16
GLA_TAU = 16.0
GLA_SUB = 16
NSA_HEADS = 16
NSA_KV_GROUPS = 4
NSA_GROUP_HEADS = 4
NSA_DH = 64
NSA_KV_W = 2 * NSA_KV_GROUPS * NSA_DH
CMP_LEN = 32
CMP_STRIDE = 16
CMP_HIDDEN = 128
SLC_BLOCK = 64
SLC_TOPK = 16
WINDOW = 512
FORCE_SCORE = 1.0e4
D_FF = 2816
CONV_W = 3
PLE_DIM = 256
EPS = 1e-6

V7X_VMEM_BYTES = 64 * 1024 * 1024
VMEM_LIMIT = V7X_VMEM_BYTES - 8 * 1024 * 1024

_GLA_OFF, _GLA_W = 0, 2 * GLA_HEADS * GLA_DK + 2 * GLA_HEADS * GLA_DV
_SMALL_OFF, _SMALL_W = _GLA_OFF + _GLA_W, 128
_NQ_OFF, _NQ_W = _SMALL_OFF + _SMALL_W, NSA_HEADS * NSA_DH
_NK_OFF, _NK_W = _NQ_OFF + _NQ_W, 3 * NSA_KV_W
_MG_OFF, _MG_W = _NK_OFF + _NK_W, 2 * D_MODEL
_PACK_W = _MG_OFF + _MG_W
_NG_LANE0 = GLA_GATE_RANK

_Q_SCALE = float(NSA_DH ** -0.5 * np.log2(np.e))

_MASK_BIG = float(2.0 ** 100)


def _cparams(*sem):
    return pltpu.CompilerParams(dimension_semantics=sem, vmem_limit_bytes=VMEM_LIMIT)


def _const_spec(shape):
    nd = len(shape)
    return pl.BlockSpec(shape, lambda *_: (0,) * nd, pipeline_mode=pl.Buffered(1))


def _rms(x, g):
    return x * lax.rsqrt(jnp.mean(x * x, axis=-1, keepdims=True) + EPS) * g


def _gelu_tanh(x):
    return 0.5 * x * (1.0 + jnp.tanh(np.sqrt(2.0 / np.pi).astype(np.float32) * (x + 0.044715 * (x * x * x))))


def _sigmoid(x):
    return 1.0 / (1.0 + jnp.exp(-x))


def _dot(a, b):
    return jnp.dot(a, b, preferred_element_type=F32)


def _dot_nt(a, b):
    return lax.dot_general(a, b, (((1,), (1,)), ((), ())), preferred_element_type=F32)


def _dot_tn(a, b):
    return lax.dot_general(a, b, (((0,), (0,)), ((), ())), preferred_element_type=F32)


def _masked_softmax(s, mask):
    s = jnp.where(mask, s, -jnp.inf)
    m = jnp.max(s, axis=-1, keepdims=True)
    m = jnp.where(m > -jnp.inf, m, 0.0)
    e = jnp.exp2(s - m)
    d = jnp.sum(e, axis=-1, keepdims=True)
    return e / jnp.where(d > 0, d, 1.0)


def _lane_column(x, col):
    lane = lax.broadcasted_iota(jnp.int32, x.shape, 1)
    return jnp.sum(jnp.where(lane == col, x, 0.0), axis=-1, keepdims=True)


def _pack_w_in(w_in):
    sizes = (512, 512, 1024, 1024, GLA_GATE_RANK, 1024, NSA_KV_W, NSA_KV_W, NSA_KV_W, NSA_HEADS * 3, 2 * D_MODEL)
    pts = [int(v) for v in np.cumsum(sizes)[:-1]]
    gq, gk, gv, gr, ga, nq, nkc, nks, nkw, ng, mg = jnp.split(w_in, pts, axis=1)
    small = jnp.concatenate([ga, ng, jnp.zeros((D_MODEL, _SMALL_W - GLA_GATE_RANK - NSA_HEADS * 3), w_in.dtype)], axis=1)
    return jnp.concatenate([gq, gk, gv, gr, small, nq, nkc, nks, nkw, mg], axis=1).astype(BF16)


def _pack_w_kv_t(w_in):
    lo = 512 + 512 + 1024 + 1024 + GLA_GATE_RANK + NSA_HEADS * NSA_DH
    return w_in[:, lo:lo + _NK_W].T.astype(BF16)


def _proj_in_kernel(x_ref, g_ref, w_ref, *refs, kv_major):
    h = _rms(x_ref[...], g_ref[...]).astype(BF16)

    def seg(off, width):
        return _dot(h, w_ref[:, off:off + width])

    if kv_major:
        wkv_ref, gla_ref, small_ref, nq_ref, kc_ref, ks_ref, kw_ref, nkb_ref, mg_ref = refs
        nk = _dot_nt(wkv_ref[...], h)
        kc_ref[0] = nk[0:NSA_KV_W]
        ks_ref[0] = nk[NSA_KV_W:2 * NSA_KV_W]
        kw_ref[0] = nk[2 * NSA_KV_W:3 * NSA_KV_W]
        nkb_ref[0] = nk.astype(BF16)
    else:
        gla_ref, small_ref, nq_ref, kc_ref, ks_ref, kw_ref, mg_ref = refs
        nk = seg(_NK_OFF, _NK_W)
        kc_ref[...] = nk[:, 0:NSA_KV_W]
        ks_ref[...] = nk[:, NSA_KV_W:2 * NSA_KV_W]
        kw_ref[...] = nk[:, 2 * NSA_KV_W:3 * NSA_KV_W]
    gla_ref[...] = seg(_GLA_OFF, _GLA_W)
    small_ref[...] = seg(_SMALL_OFF, _SMALL_W)
    nq_ref[...] = (seg(_NQ_OFF, _NQ_W) * _Q_SCALE).astype(BF16)
    mg_ref[...] = seg(_MG_OFF, _MG_W)


def _proj_in(x, g_pre, w_pack, tm, w_kv_t=None, seq=None):
    m = x.shape[0]
    assert m % tm == 0
    kv_major = w_kv_t is not None
    row = lambda w: pl.BlockSpec((tm, w), lambda i: (i, 0))
    sds = jax.ShapeDtypeStruct
    in_specs = [row(D_MODEL), _const_spec((1, D_MODEL)), _const_spec((D_MODEL, _PACK_W))]
    args = [x, g_pre.reshape(1, D_MODEL), w_pack]
    head = [(sds((m, _GLA_W), F32), row(_GLA_W)), (sds((m, _SMALL_W), F32), row(_SMALL_W)),
            (sds((m, _NQ_W), BF16), row(_NQ_W))]
    if kv_major:
        assert seq % tm == 0
        tps = seq // tm
        col = lambda w: pl.BlockSpec((1, w, tm), lambda i: (i // tps, 0, i % tps))
        in_specs.append(_const_spec((_NK_W, D_MODEL)))
        args.append(w_kv_t)
        kv = [(sds((m // seq, NSA_KV_W, seq), F32), col(NSA_KV_W))] * 3 + [(sds((m // seq, _NK_W, seq), BF16), col(_NK_W))]
    else:
        kv = [(sds((m, NSA_KV_W), F32), row(NSA_KV_W))] * 3
    outs = head + kv + [(sds((m, _MG_W), F32), row(_MG_W))]
    return pl.pallas_call(
        functools.partial(_proj_in_kernel, kv_major=kv_major),
        out_shape=tuple(o[0] for o in outs), grid=(m // tm,), in_specs=in_specs,
        out_specs=tuple(o[1] for o in outs),
        compiler_params=_cparams("parallel"), name="proj_in",
    )(*args)


def _log_decay(small, wg, bg):
    x = _dot(small.astype(BF16), wg) + bg
    return (jnp.minimum(x, 0.0) - jnp.log1p(jnp.exp(-jnp.abs(x)))) * (1.0 / GLA_TAU)


def _pack_w_gate(w_gate_up):
    pad = jnp.zeros((_SMALL_W - GLA_GATE_RANK, GLA_HEADS * GLA_DK), w_gate_up.dtype)
    return jnp.concatenate([w_gate_up, pad], axis=0).astype(BF16)


def _gla_out(o, gn, r):
    o = o * lax.rsqrt(jnp.mean(o * o, axis=-1, keepdims=True) + EPS) * gn
    return (o * (r * _sigmoid(r))).astype(BF16)


def _gla_chunk_kernel(q_ref, k_ref, v_ref, r_ref, small_ref, wg_ref, bg_ref, gn_ref, og_ref, st_ref,
                      s_scr, kp_scr, bp_scr, *, chunk):
    c = pl.program_id(1)
    SUB, DK, DV = GLA_SUB, GLA_DK, GLA_DV

    @pl.when(c == 0)
    def _():
        s_scr[...] = jnp.zeros_like(s_scr)
        kp_scr[:, 0:SUB, :] = jnp.zeros((GLA_HEADS, SUB, DK), F32)
        bp_scr[:, 0:SUB, :] = jnp.zeros((GLA_HEADS, SUB, DK), F32)

    small = small_ref[...]
    for h in range(GLA_HEADS):
        kh = slice(h * DK, (h + 1) * DK)
        vh = slice(h * DV, (h + 1) * DV)
        la = _log_decay(small, wg_ref[:, kh], bg_ref[:, kh])
        o, st_new = _gla_head_chunk(q_ref[:, kh] * (DK ** -0.5), k_ref[:, kh], v_ref[:, vh].astype(BF16), la,
                                    s_scr[h], kp_scr.at[h], bp_scr.at[h], chunk)
        s_scr[h] = st_new
        og_ref[:, vh] = _gla_out(o, gn_ref[:, vh], r_ref[:, vh])

    @pl.when(c == pl.num_programs(1) - 1)
    def _():
        for h in range(GLA_HEADS):
            st_ref[0, h] = s_scr[h].T


def _gla_head_chunk(q, k, vb, la, st, kp_ref, bp_ref, chunk):
    C, SUB = chunk, GLA_SUB
    ri = lax.broadcasted_iota(jnp.int32, (C, C), 0)
    ci = lax.broadcasted_iota(jnp.int32, (C, C), 1)
    tri = jnp.where(ci <= ri, 1.0, 0.0).astype(BF16)
    hi = la.astype(BF16)
    r1 = la - hi.astype(F32)
    mid = r1.astype(BF16)
    lo = (r1 - mid.astype(F32)).astype(BF16)
    b = _dot(tri, hi) + _dot(tri, mid) + _dot(tri, lo)

    o = _dot_nt((q * jnp.exp(b)).astype(BF16), st.astype(BF16))
    b_last = b[C - 1:C, :]
    kdec = (k * jnp.exp(b_last - b)).astype(BF16)
    st_new = st * jnp.exp(b_last) + _dot_tn(vb, kdec)

    row = lax.broadcasted_iota(jnp.int32, (C, GLA_DK), 0)
    blocks = [jnp.zeros((SUB, C), F32)]
    for i in range(1, C // SUB):
        beta = b[SUB * i - 1:SUB * i, :]
        qi = (q[SUB * i:SUB * (i + 1)] * jnp.exp(b[SUB * i:SUB * (i + 1)] - beta)).astype(BF16)
        ki = (k * jnp.exp(jnp.where(row < SUB * i, beta - b, -jnp.inf))).astype(BF16)
        blocks.append(_dot_nt(qi, ki))
    a = jnp.concatenate(blocks, axis=0)

    kp_ref[SUB:SUB + C, :] = k
    bp_ref[SUB:SUB + C, :] = b
    tmod = ri % SUB
    for j in range(SUB):
        ks = kp_ref[SUB - j:SUB - j + C, :]
        bs = bp_ref[SUB - j:SUB - j + C, :]
        aj = jnp.sum(q * ks * jnp.exp(b - bs), axis=-1, keepdims=True)
        a = a + jnp.where((ci == ri - j) & (tmod >= j), aj, 0.0)
    return o + _dot(a.astype(BF16), vb), st_new


def _gla_prompt(gla, small, w_gate_pack, b_gate, g_norm, batch, seq, chunk=128):
    n = seq // chunk
    assert seq % chunk == 0 and chunk % GLA_SUB == 0
    H = GLA_HEADS
    kw, vw = H * GLA_DK, H * GLA_DV
    assert vw == 2 * kw
    kern = functools.partial(_gla_chunk_kernel, chunk=chunk)
    return pl.pallas_call(
        kern,
        out_shape=(jax.ShapeDtypeStruct((batch * seq, vw), BF16),
                   jax.ShapeDtypeStruct((batch, H, GLA_DK, GLA_DV), F32)),
        grid=(batch, n),
        in_specs=[
            pl.BlockSpec((chunk, kw), lambda b, c: (b * n + c, 0)),
            pl.BlockSpec((chunk, kw), lambda b, c: (b * n + c, 1)),
            pl.BlockSpec((chunk, vw), lambda b, c: (b * n + c, 1)),
            pl.BlockSpec((chunk, vw), lambda b, c: (b * n + c, 2)),
            pl.BlockSpec((chunk, _SMALL_W), lambda b, c: (b * n + c, 0)),
            _const_spec((_SMALL_W, kw)), _const_spec((1, kw)), _const_spec((1, vw)),
        ],
        out_specs=(pl.BlockSpec((chunk, vw), lambda b, c: (b * n + c, 0)),
                   pl.BlockSpec((1, H, GLA_DK, GLA_DV), lambda b, c: (b, 0, 0, 0))),
        scratch_shapes=[pltpu.VMEM((H, GLA_DV, GLA_DK), F32),
                        pltpu.VMEM((H, GLA_SUB + chunk, GLA_DK), F32),
                        pltpu.VMEM((H, GLA_SUB + chunk, GLA_DK), F32)],
        compiler_params=_cparams("parallel", "arbitrary"), name="gla_chunk",
    )(gla, gla, gla, gla, small, w_gate_pack, b_gate.reshape(1, -1), g_norm.reshape(1, -1))


def _pack_w_cmp1(w1):
    w = w1.reshape(2, 2, CMP_STRIDE, NSA_DH, CMP_HIDDEN)
    per_c = jnp.concatenate([w[:, 0], w[:, 1]], axis=-1)
    z = jnp.zeros_like(per_c[0])
    w16 = jnp.concatenate([jnp.concatenate([per_c[0], z], axis=-1),
                           jnp.concatenate([z, per_c[1]], axis=-1)], axis=1)
    return w16.reshape(CMP_STRIDE // 2, 4 * NSA_DH, 4 * CMP_HIDDEN).astype(BF16)


def _compress_kernel(pt_ref, kv_hbm, wp_ref, w1_ref, pef_ref, b1_ref, w2_ref, b2_ref, out_ref,
                     buf, tok_scr, hb_scr, sem, *, n_pages, pages_per_row):
    bidx = pl.program_id(0)
    slot = bidx % 2
    n_chunk = n_pages * (PAGE_SIZE // CMP_STRIDE)
    GW = 2 * NSA_DH

    def page_copy(seq, p, slt):
        page = pt_ref[seq, p]
        off = pl.multiple_of((page % pages_per_row) * PAGE_SIZE, PAGE_SIZE)
        src = kv_hbm.at[page // pages_per_row, :, :, :, pl.ds(off, PAGE_SIZE)]
        dst = buf.at[slt, :, :, :, pl.ds(pl.multiple_of(p * PAGE_SIZE, PAGE_SIZE), PAGE_SIZE)]
        return pltpu.make_async_copy(src, dst, sem.at[slt])

    def start_pages(seq, slt):
        def start(p, carry):
            page_copy(seq, p, slt).start()
            return carry
        lax.fori_loop(0, n_pages, start, 0)

    @pl.when(bidx == 0)
    def _():
        start_pages(0, 0)

    @pl.when(bidx + 1 < pl.num_programs(0))
    def _():
        start_pages(bidx + 1, 1 - slot)

    def wait(p, carry):
        page_copy(bidx, p, slot).wait()
        return carry

    lax.fori_loop(0, n_pages, wait, 0)

    HID = CMP_HIDDEN
    hb_scr[n_chunk:n_chunk + 8, :] = jnp.zeros((8, HID), F32)
    const = [_dot(pef_ref[c].astype(BF16), w1_ref[c]) + b1_ref[c] for c in range(2)]
    for g in range(NSA_KV_GROUPS):

        def to_token_major(p, carry):
            off = pl.multiple_of(p * PAGE_SIZE, PAGE_SIZE)
            tok_scr[pl.ds(off, PAGE_SIZE), :] = buf[slot, g, :, :, pl.ds(off, PAGE_SIZE)].reshape(GW, PAGE_SIZE).T
            return carry

        lax.fori_loop(0, n_pages, to_token_major, 0, unroll=8)
        acc = jnp.zeros((n_chunk, 4 * HID), F32)
        for p in range(0, CMP_STRIDE, 2):
            xs = [tok_scr[pl.ds(p + i, n_chunk, stride=CMP_STRIDE), :].astype(BF16) for i in range(2)]
            acc = acc + _dot(jnp.concatenate(xs, axis=1), wp_ref[p // 2])
        for c in range(2):
            hb_scr[0:n_chunk, :] = acc[:, (2 * c + 1) * HID:(2 * c + 2) * HID]
            hid = acc[:, 2 * c * HID:(2 * c + 1) * HID] + hb_scr[pl.ds(1, n_chunk), :] + const[c]
            out_ref[0, g * 2 + c] = _dot(_gelu_tanh(hid).astype(BF16), w2_ref[c]) + b2_ref[c]


def _compress(page_table, kv_t, wp, w1b, pe_flat, b1, w2b, b2):
    batch, n_pages = page_table.shape
    n_chunk = n_pages * (PAGE_SIZE // CMP_STRIDE)
    assert kv_t.shape[-1] % PAGE_SIZE == 0
    kern = functools.partial(_compress_kernel, n_pages=n_pages, pages_per_row=kv_t.shape[-1] // PAGE_SIZE)
    grid_spec = pltpu.PrefetchScalarGridSpec(
        num_scalar_prefetch=1, grid=(batch,),
        in_specs=[pl.BlockSpec(memory_space=pl.ANY),
                  _const_spec((CMP_STRIDE // 2, 4 * NSA_DH, 4 * CMP_HIDDEN)),
                  _const_spec((2, CMP_LEN * NSA_DH, CMP_HIDDEN)),
                  _const_spec((2, 1, CMP_LEN * NSA_DH)),
                  _const_spec((2, 1, CMP_HIDDEN)),
                  _const_spec((2, CMP_HIDDEN, NSA_DH)),
                  _const_spec((2, 1, NSA_DH))],
        out_specs=pl.BlockSpec((1, 2 * NSA_KV_GROUPS, n_chunk, NSA_DH), lambda b, pt: (b, 0, 0, 0)),
        scratch_shapes=[pltpu.VMEM((2, NSA_KV_GROUPS, 2, NSA_DH, n_pages * PAGE_SIZE), F32),
                        pltpu.VMEM((n_pages * PAGE_SIZE, 2 * NSA_DH), F32),
                        pltpu.VMEM((n_chunk + 8, CMP_HIDDEN), F32),
                        pltpu.SemaphoreType.DMA((2,))])
    return pl.pallas_call(
        kern, out_shape=jax.ShapeDtypeStruct((batch, 2 * NSA_KV_GROUPS, n_chunk, NSA_DH), F32),
        grid_spec=grid_spec, compiler_params=_cparams("arbitrary"), name="compress",
    )(page_table, kv_t, wp, w1b, pe_flat, b1, w2b, b2)


def _feature_major(kv):
    return kv.transpose(0, 2, 3, 4, 1)


def _token_major(kv_t, batch, seq):
    return kv_t.reshape(batch, NSA_KV_GROUPS, 2, NSA_DH, seq).transpose(0, 4, 1, 2, 3)


def _permute_cmp(kvc):
    b, gc, n, d = kvc.shape
    return kvc.reshape(b, gc, n // 4, 4, d).transpose(0, 1, 3, 2, 4).reshape(b, gc, n, d)


def _group_queries(qall):
    return jnp.concatenate([qall[:, r * NSA_DH:(r + 1) * NSA_DH] for r in range(NSA_GROUP_HEADS)], axis=0)


def _block_importance(psum, n_slc):
    p0, p1, p2, p3 = (psum[:, m * n_slc:(m + 1) * n_slc] for m in range(4))
    lane = lax.broadcasted_iota(jnp.int32, p3.shape, 1)
    p3s = jnp.where(lane == 0, 0.0, pltpu.roll(p3, 1, axis=1))
    return ((((((p0 + p3s) + p1) + p0) + p2) + p1) + p3) + p2


def _top_blocks_t(score_t, count):
    blk = lax.broadcasted_iota(jnp.int32, score_t.shape, 0).astype(F32)
    n = float(score_t.shape[0])
    finite = score_t > -jnp.inf
    picked = jnp.zeros(score_t.shape, F32)
    for _ in range(count):
        mx = jnp.max(score_t, axis=0, keepdims=True)
        idx = jnp.min(jnp.where(score_t == mx, blk, n), axis=0, keepdims=True)
        pick = blk == idx
        picked = jnp.where(pick, 1.0, picked)
        score_t = jnp.where(pick, -jnp.inf, score_t)
    return jnp.where(finite, picked, 0.0)


_N_FORCED = 3


def _cmp_topk_kernel(q_ref, kc_ref, vc_ref, small_ref, oc_ref, sel_ref, imp_scr, *, tq, sub, n_slc):
    g = pl.program_id(1)
    q0 = pl.program_id(2) * tq
    rows = NSA_GROUP_HEADS * sub
    n_cmp = 4 * n_slc
    kc = kc_ref[0, 0].astype(BF16)
    vc = vc_ref[0, 0].astype(BF16)
    lane = lax.broadcasted_iota(jnp.int32, (1, n_cmp), 1)
    blk_end = (4 * (lane % n_slc) + lane // n_slc) * CMP_STRIDE + (CMP_LEN - 1)
    for j in range(tq // sub):
        js = slice(j * sub, (j + 1) * sub)
        q = _group_queries(q_ref[js, :])
        s = _dot_nt(q, kc)
        t_rows = (q0 + j * sub) + lax.broadcasted_iota(jnp.int32, (rows, 1), 0) % sub
        p = _masked_softmax(s, blk_end <= t_rows)
        oc = _dot(p.astype(BF16), vc)
        psum = ((p[0:sub] + p[sub:2 * sub]) + p[2 * sub:3 * sub]) + p[3 * sub:4 * sub]
        imp_scr[js, :] = _block_importance(psum, n_slc)
        sig = _sigmoid(small_ref[js, :])
        outs = []
        for r in range(NSA_GROUP_HEADS):
            gate = _lane_column(sig, _NG_LANE0 + (g * NSA_GROUP_HEADS + r) * 3)
            outs.append(gate * oc[r * sub:(r + 1) * sub])
        oc_ref[js, :] = jnp.concatenate(outs, axis=1)

    cur = (q0 + lax.broadcasted_iota(jnp.int32, (1, tq), 1)) // SLC_BLOCK
    blk = lax.broadcasted_iota(jnp.int32, (n_slc, 1), 0)
    forced = (blk == 0) | (blk == cur) | (blk == cur - 1)
    valid = blk <= cur
    score_t = jnp.where(valid & jnp.logical_not(forced), imp_scr[...].T, -jnp.inf)
    sel_t = jnp.where(valid & forced, 1.0, _top_blocks_t(score_t, min(SLC_TOPK, n_slc) - _N_FORCED))
    sel_ref[0] = (sel_t.T - 1.0).astype(BF16)


def _cmp_topk_prompt(nq, kvc_perm, small, batch, seq, tq=512, sub=128):
    G = NSA_KV_GROUPS
    n_slc = seq // SLC_BLOCK
    assert kvc_perm.shape[2] == 4 * n_slc and seq % tq == 0 and tq % sub == 0 and n_slc >= _N_FORCED
    nqt = seq // tq
    gw = NSA_GROUP_HEADS * NSA_DH
    kern = functools.partial(_cmp_topk_kernel, tq=tq, sub=sub, n_slc=n_slc)
    return pl.pallas_call(
        kern,
        out_shape=(jax.ShapeDtypeStruct((batch * seq, NSA_HEADS * NSA_DH), F32),
                   jax.ShapeDtypeStruct((batch * G, seq, n_slc), BF16)),
        grid=(batch, G, nqt),
        in_specs=[pl.BlockSpec((tq, gw), lambda b, g, i: (b * nqt + i, g)),
                  pl.BlockSpec((1, 1, 4 * n_slc, NSA_DH), lambda b, g, i: (b, 2 * g, 0, 0)),
                  pl.BlockSpec((1, 1, 4 * n_slc, NSA_DH), lambda b, g, i: (b, 2 * g + 1, 0, 0)),
                  pl.BlockSpec((tq, _SMALL_W), lambda b, g, i: (b * nqt + i, 0))],
        out_specs=(pl.BlockSpec((tq, gw), lambda b, g, i: (b * nqt + i, g)),
                   pl.BlockSpec((1, tq, n_slc), lambda b, g, i: (b * G + g, i, 0))),
        scratch_shapes=[pltpu.VMEM((tq, n_slc), F32)],
        compiler_params=_cparams("parallel", "parallel", "parallel"), name="cmp_topk",
    )(nq, kvc_perm, kvc_perm, small)


def _slc_win_kernel(q_ref, sel_ref, ks_ref, kw_ref, small_ref, oc_ref, on_ref, *, tq, tk, n_slc):
    g = pl.program_id(1)
    q0 = pl.program_id(2) * tq
    rows = NSA_GROUP_HEADS * tq
    DH = NSA_DH
    q = _group_queries(q_ref[...])
    qpad = jnp.concatenate([q, jnp.zeros((rows, DH), BF16)], axis=1)
    sel4 = jnp.concatenate([sel_ref[0]] * NSA_GROUP_HEADS, axis=0)
    qaug = jnp.concatenate([qpad, sel4], axis=1)
    t_rows = q0 + lax.broadcasted_iota(jnp.int32, (rows, 1), 0) % tq

    is_k = lax.broadcasted_iota(jnp.int32, (2 * DH, tk), 0) < DH
    blk_r = lax.broadcasted_iota(jnp.int32, (n_slc, tk), 0)
    tok_c = lax.broadcasted_iota(jnp.int32, (n_slc, tk), 1)
    tok_l = lax.broadcasted_iota(jnp.int32, (1, tk), 1)

    def tile(kt, m, acc, causal):
        k0 = pl.multiple_of(kt * tk, tk)
        kv = ks_ref[0, :, pl.ds(k0, tk)]
        onehot = jnp.where(blk_r == (k0 + tok_c) // SLC_BLOCK, _MASK_BIG, 0.0).astype(BF16)
        kaug = jnp.concatenate([jnp.where(is_k, kv, jnp.zeros_like(kv)), onehot], axis=0)
        s = _dot(qaug, kaug)
        if causal:
            s = jnp.where(k0 + tok_l <= t_rows, s, -_MASK_BIG)
        m_new = jnp.maximum(m, jnp.max(s, axis=-1, keepdims=True))
        p = jnp.exp2(s - m_new)
        vaug = jnp.where(is_k, jnp.ones_like(kv), kv)
        acc = jnp.exp2(m - m_new) * acc + _dot_nt(p.astype(BF16), vaug)
        return m_new, acc

    def tile_pair(i, carry):
        return tile(2 * i + 1, *tile(2 * i, *carry, False), False)

    last = (q0 + tq - 1) // tk
    carry = (jnp.full((rows, 1), -1e30, F32), jnp.zeros((rows, 2 * DH), F32))
    carry = lax.fori_loop(0, last // 2, tile_pair, carry)
    carry = lax.fori_loop(2 * (last // 2), last, lambda kt, c: tile(kt, *c, False), carry)
    m, acc = tile(last, *carry, True)
    o_s = acc[:, DH:] / acc[:, 0:DH]

    wlen = WINDOW + tq
    w0 = pl.multiple_of(jnp.maximum(q0 - WINDOW, 0), tq)
    kvw = kw_ref[0, :, pl.ds(w0, wlen)]
    row_w = lax.broadcasted_iota(jnp.int32, (2 * DH, wlen), 0)
    s_w = _dot(qpad, jnp.where(row_w < DH, kvw, jnp.zeros_like(kvw)))
    age = t_rows - (w0 + lax.broadcasted_iota(jnp.int32, (1, wlen), 1))
    s_w = jnp.where(age >= 0, jnp.where(age < WINDOW, s_w, -_MASK_BIG), -_MASK_BIG)
    e_w = jnp.exp2(s_w - jnp.max(s_w, axis=-1, keepdims=True))
    acc_w = _dot_nt(e_w.astype(BF16), jnp.where(row_w < DH, jnp.ones_like(kvw), kvw))
    o_w = acc_w[:, DH:] / acc_w[:, 0:DH]

    sig = _sigmoid(small_ref[...])
    oc = oc_ref[...]
    outs = []
    for r in range(NSA_GROUP_HEADS):
        lane0 = _NG_LANE0 + (g * NSA_GROUP_HEADS + r) * 3
        g_s = _lane_column(sig, lane0 + 1)
        g_w = _lane_column(sig, lane0 + 2)
        rs = slice(r * tq, (r + 1) * tq)
        outs.append(oc[:, r * DH:(r + 1) * DH] + g_s * o_s[rs] + g_w * o_w[rs])
    on_ref[...] = jnp.concatenate(outs, axis=1).astype(BF16)


def _slc_win_prompt(nq, selm, nkb, small, oc, batch, seq, tq=256, tk=512):
    G = NSA_KV_GROUPS
    n_slc = seq // SLC_BLOCK
    assert seq % tk == 0 and seq % tq == 0 and tk % tq == 0 and seq >= WINDOW + tq
    nqt = seq // tq
    gw = NSA_GROUP_HEADS * NSA_DH
    kvw = 2 * NSA_DH
    kern = functools.partial(_slc_win_kernel, tq=tq, tk=tk, n_slc=n_slc)
    return pl.pallas_call(
        kern, out_shape=jax.ShapeDtypeStruct((batch * seq, NSA_HEADS * NSA_DH), BF16),
        grid=(batch, G, nqt),
        in_specs=[pl.BlockSpec((tq, gw), lambda b, g, i: (b * nqt + i, g)),
                  pl.BlockSpec((1, tq, n_slc), lambda b, g, i: (b * G + g, i, 0)),
                  pl.BlockSpec((1, kvw, seq), lambda b, g, i: (b, G + g, 0)),
                  pl.BlockSpec((1, kvw, seq), lambda b, g, i: (b, 2 * G + g, 0)),
                  pl.BlockSpec((tq, _SMALL_W), lambda b, g, i: (b * nqt + i, 0)),
                  pl.BlockSpec((tq, gw), lambda b, g, i: (b * nqt + i, g))],
        out_specs=pl.BlockSpec((tq, gw), lambda b, g, i: (b * nqt + i, g)),
        compiler_params=_cparams("parallel", "parallel", "parallel"), name="slc_win",
    )(nq, selm, nkb, nkb, small, oc)


def _mix_out_kernel(og_ref, on_ref, mg_ref, x_ref, wgo_ref, wno_ref, wo_ref, gpost_ref, y_ref):
    yg = _dot(og_ref[...], wgo_ref[...])
    yn = _dot(on_ref[...], wno_ref[...])
    mg = mg_ref[...]
    mix = _sigmoid(mg[:, 0:D_MODEL]) * yg + _sigmoid(mg[:, D_MODEL:]) * yn
    z = _dot(mix.astype(BF16), wo_ref[...])
    y_ref[...] = x_ref[...] + _rms(z, gpost_ref[...])


def _mix_out(og, on, mg, x, wgo, wno, wo, g_post, tm):
    m = x.shape[0]
    row = lambda w: pl.BlockSpec((tm, w), lambda i: (i, 0))
    wspec = _const_spec((D_MODEL, D_MODEL))
    return pl.pallas_call(
        _mix_out_kernel, out_shape=jax.ShapeDtypeStruct((m, D_MODEL), F32), grid=(m // tm,),
        in_specs=[row(D_MODEL), row(D_MODEL), row(2 * D_MODEL), row(D_MODEL), wspec, wspec, wspec,
                  _const_spec((1, D_MODEL))],
        out_specs=row(D_MODEL), compiler_params=_cparams("parallel"), name="mix_out",
    )(og, on, mg, x, wgo, wno, wo, g_post.reshape(1, D_MODEL))


_FFN_HALO = 16


def _ffn_tail(x, conv, bgate, p, wdn_ref, gpost_ref, gple_ref, wpg_ref, wpp_ref):
    y = _dot((_gelu_tanh(conv) * bgate).astype(BF16), wdn_ref[...])
    x2 = x + _rms(y, gpost_ref[...])
    gate = _sigmoid(_dot(_rms(x2, gple_ref[...]).astype(BF16), wpg_ref[...]))
    return x2 + gate * _dot(p.astype(BF16), wpp_ref[...])


def _ffn_prompt_kernel(x_ref, xh_ref, p_ref, gpre_ref, wup_ref, cw_ref, cb_ref, wdn_ref, gpost_ref, gple_ref,
                       wpg_ref, wpp_ref, y_ref, alast_ref, h_scr, a_scr, *, tm, tiles_per_seq):
    H = _FFN_HALO
    x = x_ref[...]
    h_scr[0:H, :] = _rms(xh_ref[...], gpre_ref[...]).astype(BF16)
    h_scr[H:H + tm, :] = _rms(x, gpre_ref[...]).astype(BF16)
    ab = _dot(h_scr[...], wup_ref[...])
    keep = jnp.where(pl.program_id(0) % tiles_per_seq == 0, 0.0, 1.0)
    a_scr[0:H, :] = ab[0:H, 0:D_FF] * keep
    a_scr[H:H + tm, :] = ab[H:, 0:D_FF]
    cw = cw_ref[...]
    conv = (a_scr[H - 2:H - 2 + tm, :] * cw[0:1] + a_scr[H - 1:H - 1 + tm, :] * cw[1:2]
            + a_scr[H:H + tm, :] * cw[2:3]) + cb_ref[...]
    y_ref[...] = _ffn_tail(x, conv, ab[H:, D_FF:], p_ref[...], wdn_ref, gpost_ref, gple_ref, wpg_ref, wpp_ref)
    alast_ref[0] = a_scr[H + tm - 8:H + tm, :]


def _ffn_weight_specs():
    return [_const_spec((1, D_MODEL)), _const_spec((D_MODEL, 2 * D_FF)), _const_spec((CONV_W, D_FF)),
            _const_spec((1, D_FF)), _const_spec((D_FF, D_MODEL)), _const_spec((1, D_MODEL)),
            _const_spec((1, D_MODEL)), _const_spec((D_MODEL, D_MODEL)), _const_spec((PLE_DIM, D_MODEL))]


def _ffn_prompt(x, p, fw, batch, seq, tm=256):
    assert seq % tm == 0 and tm % _FFN_HALO == 0
    tps = seq // tm
    hb = tm // _FFN_HALO
    kern = functools.partial(_ffn_prompt_kernel, tm=tm, tiles_per_seq=tps)
    return pl.pallas_call(
        kern,
        out_shape=(jax.ShapeDtypeStruct((batch * seq, D_MODEL), F32), jax.ShapeDtypeStruct((batch, 8, D_FF), F32)),
        grid=(batch * tps,),
        in_specs=[pl.BlockSpec((tm, D_MODEL), lambda i: (i, 0)),
                  pl.BlockSpec((_FFN_HALO, D_MODEL), lambda i: (jnp.maximum(i * hb - 1, 0), 0)),
                  pl.BlockSpec((tm, PLE_DIM), lambda i: (i, 0))] + _ffn_weight_specs(),
        out_specs=(pl.BlockSpec((tm, D_MODEL), lambda i: (i, 0)),
                   pl.BlockSpec((1, 8, D_FF), lambda i: (i // tps, 0, 0))),
        scratch_shapes=[pltpu.VMEM((_FFN_HALO + tm, D_MODEL), BF16), pltpu.VMEM((_FFN_HALO + tm, D_FF), F32)],
        compiler_params=_cparams("arbitrary"), name="ffn_prompt",
    )(x, x, p, *fw)


def _layer_prompt(x, p, w, tm=256):
    batch, seq, _ = x.shape
    m = batch * seq
    x2 = x.reshape(m, D_MODEL)
    gla, small, nq, kc_t, ks_t, kw_t, nkb_t, mg = _proj_in(x2, w["g_mix_pre"], w["w_in_pack"], tm, w["w_kv_t"], seq)
    og, gla_state = _gla_prompt(gla, small, w["w_gate_pack"], w["b_gla_gate"], w["g_gla_norm"], batch, seq)

    n_pages = seq // PAGE_SIZE
    ident_pt = jnp.arange(batch * n_pages, dtype=jnp.int32).reshape(batch, n_pages)
    kvc = _compress(ident_pt, kc_t.reshape(batch, NSA_KV_GROUPS, 2, NSA_DH, seq), *w["cmp"])
    oc, selm = _cmp_topk_prompt(nq, _permute_cmp(kvc), small, batch, seq)
    on = _slc_win_prompt(nq, selm, nkb_t, small, oc, batch, seq)

    x1 = _mix_out(og, on, mg, x2, w["w_gla_out"], w["w_nsa_out"], w["w_out"], w["g_mix_post"], tm)
    y, alast = _ffn_prompt(x1, p.reshape(m, PLE_DIM), w["ffn"], batch, seq, tm)

    win = min(WINDOW, seq)
    return (y.reshape(batch, seq, D_MODEL), _token_major(kc_t, batch, seq), _token_major(ks_t, batch, seq),
            _token_major(kw_t[:, :, seq - win:], batch, win), gla_state, alast[:, 8 - (CONV_W - 1):])


def _row_to_col(row):
    n = row.shape[-1]
    ri = lax.broadcasted_iota(jnp.int32, (n, n), 0)
    ci = lax.broadcasted_iota(jnp.int32, (n, n), 1)
    return jnp.sum(jnp.where(ri == ci, jnp.broadcast_to(row, (n, n)), 0.0), axis=-1, keepdims=True)


def _gla_step_kernel(gla_ref, small_ref, s0_ref, wg_ref, bg_ref, gn_ref, og_ref, s1_ref):
    row = gla_ref[0]
    small = small_ref[0]
    H, DK, DV = GLA_HEADS, GLA_DK, GLA_DV
    outs = []
    for h in range(H):
        q = row[:, h * DK:(h + 1) * DK] * (DK ** -0.5)
        k = row[:, H * DK + h * DK:H * DK + (h + 1) * DK]
        v = row[:, 2 * H * DK + h * DV:2 * H * DK + (h + 1) * DV]
        r = row[:, 2 * H * DK + H * DV + h * DV:2 * H * DK + H * DV + (h + 1) * DV]
        alpha = jnp.exp(_log_decay(small, wg_ref[:, h * DK:(h + 1) * DK], bg_ref[:, h * DK:(h + 1) * DK]))
        s0 = s0_ref[0, h]
        s1_ref[0, h] = _row_to_col(alpha) * s0 + _row_to_col(k) * v
        o = _dot((q * alpha).astype(BF16), s0.astype(BF16)) + jnp.sum(q * k, axis=-1, keepdims=True) * v
        outs.append(_gla_out(o, gn_ref[:, h * DV:(h + 1) * DV], r))
    og_ref[0] = jnp.concatenate(outs, axis=1)


def _gla_sample(gla, small, state, w_gate_pack, b_gate, g_norm):
    nb = gla.shape[0]
    H = GLA_HEADS
    og, s1 = pl.pallas_call(
        _gla_step_kernel,
        out_shape=(jax.ShapeDtypeStruct((nb, 1, H * GLA_DV), BF16), jax.ShapeDtypeStruct(state.shape, F32)),
        grid=(nb,),
        in_specs=[pl.BlockSpec((1, 1, _GLA_W), lambda b: (b, 0, 0)),
                  pl.BlockSpec((1, 1, _SMALL_W), lambda b: (b, 0, 0)),
                  pl.BlockSpec((1, H, GLA_DK, GLA_DV), lambda b: (b, 0, 0, 0)),
                  _const_spec((_SMALL_W, H * GLA_DK)), _const_spec((1, H * GLA_DK)), _const_spec((1, H * GLA_DV))],
        out_specs=(pl.BlockSpec((1, 1, H * GLA_DV), lambda b: (b, 0, 0)),
                   pl.BlockSpec((1, H, GLA_DK, GLA_DV), lambda b: (b, 0, 0, 0))),
        compiler_params=_cparams("parallel"), name="gla_step",
    )(gla.reshape(nb, 1, _GLA_W), small.reshape(nb, 1, _SMALL_W), state, w_gate_pack,
      b_gate.reshape(1, -1), g_norm.reshape(1, -1))
    return og.reshape(nb, H * GLA_DV), s1


def _head_group_rows(x_groups):
    grp = lax.broadcasted_iota(jnp.int32, x_groups[0].shape, 0) // NSA_GROUP_HEADS
    out = jnp.zeros(x_groups[0].shape, F32)
    for g, xg in enumerate(x_groups):
        out = jnp.where(grp == g, xg, out)
    return out


def _cmp_topk_sample_kernel(q_ref, kvc_ref, oc_ref, idx_ref, *, t, n_lane, n_pick):
    q = q_ref[0]
    n_cmp = kvc_ref.shape[2]
    lane = lax.broadcasted_iota(jnp.int32, (1, n_cmp), 1)
    blk_end = (4 * (lane % n_lane) + lane // n_lane) * CMP_STRIDE + (CMP_LEN - 1)
    ocs, imps = [], []
    for g in range(NSA_KV_GROUPS):
        s = _dot_nt(q, kvc_ref[0, 2 * g].astype(BF16))
        p = _masked_softmax(s, jnp.broadcast_to(blk_end <= t, s.shape))
        ocs.append(_dot(p.astype(BF16), kvc_ref[0, 2 * g + 1].astype(BF16)))
        r0 = NSA_GROUP_HEADS * g
        psum = ((p[r0:r0 + 1] + p[r0 + 1:r0 + 2]) + p[r0 + 2:r0 + 3]) + p[r0 + 3:r0 + 4]
        imps.append(_block_importance(psum, n_lane))
    oc_ref[0] = _head_group_rows(ocs)

    imp = jnp.concatenate(imps + [jnp.zeros((8 - NSA_KV_GROUPS, n_lane), F32)], axis=0)
    blk = lax.broadcasted_iota(jnp.int32, imp.shape, 1)
    cur = t // SLC_BLOCK
    forced = (blk == 0) | (blk == cur) | (blk == cur - 1)
    score = jnp.where(blk <= cur, jnp.where(forced, FORCE_SCORE, imp), -jnp.inf)
    idx_out = jnp.zeros(imp.shape, jnp.int32)
    for i in range(n_pick):
        mx = jnp.max(score, axis=-1, keepdims=True)
        idx = jnp.min(jnp.where(score == mx, blk, n_lane), axis=-1, keepdims=True)
        idx_out = jnp.where(blk == i, idx, idx_out)
        score = jnp.where(blk == idx, -jnp.inf, score)
    idx_ref[0] = idx_out


def _attend_with_new_token(qpad, qf, kv, valid, new_row):
    DH = NSA_DH
    is_k = lax.broadcasted_iota(jnp.int32, kv.shape, 0) < DH
    s = jnp.where(valid, _dot(qpad, jnp.where(is_k, kv, jnp.zeros_like(kv))), -jnp.inf)
    s_n = jnp.sum(qf * new_row[:, 0:DH].astype(BF16).astype(F32), axis=-1, keepdims=True)
    m = jnp.maximum(jnp.max(s, axis=-1, keepdims=True), s_n)
    e = jnp.exp2(s - m)
    en = jnp.exp2(s_n - m)
    num = _dot_nt(e.astype(BF16), kv)[:, DH:] + en * new_row[:, DH:]
    return num / (jnp.sum(e, axis=-1, keepdims=True) + en)


def _slc_win_sample_kernel(idx_ref, pt_ref, *refs, t, win, n_pick):
    G = NSA_KV_GROUPS
    GW = 2 * NSA_DH
    kv_refs = refs[:G * n_pick]
    q_ref, ksn_ref, kwn_ref, cw_ref, small_ref, oc_ref, on_ref = refs[G * n_pick:]
    b = pl.program_id(0)
    q = q_ref[0]
    qf = q.astype(F32)
    qpad = jnp.concatenate([q, jnp.zeros_like(q)], axis=1)
    ksn = ksn_ref[0]
    kwn = kwn_ref[0]
    blocks_per_page = PAGE_SIZE // SLC_BLOCK
    lane_blk = (lax.broadcasted_iota(jnp.int32, (1, n_pick * PAGE_SIZE), 1) // SLC_BLOCK) % blocks_per_page
    pos = (t - win) + lax.broadcasted_iota(jnp.int32, (1, win), 1)
    w_valid = (pos <= t) & (pos > t - WINDOW) & (pos >= 0)
    o_s, o_w = [], []
    for g in range(G):
        ks = range(g * n_pick, (g + 1) * n_pick)
        kv = jnp.concatenate([kv_refs[i][0, 0].reshape(GW, PAGE_SIZE) for i in ks], axis=1).astype(BF16)
        want = jnp.concatenate([jnp.full((1, PAGE_SIZE), idx_ref[b, i] % blocks_per_page, jnp.int32) for i in ks], axis=1)
        o_s.append(_attend_with_new_token(qpad, qf, kv, lane_blk == want, ksn[:, g * GW:(g + 1) * GW]))
        kvw = cw_ref[0, g].reshape(GW, win).astype(BF16)
        o_w.append(_attend_with_new_token(qpad, qf, kvw, w_valid, kwn[:, g * GW:(g + 1) * GW]))
    o_s = _head_group_rows(o_s)
    o_w = _head_group_rows(o_w)
    sig = jnp.broadcast_to(_sigmoid(small_ref[0]), (NSA_HEADS, _SMALL_W))
    lane = lax.broadcasted_iota(jnp.int32, sig.shape, 1)
    head = lax.broadcasted_iota(jnp.int32, sig.shape, 0)
    gate = [jnp.sum(jnp.where(lane == _NG_LANE0 + 3 * head + x, sig, 0.0), axis=-1, keepdims=True) for x in range(3)]
    on_ref[0] = (gate[0] * oc_ref[0] + gate[1] * o_s + gate[2] * o_w).astype(BF16)


def _nsa_sample(nq, ks_new, kw_new, small, kvc_perm, cache_slc, cache_win, page_table):
    nb, n_pages = page_table.shape
    past = n_pages * PAGE_SIZE
    t = past
    n_lane = past // SLC_BLOCK
    n_pick = SLC_TOPK - 1
    assert kvc_perm.shape[2] == 4 * n_lane and n_lane >= SLC_TOPK and n_lane % 128 == 0
    win = cache_win.shape[1]
    q3 = nq.reshape(nb, NSA_HEADS, NSA_DH)
    G = NSA_KV_GROUPS

    oc, idx = pl.pallas_call(
        functools.partial(_cmp_topk_sample_kernel, t=t, n_lane=n_lane, n_pick=n_pick),
        out_shape=(jax.ShapeDtypeStruct((nb, NSA_HEADS, NSA_DH), F32), jax.ShapeDtypeStruct((nb, 8, n_lane), jnp.int32)),
        grid=(nb,),
        in_specs=[pl.BlockSpec((1, NSA_HEADS, NSA_DH), lambda b: (b, 0, 0)),
                  pl.BlockSpec((1, 2 * G, 4 * n_lane, NSA_DH), lambda b: (b, 0, 0, 0))],
        out_specs=(pl.BlockSpec((1, NSA_HEADS, NSA_DH), lambda b: (b, 0, 0)),
                   pl.BlockSpec((1, 8, n_lane), lambda b: (b, 0, 0))),
        compiler_params=_cparams("parallel"), name="cmp_topk_sample",
    )(q3, kvc_perm)
    blk_idx = idx[:, :G, :n_pick].reshape(nb, G * n_pick)

    half = PAGE_SIZE // SLC_BLOCK
    slc_t = _feature_major(cache_slc)
    win_t = _feature_major(cache_win)

    def kv_spec(i):
        def index(b, idx_ref, pt_ref):
            return (pt_ref[b, idx_ref[b, i] // half], i // n_pick, 0, 0, 0)
        return pl.BlockSpec((1, 1, 2, NSA_DH, PAGE_SIZE), index)

    per_seq = lambda shape: pl.BlockSpec((1,) + shape, lambda b, *_: (b,) + (0,) * len(shape))
    grid_spec = pltpu.PrefetchScalarGridSpec(
        num_scalar_prefetch=2, grid=(nb,),
        in_specs=[kv_spec(i) for i in range(G * n_pick)] + [
            per_seq((NSA_HEADS, NSA_DH)), per_seq((1, NSA_KV_W)), per_seq((1, NSA_KV_W)),
            per_seq((G, 2, NSA_DH, win)), per_seq((1, _SMALL_W)), per_seq((NSA_HEADS, NSA_DH))],
        out_specs=per_seq((NSA_HEADS, NSA_DH)))
    on = pl.pallas_call(
        functools.partial(_slc_win_sample_kernel, t=t, win=win, n_pick=n_pick),
        out_shape=jax.ShapeDtypeStruct((nb, NSA_HEADS, NSA_DH), BF16), grid_spec=grid_spec,
        compiler_params=_cparams("parallel"), name="slc_win_sample",
    )(blk_idx, page_table, *([slc_t] * (G * n_pick)), q3,
      ks_new.reshape(nb, 1, NSA_KV_W), kw_new.reshape(nb, 1, NSA_KV_W),
      win_t, small.reshape(nb, 1, _SMALL_W), oc)
    return on.reshape(nb, NSA_HEADS * NSA_DH)


def _ffn_sample_kernel(x_ref, s0_ref, s1_ref, p_ref, gpre_ref, wup_ref, cw_ref, cb_ref, wdn_ref, gpost_ref,
                       gple_ref, wpg_ref, wpp_ref, y_ref, a_ref):
    x = x_ref[...]
    ab = _dot(_rms(x, gpre_ref[...]).astype(BF16), wup_ref[...])
    a = ab[:, 0:D_FF]
    cw = cw_ref[...]
    conv = (s0_ref[...] * cw[0:1] + s1_ref[...] * cw[1:2] + a * cw[2:3]) + cb_ref[...]
    y_ref[...] = _ffn_tail(x, conv, ab[:, D_FF:], p_ref[...], wdn_ref, gpost_ref, gple_ref, wpg_ref, wpp_ref)
    a_ref[...] = a


def _ffn_sample(x, conv_state, p, fw):
    nb = x.shape[0]
    full = lambda w: pl.BlockSpec((nb, w), lambda i: (0, 0))
    return pl.pallas_call(
        _ffn_sample_kernel,
        out_shape=(jax.ShapeDtypeStruct((nb, D_MODEL), F32), jax.ShapeDtypeStruct((nb, D_FF), F32)),
        grid=(1,),
        in_specs=[full(D_MODEL), full(D_FF), full(D_FF), full(PLE_DIM)] + _ffn_weight_specs(),
        out_specs=(full(D_MODEL), full(D_FF)),
        compiler_params=_cparams("arbitrary"), name="ffn_sample",
    )(x, conv_state[:, 0], conv_state[:, 1], p, *fw)


def _layer_sample(x, p, cache_cmp, cache_slc, cache_win, state_gla, state_conv, page_table, w):
    nb, s, _ = x.shape
    assert s == 1
    x2 = x.reshape(nb, D_MODEL)
    gla, small, nq, kc, ks, kw, mg = _proj_in(x2, w["g_mix_pre"], w["w_in_pack"], nb)
    og, gla_state = _gla_sample(gla, small, state_gla, w["w_gate_pack"], w["b_gla_gate"], w["g_gla_norm"])

    kvc = _compress(page_table, _feature_major(cache_cmp), *w["cmp"])
    on = _nsa_sample(nq, ks, kw, small, _permute_cmp(kvc), cache_slc, cache_win, page_table)

    x1 = _mix_out(og, on, mg, x2, w["w_gla_out"], w["w_nsa_out"], w["w_out"], w["g_mix_post"], nb)
    y, a = _ffn_sample(x1, state_conv, p.reshape(nb, PLE_DIM), w["ffn"])
    kv_shape = (nb, 1, NSA_KV_GROUPS, 2, NSA_DH)
    conv_new = jnp.stack([state_conv[:, 1], a], axis=1)
    return (y.reshape(nb, 1, D_MODEL), kc.reshape(kv_shape), ks.reshape(kv_shape), kw.reshape(kv_shape),
            gla_state, conv_new)


def _prep_weights(g_mix_pre, g_mix_post, g_ffn_pre, g_ffn_post, g_ple, w_in, w_gla_gate_up, b_gla_gate,
                  g_gla_norm, w_gla_out, w_cmp1, b_cmp1, w_cmp2, b_cmp2, pe_cmp, w_nsa_out, w_out, w_ffn_up,
                  conv_ffn_w, b_conv_ffn, w_ffn_down, w_ple_proj, w_ple_gate):
    pe_flat = pe_cmp.transpose(1, 0, 2).reshape(2, 1, CMP_LEN * NSA_DH)
    row = lambda v: v.reshape(1, -1)
    return dict(
        g_mix_pre=g_mix_pre, g_mix_post=g_mix_post, w_in_pack=_pack_w_in(w_in), w_kv_t=_pack_w_kv_t(w_in),
        w_gate_pack=_pack_w_gate(w_gla_gate_up), b_gla_gate=b_gla_gate, g_gla_norm=g_gla_norm,
        w_gla_out=w_gla_out.astype(BF16), w_nsa_out=w_nsa_out.astype(BF16), w_out=w_out.astype(BF16),
        cmp=(_pack_w_cmp1(w_cmp1), w_cmp1.astype(BF16), pe_flat, b_cmp1.reshape(2, 1, CMP_HIDDEN),
             w_cmp2.astype(BF16), b_cmp2.reshape(2, 1, NSA_DH)),
        ffn=(row(g_ffn_pre), w_ffn_up.astype(BF16), conv_ffn_w, row(b_conv_ffn), w_ffn_down.astype(BF16),
             row(g_ffn_post), row(g_ple), w_ple_gate.astype(BF16), w_ple_proj.astype(BF16)),
    )


def kernel(x_prompt, x_sample, cache_cmp_kv, cache_slc_kv, cache_win_kv, state_gla, state_ffn_conv, page_table,
           p_prompt, p_sample, g_mix_pre, g_mix_post, g_ffn_pre, g_ffn_post, g_ple, w_in, w_gla_gate_up,
           b_gla_gate, g_gla_norm, w_gla_out, w_cmp1, b_cmp1, w_cmp2, b_cmp2, pe_cmp, w_nsa_out, w_out, w_ffn_up,
           conv_ffn_w, b_conv_ffn, w_ffn_down, w_ple_proj, w_ple_gate):
    layer_weights = (g_mix_pre, g_mix_post, g_ffn_pre, g_ffn_post, g_ple, w_in, w_gla_gate_up, b_gla_gate,
                     g_gla_norm, w_gla_out, w_cmp1, b_cmp1, w_cmp2, b_cmp2, pe_cmp, w_nsa_out, w_out, w_ffn_up,
                     conv_ffn_w, b_conv_ffn, w_ffn_down, w_ple_proj, w_ple_gate)
    depth = w_in.shape[0]
    yp, ys = x_prompt, x_sample
    extras_p, extras_s = [], []
    for i in range(depth):
        w = _prep_weights(*[v[i] for v in layer_weights])
        yp, *ep = _layer_prompt(yp, p_prompt[i], w)
        ys, *es = _layer_sample(ys, p_sample[i], cache_cmp_kv[i], cache_slc_kv[i], cache_win_kv[i], state_gla[i],
                                state_ffn_conv[i], page_table, w)
        extras_p.append(ep)
        extras_s.append(es)
    stack = lambda rows, j: jnp.stack([r[j] for r in rows])
    outs = [yp, ys]
    for j in range(5):
        outs += [stack(extras_p, j), stack(extras_s, j)]
    return tuple(outs)
```

```python
import functools

import numpy as np
import jax
import jax.numpy as jnp
from jax import lax
from jax.experimental import pallas as pl
from jax.experimental.pallas import tpu as pltpu

F32 = jnp.float32
BF16 = jnp.bfloat16

D_MODEL = 1024
PAGE_SIZE = 128
GLA_HEADS = 4
GLA_DK = 128
GLA_DV = 256
GLA_GATE_RANK = 16
GLA_TAU = 16.0
GLA_SUB = 16
NSA_HEADS = 16
NSA_KV_GROUPS = 4
NSA_GROUP_HEADS = 4
NSA_DH = 64
NSA_KV_W = 2 * NSA_KV_GROUPS * NSA_DH
CMP_LEN = 32
CMP_STRIDE = 16
CMP_HIDDEN = 128
SLC_BLOCK = 64
SLC_TOPK = 16
WINDOW = 512
FORCE_SCORE = 1.0e4
D_FF = 2816
CONV_W = 3
PLE_DIM = 256
EPS = 1e-6

V7X_VMEM_BYTES = 64 * 1024 * 1024
VMEM_LIMIT = V7X_VMEM_BYTES - 8 * 1024 * 1024

_GLA_OFF, _GLA_W = 0, 2 * GLA_HEADS * GLA_DK + 2 * GLA_HEADS * GLA_DV
_SMALL_OFF, _SMALL_W = _GLA_OFF + _GLA_W, 128
_NQ_OFF, _NQ_W = _SMALL_OFF + _SMALL_W, NSA_HEADS * NSA_DH
_NK_OFF, _NK_W = _NQ_OFF + _NQ_W, 3 * NSA_KV_W
_MG_OFF, _MG_W = _NK_OFF + _NK_W, 2 * D_MODEL
_PACK_W = _MG_OFF + _MG_W
_NG_LANE0 = GLA_GATE_RANK

_Q_SCALE = float(NSA_DH ** -0.5 * np.log2(np.e))

_MASK_BIG = float(2.0 ** 100)


def _cparams(*sem):
    return pltpu.CompilerParams(dimension_semantics=sem, vmem_limit_bytes=VMEM_LIMIT)


def _const_spec(shape):
    nd = len(shape)
    return pl.BlockSpec(shape, lambda *_: (0,) * nd, pipeline_mode=pl.Buffered(1))


def _rms(x, g):
    return x * lax.rsqrt(jnp.mean(x * x, axis=-1, keepdims=True) + EPS) * g


def _gelu_tanh(x):
    return 0.5 * x * (1.0 + jnp.tanh(np.sqrt(2.0 / np.pi).astype(np.float32) * (x + 0.044715 * (x * x * x))))


def _sigmoid(x):
    return 1.0 / (1.0 + jnp.exp(-x))


def _dot(a, b):
    return jnp.dot(a, b, preferred_element_type=F32)


def _dot_nt(a, b):
    return lax.dot_general(a, b, (((1,), (1,)), ((), ())), preferred_element_type=F32)


def _dot_tn(a, b):
    return lax.dot_general(a, b, (((0,), (0,)), ((), ())), preferred_element_type=F32)


def _masked_softmax(s, mask):
    s = jnp.where(mask, s, -jnp.inf)
    m = jnp.max(s, axis=-1, keepdims=True)
    m = jnp.where(m > -jnp.inf, m, 0.0)
    e = jnp.exp2(s - m)
    d = jnp.sum(e, axis=-1, keepdims=True)
    return e / jnp.where(d > 0, d, 1.0)


def _lane_column(x, col):
    lane = lax.broadcasted_iota(jnp.int32, x.shape, 1)
    return jnp.sum(jnp.where(lane == col, x, 0.0), axis=-1, keepdims=True)


def _pack_w_in(w_in):
    sizes = (512, 512, 1024, 1024, GLA_GATE_RANK, 1024, NSA_KV_W, NSA_KV_W, NSA_KV_W, NSA_HEADS * 3, 2 * D_MODEL)
    pts = [int(v) for v in np.cumsum(sizes)[:-1]]
    gq, gk, gv, gr, ga, nq, nkc, nks, nkw, ng, mg = jnp.split(w_in, pts, axis=1)
    small = jnp.concatenate([ga, ng, jnp.zeros((D_MODEL, _SMALL_W - GLA_GATE_RANK - NSA_HEADS * 3), w_in.dtype)], axis=1)
    return jnp.concatenate([gq, gk, gv, gr, small, nq, nkc, nks, nkw, mg], axis=1).astype(BF16)


def _pack_w_kv_t(w_in):
    lo = 512 + 512 + 1024 + 1024 + GLA_GATE_RANK + NSA_HEADS * NSA_DH
    return w_in[:, lo:lo + _NK_W].T.astype(BF16)


def _proj_in_kernel(x_ref, g_ref, w_ref, *refs, kv_major, tiles_per_seq=None, n_slc=None):
    h = _rms(x_ref[...], g_ref[...]).astype(BF16)

    def seg(off, width):
        return _dot(h, w_ref[:, off:off + width])

    if kv_major:
        wkv_ref, gla_ref, small_ref, nq_ref, kc_ref, ks_ref, kw_ref, ksa_ref, vsa_ref, kwp_ref, vwa_ref, mg_ref = refs
        nk = _dot_nt(wkv_ref[...], h)
        kc_ref[0] = nk[0:NSA_KV_W]
        ks_ref[0] = nk[NSA_KV_W:2 * NSA_KV_W]
        kw_ref[0] = nk[2 * NSA_KV_W:3 * NSA_KV_W]
        nkb = nk.astype(BF16)
        tm = nk.shape[1]
        DH = NSA_DH
        tok = (pl.program_id(0) % tiles_per_seq) * tm + lax.broadcasted_iota(jnp.int32, (n_slc, tm), 1)
        blk = lax.broadcasted_iota(jnp.int32, (n_slc, tm), 0)
        onehot = jnp.where(blk == tok // SLC_BLOCK, _MASK_BIG, 0.0).astype(BF16)
        zeros = jnp.zeros((DH, tm), BF16)
        ones = jnp.ones((DH, tm), BF16)
        for g in range(NSA_KV_GROUPS):
            s0 = NSA_KV_W + g * 2 * DH
            w0 = 2 * NSA_KV_W + g * 2 * DH
            ksa_ref[0, g * (2 * DH + n_slc):(g + 1) * (2 * DH + n_slc)] = jnp.concatenate(
                [nkb[s0:s0 + DH], zeros, onehot], axis=0)
            vsa_ref[0, g * 2 * DH:(g + 1) * 2 * DH] = jnp.concatenate([ones, nkb[s0 + DH:s0 + 2 * DH]], axis=0)
            kwp_ref[0, g * 2 * DH:(g + 1) * 2 * DH] = jnp.concatenate([nkb[w0:w0 + DH], zeros], axis=0)
            vwa_ref[0, g * 2 * DH:(g + 1) * 2 * DH] = jnp.concatenate([ones, nkb[w0 + DH:w0 + 2 * DH]], axis=0)
    else:
        gla_ref, small_ref, nq_ref, kc_ref, ks_ref, kw_ref, mg_ref = refs
        nk = seg(_NK_OFF, _NK_W)
        kc_ref[...] = nk[:, 0:NSA_KV_W]
        ks_ref[...] = nk[:, NSA_KV_W:2 * NSA_KV_W]
        kw_ref[...] = nk[:, 2 * NSA_KV_W:3 * NSA_KV_W]
    gla_ref[...] = seg(_GLA_OFF, _GLA_W)
    small_ref[...] = seg(_SMALL_OFF, _SMALL_W)
    nq_ref[...] = (seg(_NQ_OFF, _NQ_W) * _Q_SCALE).astype(BF16)
    mg_ref[...] = seg(_MG_OFF, _MG_W)


def _proj_in(x, g_pre, w_pack, tm, w_kv_t=None, seq=None):
    m = x.shape[0]
    assert m % tm == 0
    kv_major = w_kv_t is not None
    row = lambda w: pl.BlockSpec((tm, w), lambda i: (i, 0))
    sds = jax.ShapeDtypeStruct
    in_specs = [row(D_MODEL), _const_spec((1, D_MODEL)), _const_spec((D_MODEL, _PACK_W))]
    args = [x, g_pre.reshape(1, D_MODEL), w_pack]
    head = [(sds((m, _GLA_W), F32), row(_GLA_W)), (sds((m, _SMALL_W), F32), row(_SMALL_W)),
            (sds((m, _NQ_W), BF16), row(_NQ_W))]
    if kv_major:
        assert seq % tm == 0
        tps = seq // tm
        col = lambda w: pl.BlockSpec((1, w, tm), lambda i: (i // tps, 0, i % tps))
        in_specs.append(_const_spec((_NK_W, D_MODEL)))
        args.append(w_kv_t)
        n_slc = seq // SLC_BLOCK
        att_rows = [NSA_KV_GROUPS * (2 * NSA_DH + n_slc)] + [NSA_KV_W] * 3
        kv = ([(sds((m // seq, NSA_KV_W, seq), F32), col(NSA_KV_W))] * 3
              + [(sds((m // seq, r, seq), BF16), col(r)) for r in att_rows])
        static = dict(tiles_per_seq=tps, n_slc=n_slc)
    else:
        kv = [(sds((m, NSA_KV_W), F32), row(NSA_KV_W))] * 3
        static = {}
    outs = head + kv + [(sds((m, _MG_W), F32), row(_MG_W))]
    return pl.pallas_call(
        functools.partial(_proj_in_kernel, kv_major=kv_major, **static),
        out_shape=tuple(o[0] for o in outs), grid=(m // tm,), in_specs=in_specs,
        out_specs=tuple(o[1] for o in outs),
        compiler_params=_cparams("parallel"), name="proj_in",
    )(*args)


def _log_decay(small, wg, bg):
    x = _dot(small.astype(BF16), wg) + bg
    return (jnp.minimum(x, 0.0) - jnp.log1p(jnp.exp(-jnp.abs(x)))) * (1.0 / GLA_TAU)


def _pack_w_gate(w_gate_up):
    pad = jnp.zeros((_SMALL_W - GLA_GATE_RANK, GLA_HEADS * GLA_DK), w_gate_up.dtype)
    return jnp.concatenate([w_gate_up, pad], axis=0).astype(BF16)


def _gla_out(o, gn, r):
    o = o * lax.rsqrt(jnp.mean(o * o, axis=-1, keepdims=True) + EPS) * gn
    return (o * (r * _sigmoid(r))).astype(BF16)


def _gla_chunk_kernel(q_ref, k_ref, v_ref, r_ref, small_ref, wg_ref, bg_ref, gn_ref, og_ref, st_ref,
                      s_scr, kp_scr, bp_scr, *, chunk):
    c = pl.program_id(1)
    SUB, DK, DV = GLA_SUB, GLA_DK, GLA_DV

    @pl.when(c == 0)
    def _():
        s_scr[...] = jnp.zeros_like(s_scr)
        kp_scr[:, 0:SUB, :] = jnp.zeros((GLA_HEADS, SUB, DK), F32)
        bp_scr[:, 0:SUB, :] = jnp.zeros((GLA_HEADS, SUB, DK), F32)

    small = small_ref[...]
    for h in range(GLA_HEADS):
        kh = slice(h * DK, (h + 1) * DK)
        vh = slice(h * DV, (h + 1) * DV)
        la = _log_decay(small, wg_ref[:, kh], bg_ref[:, kh])
        o, st_new = _gla_head_chunk(q_ref[:, kh] * (DK ** -0.5), k_ref[:, kh], v_ref[:, vh].astype(BF16), la,
                                    s_scr[h], kp_scr.at[h], bp_scr.at[h], chunk)
        s_scr[h] = st_new
        og_ref[:, vh] = _gla_out(o, gn_ref[:, vh], r_ref[:, vh])

    @pl.when(c == pl.num_programs(1) - 1)
    def _():
        for h in range(GLA_HEADS):
            st_ref[0, h] = s_scr[h].T


def _gla_head_chunk(q, k, vb, la, st, kp_ref, bp_ref, chunk):
    C, SUB = chunk, GLA_SUB
    ri = lax.broadcasted_iota(jnp.int32, (C, C), 0)
    ci = lax.broadcasted_iota(jnp.int32, (C, C), 1)
    tri = jnp.where(ci <= ri, 1.0, 0.0).astype(BF16)
    hi = la.astype(BF16)
    r1 = la - hi.astype(F32)
    mid = r1.astype(BF16)
    lo = (r1 - mid.astype(F32)).astype(BF16)
    b = _dot(tri, hi) + _dot(tri, mid) + _dot(tri, lo)

    o = _dot_nt((q * jnp.exp(b)).astype(BF16), st.astype(BF16))
    b_last = b[C - 1:C, :]
    kdec = (k * jnp.exp(b_last - b)).astype(BF16)
    st_new = st * jnp.exp(b_last) + _dot_tn(vb, kdec)

    row = lax.broadcasted_iota(jnp.int32, (C, GLA_DK), 0)
    blocks = [jnp.zeros((SUB, C), F32)]
    for i in range(1, C // SUB):
        beta = b[SUB * i - 1:SUB * i, :]
        qi = (q[SUB * i:SUB * (i + 1)] * jnp.exp(b[SUB * i:SUB * (i + 1)] - beta)).astype(BF16)
        ki = (k * jnp.exp(jnp.where(row < SUB * i, beta - b, -jnp.inf))).astype(BF16)
        blocks.append(_dot_nt(qi, ki))
    a = jnp.concatenate(blocks, axis=0)

    kp_ref[SUB:SUB + C, :] = k
    bp_ref[SUB:SUB + C, :] = b
    tmod = ri % SUB
    for j in range(SUB):
        ks = kp_ref[SUB - j:SUB - j + C, :]
        bs = bp_ref[SUB - j:SUB - j + C, :]
        aj = jnp.sum(q * ks * jnp.exp(b - bs), axis=-1, keepdims=True)
        a = a + jnp.where((ci == ri - j) & (tmod >= j), aj, 0.0)
    return o + _dot(a.astype(BF16), vb), st_new


def _gla_prompt(gla, small, w_gate_pack, b_gate, g_norm, batch, seq, chunk=128):
    n = seq // chunk
    assert seq % chunk == 0 and chunk % GLA_SUB == 0
    H = GLA_HEADS
    kw, vw = H * GLA_DK, H * GLA_DV
    assert vw == 2 * kw
    kern = functools.partial(_gla_chunk_kernel, chunk=chunk)
    return pl.pallas_call(
        kern,
        out_shape=(jax.ShapeDtypeStruct((batch * seq, vw), BF16),
                   jax.ShapeDtypeStruct((batch, H, GLA_DK, GLA_DV), F32)),
        grid=(batch, n),
        in_specs=[
            pl.BlockSpec((chunk, kw), lambda b, c: (b * n + c, 0)),
            pl.BlockSpec((chunk, kw), lambda b, c: (b * n + c, 1)),
            pl.BlockSpec((chunk, vw), lambda b, c: (b * n + c, 1)),
            pl.BlockSpec((chunk, vw), lambda b, c: (b * n + c, 2)),
            pl.BlockSpec((chunk, _SMALL_W), lambda b, c: (b * n + c, 0)),
            _const_spec((_SMALL_W, kw)), _const_spec((1, kw)), _const_spec((1, vw)),
        ],
        out_specs=(pl.BlockSpec((chunk, vw), lambda b, c: (b * n + c, 0)),
                   pl.BlockSpec((1, H, GLA_DK, GLA_DV), lambda b, c: (b, 0, 0, 0))),
        scratch_shapes=[pltpu.VMEM((H, GLA_DV, GLA_DK), F32),
                        pltpu.VMEM((H, GLA_SUB + chunk, GLA_DK), F32),
                        pltpu.VMEM((H, GLA_SUB + chunk, GLA_DK), F32)],
        compiler_params=_cparams("parallel", "arbitrary"), name="gla_chunk",
    )(gla, gla, gla, gla, small, w_gate_pack, b_gate.reshape(1, -1), g_norm.reshape(1, -1))


def _pack_w_cmp1(w1):
    w = w1.reshape(2, 2, CMP_STRIDE, NSA_DH, CMP_HIDDEN)
    per_c = jnp.concatenate([w[:, 0], w[:, 1]], axis=-1)
    z = jnp.zeros_like(per_c[0])
    w16 = jnp.concatenate([jnp.concatenate([per_c[0], z], axis=-1),
                           jnp.concatenate([z, per_c[1]], axis=-1)], axis=1)
    return w16.reshape(CMP_STRIDE // 2, 4 * NSA_DH, 4 * CMP_HIDDEN).astype(BF16)


def _compress_kernel(pt_ref, kv_hbm, wp_ref, w1_ref, pef_ref, b1_ref, w2_ref, b2_ref, out_ref,
                     buf, tok_scr, hb_scr, sem, *, n_pages, pages_per_row):
    bidx = pl.program_id(0)
    slot = bidx % 2
    n_chunk = n_pages * (PAGE_SIZE // CMP_STRIDE)
    GW = 2 * NSA_DH

    def page_copy(seq, p, slt):
        page = pt_ref[seq, p]
        off = pl.multiple_of((page % pages_per_row) * PAGE_SIZE, PAGE_SIZE)
        src = kv_hbm.at[page // pages_per_row, :, :, :, pl.ds(off, PAGE_SIZE)]
        dst = buf.at[slt, :, :, :, pl.ds(pl.multiple_of(p * PAGE_SIZE, PAGE_SIZE), PAGE_SIZE)]
        return pltpu.make_async_copy(src, dst, sem.at[slt])

    def start_pages(seq, slt):
        def start(p, carry):
            page_copy(seq, p, slt).start()
            return carry
        lax.fori_loop(0, n_pages, start, 0)

    @pl.when(bidx == 0)
    def _():
        start_pages(0, 0)

    @pl.when(bidx + 1 < pl.num_programs(0))
    def _():
        start_pages(bidx + 1, 1 - slot)

    def wait(p, carry):
        page_copy(bidx, p, slot).wait()
        return carry

    lax.fori_loop(0, n_pages, wait, 0)

    HID = CMP_HIDDEN
    hb_scr[n_chunk:n_chunk + 8, :] = jnp.zeros((8, HID), F32)
    const = [_dot(pef_ref[c].astype(BF16), w1_ref[c]) + b1_ref[c] for c in range(2)]
    for g in range(NSA_KV_GROUPS):

        def to_token_major(p, carry):
            off = pl.multiple_of(p * PAGE_SIZE, PAGE_SIZE)
            tok_scr[pl.ds(off, PAGE_SIZE), :] = buf[slot, g, :, :, pl.ds(off, PAGE_SIZE)].reshape(GW, PAGE_SIZE).T
            return carry

        lax.fori_loop(0, n_pages, to_token_major, 0, unroll=8)
        acc = jnp.zeros((n_chunk, 4 * HID), F32)
        for p in range(0, CMP_STRIDE, 2):
            xs = [tok_scr[pl.ds(p + i, n_chunk, stride=CMP_STRIDE), :].astype(BF16) for i in range(2)]
            acc = acc + _dot(jnp.concatenate(xs, axis=1), wp_ref[p // 2])
        for c in range(2):
            hb_scr[0:n_chunk, :] = acc[:, (2 * c + 1) * HID:(2 * c + 2) * HID]
            hid = acc[:, 2 * c * HID:(2 * c + 1) * HID] + hb_scr[pl.ds(1, n_chunk), :] + const[c]
            out_ref[0, g * 2 + c] = _dot(_gelu_tanh(hid).astype(BF16), w2_ref[c]) + b2_ref[c]


def _compress(page_table, kv_t, wp, w1b, pe_flat, b1, w2b, b2):
    batch, n_pages = page_table.shape
    n_chunk = n_pages * (PAGE_SIZE // CMP_STRIDE)
    assert kv_t.shape[-1] % PAGE_SIZE == 0
    kern = functools.partial(_compress_kernel, n_pages=n_pages, pages_per_row=kv_t.shape[-1] // PAGE_SIZE)
    grid_spec = pltpu.PrefetchScalarGridSpec(
        num_scalar_prefetch=1, grid=(batch,),
        in_specs=[pl.BlockSpec(memory_space=pl.ANY),
                  _const_spec((CMP_STRIDE // 2, 4 * NSA_DH, 4 * CMP_HIDDEN)),
                  _const_spec((2, CMP_LEN * NSA_DH, CMP_HIDDEN)),
                  _const_spec((2, 1, CMP_LEN * NSA_DH)),
                  _const_spec((2, 1, CMP_HIDDEN)),
                  _const_spec((2, CMP_HIDDEN, NSA_DH)),
                  _const_spec((2, 1, NSA_DH))],
        out_specs=pl.BlockSpec((1, 2 * NSA_KV_GROUPS, n_chunk, NSA_DH), lambda b, pt: (b, 0, 0, 0)),
        scratch_shapes=[pltpu.VMEM((2, NSA_KV_GROUPS, 2, NSA_DH, n_pages * PAGE_SIZE), F32),
                        pltpu.VMEM((n_pages * PAGE_SIZE, 2 * NSA_DH), F32),
                        pltpu.VMEM((n_chunk + 8, CMP_HIDDEN), F32),
                        pltpu.SemaphoreType.DMA((2,))])
    return pl.pallas_call(
        kern, out_shape=jax.ShapeDtypeStruct((batch, 2 * NSA_KV_GROUPS, n_chunk, NSA_DH), F32),
        grid_spec=grid_spec, compiler_params=_cparams("arbitrary"), name="compress",
    )(page_table, kv_t, wp, w1b, pe_flat, b1, w2b, b2)


def _feature_major(kv):
    return kv.transpose(0, 2, 3, 4, 1)


def _token_major(kv_t, batch, seq):
    return kv_t.reshape(batch, NSA_KV_GROUPS, 2, NSA_DH, seq).transpose(0, 4, 1, 2, 3)


def _permute_cmp(kvc):
    b, gc, n, d = kvc.shape
    return kvc.reshape(b, gc, n // 4, 4, d).transpose(0, 1, 3, 2, 4).reshape(b, gc, n, d)


def _group_queries(qall):
    return jnp.concatenate([qall[:, r * NSA_DH:(r + 1) * NSA_DH] for r in range(NSA_GROUP_HEADS)], axis=0)


def _block_importance(psum, n_slc):
    p0, p1, p2, p3 = (psum[:, m * n_slc:(m + 1) * n_slc] for m in range(4))
    lane = lax.broadcasted_iota(jnp.int32, p3.shape, 1)
    p3s = jnp.where(lane == 0, 0.0, pltpu.roll(p3, 1, axis=1))
    return ((((((p0 + p3s) + p1) + p0) + p2) + p1) + p3) + p2


def _top_blocks_t(score_t, count):
    blk = lax.broadcasted_iota(jnp.int32, score_t.shape, 0).astype(F32)
    n = float(score_t.shape[0])
    finite = score_t > -jnp.inf
    picked = jnp.zeros(score_t.shape, F32)
    for _ in range(count):
        mx = jnp.max(score_t, axis=0, keepdims=True)
        idx = jnp.min(jnp.where(score_t == mx, blk, n), axis=0, keepdims=True)
        pick = blk == idx
        picked = jnp.where(pick, 1.0, picked)
        score_t = jnp.where(pick, -jnp.inf, score_t)
    return jnp.where(finite, picked, 0.0)


_N_FORCED = 3


def _cmp_topk_kernel(q_ref, kc_ref, vc_ref, small_ref, oc_ref, sel_ref, imp_scr, *, tq, sub, n_slc):
    g = pl.program_id(1)
    q0 = pl.program_id(2) * tq
    rows = NSA_GROUP_HEADS * sub
    n_cmp = 4 * n_slc
    kc = kc_ref[0, 0].astype(BF16)
    vc = vc_ref[0, 0].astype(BF16)
    lane = lax.broadcasted_iota(jnp.int32, (1, n_cmp), 1)
    blk_end = (4 * (lane % n_slc) + lane // n_slc) * CMP_STRIDE + (CMP_LEN - 1)
    for j in range(tq // sub):
        js = slice(j * sub, (j + 1) * sub)
        q = _group_queries(q_ref[js, :])
        s = _dot_nt(q, kc)
        t_rows = (q0 + j * sub) + lax.broadcasted_iota(jnp.int32, (rows, 1), 0) % sub
        p = _masked_softmax(s, blk_end <= t_rows)
        oc = _dot(p.astype(BF16), vc)
        psum = ((p[0:sub] + p[sub:2 * sub]) + p[2 * sub:3 * sub]) + p[3 * sub:4 * sub]
        imp_scr[js, :] = _block_importance(psum, n_slc)
        sig = _sigmoid(small_ref[js, :])
        outs = []
        for r in range(NSA_GROUP_HEADS):
            gate = _lane_column(sig, _NG_LANE0 + (g * NSA_GROUP_HEADS + r) * 3)
            outs.append(gate * oc[r * sub:(r + 1) * sub])
        oc_ref[js, :] = jnp.concatenate(outs, axis=1)

    cur = (q0 + lax.broadcasted_iota(jnp.int32, (1, tq), 1)) // SLC_BLOCK
    blk = lax.broadcasted_iota(jnp.int32, (n_slc, 1), 0)
    forced = (blk == 0) | (blk == cur) | (blk == cur - 1)
    valid = blk <= cur
    score_t = jnp.where(valid & jnp.logical_not(forced), imp_scr[...].T, -jnp.inf)
    sel_t = jnp.where(valid & forced, 1.0, _top_blocks_t(score_t, min(SLC_TOPK, n_slc) - _N_FORCED))
    sel_ref[0] = (sel_t.T - 1.0).astype(BF16)


def _cmp_topk_prompt(nq, kvc_perm, small, batch, seq, tq=512, sub=128):
    G = NSA_KV_GROUPS
    n_slc = seq // SLC_BLOCK
    assert kvc_perm.shape[2] == 4 * n_slc and seq % tq == 0 and tq % sub == 0 and n_slc >= _N_FORCED
    nqt = seq // tq
    gw = NSA_GROUP_HEADS * NSA_DH
    kern = functools.partial(_cmp_topk_kernel, tq=tq, sub=sub, n_slc=n_slc)
    return pl.pallas_call(
        kern,
        out_shape=(jax.ShapeDtypeStruct((batch * seq, NSA_HEADS * NSA_DH), F32),
                   jax.ShapeDtypeStruct((batch * G, seq, n_slc), BF16)),
        grid=(batch, G, nqt),
        in_specs=[pl.BlockSpec((tq, gw), lambda b, g, i: (b * nqt + i, g)),
                  pl.BlockSpec((1, 1, 4 * n_slc, NSA_DH), lambda b, g, i: (b, 2 * g, 0, 0)),
                  pl.BlockSpec((1, 1, 4 * n_slc, NSA_DH), lambda b, g, i: (b, 2 * g + 1, 0, 0)),
                  pl.BlockSpec((tq, _SMALL_W), lambda b, g, i: (b * nqt + i, 0))],
        out_specs=(pl.BlockSpec((tq, gw), lambda b, g, i: (b * nqt + i, g)),
                   pl.BlockSpec((1, tq, n_slc), lambda b, g, i: (b * G + g, i, 0))),
        scratch_shapes=[pltpu.VMEM((tq, n_slc), F32)],
        compiler_params=_cparams("parallel", "parallel", "parallel"), name="cmp_topk",
    )(nq, kvc_perm, kvc_perm, small)


def _slc_win_kernel(q_ref, sel_ref, ks_ref, vs_ref, kw_ref, vw_ref, small_ref, oc_ref, on_ref, *, tq, tk, n_slc):
    g = pl.program_id(1)
    q0 = pl.program_id(2) * tq
    rows = NSA_GROUP_HEADS * tq
    DH = NSA_DH
    q = _group_queries(q_ref[...])
    qpad = jnp.concatenate([q, jnp.zeros((rows, DH), BF16)], axis=1)
    sel4 = jnp.concatenate([sel_ref[0]] * NSA_GROUP_HEADS, axis=0)
    qaug = jnp.concatenate([qpad, sel4], axis=1)
    t_rows = q0 + lax.broadcasted_iota(jnp.int32, (rows, 1), 0) % tq
    tok_l = lax.broadcasted_iota(jnp.int32, (1, tk), 1)

    def tile(kt, m, acc, causal):
        k0 = pl.multiple_of(kt * tk, tk)
        s = _dot(qaug, ks_ref[0, :, pl.ds(k0, tk)])
        if causal:
            s = jnp.where(k0 + tok_l <= t_rows, s, -_MASK_BIG)
        m_new = jnp.maximum(m, jnp.max(s, axis=-1, keepdims=True))
        p = jnp.exp2(s - m_new)
        acc = jnp.exp2(m - m_new) * acc + _dot_nt(p.astype(BF16), vs_ref[0, :, pl.ds(k0, tk)])
        return m_new, acc

    unroll = 4

    def tiles(i, carry):
        for u in range(unroll):
            carry = tile(unroll * i + u, *carry, False)
        return carry

    last = (q0 + tq - 1) // tk
    carry = (jnp.full((rows, 1), -1e30, F32), jnp.zeros((rows, 2 * DH), F32))
    carry = lax.fori_loop(0, last // unroll, tiles, carry)
    m, acc = lax.fori_loop(unroll * (last // unroll), last + 1, lambda kt, c: tile(kt, *c, True), carry)
    o_s = acc[:, DH:] / acc[:, 0:DH]

    wlen = WINDOW + tq
    w0 = pl.multiple_of(jnp.maximum(q0 - WINDOW, 0), tq)
    s_w = _dot(qpad, kw_ref[0, :, pl.ds(w0, wlen)])
    age = t_rows - (w0 + lax.broadcasted_iota(jnp.int32, (1, wlen), 1))
    s_w = jnp.where(age >= 0, jnp.where(age < WINDOW, s_w, -_MASK_BIG), -_MASK_BIG)
    e_w = jnp.exp2(s_w - jnp.max(s_w, axis=-1, keepdims=True))
    acc_w = _dot_nt(e_w.astype(BF16), vw_ref[0, :, pl.ds(w0, wlen)])
    o_w = acc_w[:, DH:] / acc_w[:, 0:DH]

    sig = _sigmoid(small_ref[...])
    oc = oc_ref[...]
    outs = []
    for r in range(NSA_GROUP_HEADS):
        lane0 = _NG_LANE0 + (g * NSA_GROUP_HEADS + r) * 3
        g_s = _lane_column(sig, lane0 + 1)
        g_w = _lane_column(sig, lane0 + 2)
        rs = slice(r * tq, (r + 1) * tq)
        outs.append(oc[:, r * DH:(r + 1) * DH] + g_s * o_s[rs] + g_w * o_w[rs])
    on_ref[...] = jnp.concatenate(outs, axis=1).astype(BF16)


def _slc_win_prompt(nq, selm, att, small, oc, batch, seq, tq=256, tk=512):
    G = NSA_KV_GROUPS
    n_slc = seq // SLC_BLOCK
    assert seq % tk == 0 and seq % tq == 0 and tk % tq == 0 and seq >= WINDOW + tq
    nqt = seq // tq
    gw = NSA_GROUP_HEADS * NSA_DH
    kvw = 2 * NSA_DH
    kern = functools.partial(_slc_win_kernel, tq=tq, tk=tk, n_slc=n_slc)
    return pl.pallas_call(
        kern, out_shape=jax.ShapeDtypeStruct((batch * seq, NSA_HEADS * NSA_DH), BF16),
        grid=(batch, G, nqt),
        in_specs=[pl.BlockSpec((tq, gw), lambda b, g, i: (b * nqt + i, g)),
                  pl.BlockSpec((1, tq, n_slc), lambda b, g, i: (b * G + g, i, 0)),
                  pl.BlockSpec((1, kvw + n_slc, seq), lambda b, g, i: (b, g, 0)),
                  pl.BlockSpec((1, kvw, seq), lambda b, g, i: (b, g, 0)),
                  pl.BlockSpec((1, kvw, seq), lambda b, g, i: (b, g, 0)),
                  pl.BlockSpec((1, kvw, seq), lambda b, g, i: (b, g, 0)),
                  pl.BlockSpec((tq, _SMALL_W), lambda b, g, i: (b * nqt + i, 0)),
                  pl.BlockSpec((tq, gw), lambda b, g, i: (b * nqt + i, g))],
        out_specs=pl.BlockSpec((tq, gw), lambda b, g, i: (b * nqt + i, g)),
        compiler_params=_cparams("parallel", "parallel", "parallel"), name="slc_win",
    )(nq, selm, *att, small, oc)


def _mix_out_kernel(og_ref, on_ref, mg_ref, x_ref, wgo_ref, wno_ref, wo_ref, gpost_ref, y_ref):
    yg = _dot(og_ref[...], wgo_ref[...])
    yn = _dot(on_ref[...], wno_ref[...])
    mg = mg_ref[...]
    mix = _sigmoid(mg[:, 0:D_MODEL]) * yg + _sigmoid(mg[:, D_MODEL:]) * yn
    z = _dot(mix.astype(BF16), wo_ref[...])
    y_ref[...] = x_ref[...] + _rms(z, gpost_ref[...])


def _mix_out(og, on, mg, x, wgo, wno, wo, g_post, tm):
    m = x.shape[0]
    row = lambda w: pl.BlockSpec((tm, w), lambda i: (i, 0))
    wspec = _const_spec((D_MODEL, D_MODEL))
    return pl.pallas_call(
        _mix_out_kernel, out_shape=jax.ShapeDtypeStruct((m, D_MODEL), F32), grid=(m // tm,),
        in_specs=[row(D_MODEL), row(D_MODEL), row(2 * D_MODEL), row(D_MODEL), wspec, wspec, wspec,
                  _const_spec((1, D_MODEL))],
        out_specs=row(D_MODEL), compiler_params=_cparams("parallel"), name="mix_out",
    )(og, on, mg, x, wgo, wno, wo, g_post.reshape(1, D_MODEL))


_FFN_HALO = 16


def _ffn_tail(x, conv, bgate, p, wdn_ref, gpost_ref, gple_ref, wpg_ref, wpp_ref):
    y = _dot((_gelu_tanh(conv) * bgate).astype(BF16), wdn_ref[...])
    x2 = x + _rms(y, gpost_ref[...])
    gate = _sigmoid(_dot(_rms(x2, gple_ref[...]).astype(BF16), wpg_ref[...]))
    return x2 + gate * _dot(p.astype(BF16), wpp_ref[...])


def _ffn_prompt_kernel(x_ref, xh_ref, p_ref, gpre_ref, wup_ref, cw_ref, cb_ref, wdn_ref, gpost_ref, gple_ref,
                       wpg_ref, wpp_ref, y_ref, alast_ref, h_scr, a_scr, *, tm, tiles_per_seq):
    H = _FFN_HALO
    x = x_ref[...]
    h_scr[0:H, :] = _rms(xh_ref[...], gpre_ref[...]).astype(BF16)
    h_scr[H:H + tm, :] = _rms(x, gpre_ref[...]).astype(BF16)
    ab = _dot(h_scr[...], wup_ref[...])
    keep = jnp.where(pl.program_id(0) % tiles_per_seq == 0, 0.0, 1.0)
    a_scr[0:H, :] = ab[0:H, 0:D_FF] * keep
    a_scr[H:H + tm, :] = ab[H:, 0:D_FF]
    cw = cw_ref[...]
    conv = (a_scr[H - 2:H - 2 + tm, :] * cw[0:1] + a_scr[H - 1:H - 1 + tm, :] * cw[1:2]
            + a_scr[H:H + tm, :] * cw[2:3]) + cb_ref[...]
    y_ref[...] = _ffn_tail(x, conv, ab[H:, D_FF:], p_ref[...], wdn_ref, gpost_ref, gple_ref, wpg_ref, wpp_ref)
    alast_ref[0] = a_scr[H + tm - 8:H + tm, :]


def _ffn_weight_specs():
    return [_const_spec((1, D_MODEL)), _const_spec((D_MODEL, 2 * D_FF)), _const_spec((CONV_W, D_FF)),
            _const_spec((1, D_FF)), _const_spec((D_FF, D_MODEL)), _const_spec((1, D_MODEL)),
            _const_spec((1, D_MODEL)), _const_spec((D_MODEL, D_MODEL)), _const_spec((PLE_DIM, D_MODEL))]


def _ffn_prompt(x, p, fw, batch, seq, tm=256):
    assert seq % tm == 0 and tm % _FFN_HALO == 0
    tps = seq // tm
    hb = tm // _FFN_HALO
    kern = functools.partial(_ffn_prompt_kernel, tm=tm, tiles_per_seq=tps)
    return pl.pallas_call(
        kern,
        out_shape=(jax.ShapeDtypeStruct((batch * seq, D_MODEL), F32), jax.ShapeDtypeStruct((batch, 8, D_FF), F32)),
        grid=(batch * tps,),
        in_specs=[pl.BlockSpec((tm, D_MODEL), lambda i: (i, 0)),
                  pl.BlockSpec((_FFN_HALO, D_MODEL), lambda i: (jnp.maximum(i * hb - 1, 0), 0)),
                  pl.BlockSpec((tm, PLE_DIM), lambda i: (i, 0))] + _ffn_weight_specs(),
        out_specs=(pl.BlockSpec((tm, D_MODEL), lambda i: (i, 0)),
                   pl.BlockSpec((1, 8, D_FF), lambda i: (i // tps, 0, 0))),
        scratch_shapes=[pltpu.VMEM((_FFN_HALO + tm, D_MODEL), BF16), pltpu.VMEM((_FFN_HALO + tm, D_FF), F32)],
        compiler_params=_cparams("arbitrary"), name="ffn_prompt",
    )(x, x, p, *fw)


def _layer_prompt(x, p, w, tm=256):
    batch, seq, _ = x.shape
    m = batch * seq
    x2 = x.reshape(m, D_MODEL)
    gla, small, nq, kc_t, ks_t, kw_t, *att, mg = _proj_in(x2, w["g_mix_pre"], w["w_in_pack"], tm, w["w_kv_t"], seq)
    og, gla_state = _gla_prompt(gla, small, w["w_gate_pack"], w["b_gla_gate"], w["g_gla_norm"], batch, seq)

    n_pages = seq // PAGE_SIZE
    ident_pt = jnp.arange(batch * n_pages, dtype=jnp.int32).reshape(batch, n_pages)
    kvc = _compress(ident_pt, kc_t.reshape(batch, NSA_KV_GROUPS, 2, NSA_DH, seq), *w["cmp"])
    oc, selm = _cmp_topk_prompt(nq, _permute_cmp(kvc), small, batch, seq)
    on = _slc_win_prompt(nq, selm, att, small, oc, batch, seq)

    x1 = _mix_out(og, on, mg, x2, w["w_gla_out"], w["w_nsa_out"], w["w_out"], w["g_mix_post"], tm)
    y, alast = _ffn_prompt(x1, p.reshape(m, PLE_DIM), w["ffn"], batch, seq, tm)

    win = min(WINDOW, seq)
    return (y.reshape(batch, seq, D_MODEL), _token_major(kc_t, batch, seq), _token_major(ks_t, batch, seq),
            _token_major(kw_t[:, :, seq - win:], batch, win), gla_state, alast[:, 8 - (CONV_W - 1):])


def _row_to_col(row):
    n = row.shape[-1]
    ri = lax.broadcasted_iota(jnp.int32, (n, n), 0)
    ci = lax.broadcasted_iota(jnp.int32, (n, n), 1)
    return jnp.sum(jnp.where(ri == ci, jnp.broadcast_to(row, (n, n)), 0.0), axis=-1, keepdims=True)


def _gla_step_kernel(gla_ref, small_ref, s0_ref, wg_ref, bg_ref, gn_ref, og_ref, s1_ref):
    row = gla_ref[0]
    small = small_ref[0]
    H, DK, DV = GLA_HEADS, GLA_DK, GLA_DV
    outs = []
    for h in range(H):
        q = row[:, h * DK:(h + 1) * DK] * (DK ** -0.5)
        k = row[:, H * DK + h * DK:H * DK + (h + 1) * DK]
        v = row[:, 2 * H * DK + h * DV:2 * H * DK + (h + 1) * DV]
        r = row[:, 2 * H * DK + H * DV + h * DV:2 * H * DK + H * DV + (h + 1) * DV]
        alpha = jnp.exp(_log_decay(small, wg_ref[:, h * DK:(h + 1) * DK], bg_ref[:, h * DK:(h + 1) * DK]))
        s0 = s0_ref[0, h]
        s1_ref[0, h] = _row_to_col(alpha) * s0 + _row_to_col(k) * v
        o = _dot((q * alpha).astype(BF16), s0.astype(BF16)) + jnp.sum(q * k, axis=-1, keepdims=True) * v
        outs.append(_gla_out(o, gn_ref[:, h * DV:(h + 1) * DV], r))
    og_ref[0] = jnp.concatenate(outs, axis=1)


def _gla_sample(gla, small, state, w_gate_pack, b_gate, g_norm):
    nb = gla.shape[0]
    H = GLA_HEADS
    og, s1 = pl.pallas_call(
        _gla_step_kernel,
        out_shape=(jax.ShapeDtypeStruct((nb, 1, H * GLA_DV), BF16), jax.ShapeDtypeStruct(state.shape, F32)),
        grid=(nb,),
        in_specs=[pl.BlockSpec((1, 1, _GLA_W), lambda b: (b, 0, 0)),
                  pl.BlockSpec((1, 1, _SMALL_W), lambda b: (b, 0, 0)),
                  pl.BlockSpec((1, H, GLA_DK, GLA_DV), lambda b: (b, 0, 0, 0)),
                  _const_spec((_SMALL_W, H * GLA_DK)), _const_spec((1, H * GLA_DK)), _const_spec((1, H * GLA_DV))],
        out_specs=(pl.BlockSpec((1, 1, H * GLA_DV), lambda b: (b, 0, 0)),
                   pl.BlockSpec((1, H, GLA_DK, GLA_DV), lambda b: (b, 0, 0, 0))),
        compiler_params=_cparams("parallel"), name="gla_step",
    )(gla.reshape(nb, 1, _GLA_W), small.reshape(nb, 1, _SMALL_W), state, w_gate_pack,
      b_gate.reshape(1, -1), g_norm.reshape(1, -1))
    return og.reshape(nb, H * GLA_DV), s1


def _head_group_rows(x_groups):
    grp = lax.broadcasted_iota(jnp.int32, x_groups[0].shape, 0) // NSA_GROUP_HEADS
    out = jnp.zeros(x_groups[0].shape, F32)
    for g, xg in enumerate(x_groups):
        out = jnp.where(grp == g, xg, out)
    return out


def _cmp_topk_sample_kernel(q_ref, kvc_ref, oc_ref, idx_ref, *, t, n_lane, n_pick):
    q = q_ref[0]
    n_cmp = kvc_ref.shape[2]
    lane = lax.broadcasted_iota(jnp.int32, (1, n_cmp), 1)
    blk_end = (4 * (lane % n_lane) + lane // n_lane) * CMP_STRIDE + (CMP_LEN - 1)
    ocs, imps = [], []
    for g in range(NSA_KV_GROUPS):
        s = _dot_nt(q, kvc_ref[0, 2 * g].astype(BF16))
        p = _masked_softmax(s, jnp.broadcast_to(blk_end <= t, s.shape))
        ocs.append(_dot(p.astype(BF16), kvc_ref[0, 2 * g + 1].astype(BF16)))
        r0 = NSA_GROUP_HEADS * g
        psum = ((p[r0:r0 + 1] + p[r0 + 1:r0 + 2]) + p[r0 + 2:r0 + 3]) + p[r0 + 3:r0 + 4]
        imps.append(_block_importance(psum, n_lane))
    oc_ref[0] = _head_group_rows(ocs)

    imp = jnp.concatenate(imps + [jnp.zeros((8 - NSA_KV_GROUPS, n_lane), F32)], axis=0)
    blk = lax.broadcasted_iota(jnp.int32, imp.shape, 1)
    cur = t // SLC_BLOCK
    forced = (blk == 0) | (blk == cur) | (blk == cur - 1)
    score = jnp.where(blk <= cur, jnp.where(forced, FORCE_SCORE, imp), -jnp.inf)
    idx_out = jnp.zeros(imp.shape, jnp.int32)
    for i in range(n_pick):
        mx = jnp.max(score, axis=-1, keepdims=True)
        idx = jnp.min(jnp.where(score == mx, blk, n_lane), axis=-1, keepdims=True)
        idx_out = jnp.where(blk == i, idx, idx_out)
        score = jnp.where(blk == idx, -jnp.inf, score)
    idx_ref[0] = idx_out


def _attend_with_new_token(qpad, qf, kv, valid, new_row):
    DH = NSA_DH
    is_k = lax.broadcasted_iota(jnp.int32, kv.shape, 0) < DH
    s = jnp.where(valid, _dot(qpad, jnp.where(is_k, kv, jnp.zeros_like(kv))), -jnp.inf)
    s_n = jnp.sum(qf * new_row[:, 0:DH].astype(BF16).astype(F32), axis=-1, keepdims=True)
    m = jnp.maximum(jnp.max(s, axis=-1, keepdims=True), s_n)
    e = jnp.exp2(s - m)
    en = jnp.exp2(s_n - m)
    num = _dot_nt(e.astype(BF16), kv)[:, DH:] + en * new_row[:, DH:]
    return num / (jnp.sum(e, axis=-1, keepdims=True) + en)


def _slc_win_sample_kernel(idx_ref, pt_ref, *refs, t, win, n_pick):
    G = NSA_KV_GROUPS
    GW = 2 * NSA_DH
    kv_refs = refs[:G * n_pick]
    q_ref, ksn_ref, kwn_ref, cw_ref, small_ref, oc_ref, on_ref = refs[G * n_pick:]
    b = pl.program_id(0)
    q = q_ref[0]
    qf = q.astype(F32)
    qpad = jnp.concatenate([q, jnp.zeros_like(q)], axis=1)
    ksn = ksn_ref[0]
    kwn = kwn_ref[0]
    blocks_per_page = PAGE_SIZE // SLC_BLOCK
    lane_blk = (lax.broadcasted_iota(jnp.int32, (1, n_pick * PAGE_SIZE), 1) // SLC_BLOCK) % blocks_per_page
    pos = (t - win) + lax.broadcasted_iota(jnp.int32, (1, win), 1)
    w_valid = (pos <= t) & (pos > t - WINDOW) & (pos >= 0)
    o_s, o_w = [], []
    for g in range(G):
        ks = range(g * n_pick, (g + 1) * n_pick)
        kv = jnp.concatenate([kv_refs[i][0, 0].reshape(GW, PAGE_SIZE) for i in ks], axis=1).astype(BF16)
        want = jnp.concatenate([jnp.full((1, PAGE_SIZE), idx_ref[b, i] % blocks_per_page, jnp.int32) for i in ks], axis=1)
        o_s.append(_attend_with_new_token(qpad, qf, kv, lane_blk == want, ksn[:, g * GW:(g + 1) * GW]))
        kvw = cw_ref[0, g].reshape(GW, win).astype(BF16)
        o_w.append(_attend_with_new_token(qpad, qf, kvw, w_valid, kwn[:, g * GW:(g + 1) * GW]))
    o_s = _head_group_rows(o_s)
    o_w = _head_group_rows(o_w)
    sig = jnp.broadcast_to(_sigmoid(small_ref[0]), (NSA_HEADS, _SMALL_W))
    lane = lax.broadcasted_iota(jnp.int32, sig.shape, 1)
    head = lax.broadcasted_iota(jnp.int32, sig.shape, 0)
    gate = [jnp.sum(jnp.where(lane == _NG_LANE0 + 3 * head + x, sig, 0.0), axis=-1, keepdims=True) for x in range(3)]
    on_ref[0] = (gate[0] * oc_ref[0] + gate[1] * o_s + gate[2] * o_w).astype(BF16)


def _nsa_sample(nq, ks_new, kw_new, small, kvc_perm, cache_slc, cache_win, page_table):
    nb, n_pages = page_table.shape
    past = n_pages * PAGE_SIZE
    t = past
    n_lane = past // SLC_BLOCK
    n_pick = SLC_TOPK - 1
    assert kvc_perm.shape[2] == 4 * n_lane and n_lane >= SLC_TOPK and n_lane % 128 == 0
    win = cache_win.shape[1]
    q3 = nq.reshape(nb, NSA_HEADS, NSA_DH)
    G = NSA_KV_GROUPS

    oc, idx = pl.pallas_call(
        functools.partial(_cmp_topk_sample_kernel, t=t, n_lane=n_lane, n_pick=n_pick),
        out_shape=(jax.ShapeDtypeStruct((nb, NSA_HEADS, NSA_DH), F32), jax.ShapeDtypeStruct((nb, 8, n_lane), jnp.int32)),
        grid=(nb,),
        in_specs=[pl.BlockSpec((1, NSA_HEADS, NSA_DH), lambda b: (b, 0, 0)),
                  pl.BlockSpec((1, 2 * G, 4 * n_lane, NSA_DH), lambda b: (b, 0, 0, 0))],
        out_specs=(pl.BlockSpec((1, NSA_HEADS, NSA_DH), lambda b: (b, 0, 0)),
                   pl.BlockSpec((1, 8, n_lane), lambda b: (b, 0, 0))),
        compiler_params=_cparams("parallel"), name="cmp_topk_sample",
    )(q3, kvc_perm)
    blk_idx = idx[:, :G, :n_pick].reshape(nb, G * n_pick)

    half = PAGE_SIZE // SLC_BLOCK
    slc_t = _feature_major(cache_slc)
    win_t = _feature_major(cache_win)

    def kv_spec(i):
        def index(b, idx_ref, pt_ref):
            return (pt_ref[b, idx_ref[b, i] // half], i // n_pick, 0, 0, 0)
        return pl.BlockSpec((1, 1, 2, NSA_DH, PAGE_SIZE), index)

    per_seq = lambda shape: pl.BlockSpec((1,) + shape, lambda b, *_: (b,) + (0,) * len(shape))
    grid_spec = pltpu.PrefetchScalarGridSpec(
        num_scalar_prefetch=2, grid=(nb,),
        in_specs=[kv_spec(i) for i in range(G * n_pick)] + [
            per_seq((NSA_HEADS, NSA_DH)), per_seq((1, NSA_KV_W)), per_seq((1, NSA_KV_W)),
            per_seq((G, 2, NSA_DH, win)), per_seq((1, _SMALL_W)), per_seq((NSA_HEADS, NSA_DH))],
        out_specs=per_seq((NSA_HEADS, NSA_DH)))
    on = pl.pallas_call(
        functools.partial(_slc_win_sample_kernel, t=t, win=win, n_pick=n_pick),
        out_shape=jax.ShapeDtypeStruct((nb, NSA_HEADS, NSA_DH), BF16), grid_spec=grid_spec,
        compiler_params=_cparams("parallel"), name="slc_win_sample",
    )(blk_idx, page_table, *([slc_t] * (G * n_pick)), q3,
      ks_new.reshape(nb, 1, NSA_KV_W), kw_new.reshape(nb, 1, NSA_KV_W),
      win_t, small.reshape(nb, 1, _SMALL_W), oc)
    return on.reshape(nb, NSA_HEADS * NSA_DH)


def _ffn_sample_kernel(x_ref, s0_ref, s1_ref, p_ref, gpre_ref, wup_ref, cw_ref, cb_ref, wdn_ref, gpost_ref,
                       gple_ref, wpg_ref, wpp_ref, y_ref, a_ref):
    x = x_ref[...]
    ab = _dot(_rms(x, gpre_ref[...]).astype(BF16), wup_ref[...])
    a = ab[:, 0:D_FF]
    cw = cw_ref[...]
    conv = (s0_ref[...] * cw[0:1] + s1_ref[...] * cw[1:2] + a * cw[2:3]) + cb_ref[...]
    y_ref[...] = _ffn_tail(x, conv, ab[:, D_FF:], p_ref[...], wdn_ref, gpost_ref, gple_ref, wpg_ref, wpp_ref)
    a_ref[...] = a


def _ffn_sample(x, conv_state, p, fw):
    nb = x.shape[0]
    full = lambda w: pl.BlockSpec((nb, w), lambda i: (0, 0))
    return pl.pallas_call(
        _ffn_sample_kernel,
        out_shape=(jax.ShapeDtypeStruct((nb, D_MODEL), F32), jax.ShapeDtypeStruct((nb, D_FF), F32)),
        grid=(1,),
        in_specs=[full(D_MODEL), full(D_FF), full(D_FF), full(PLE_DIM)] + _ffn_weight_specs(),
        out_specs=(full(D_MODEL), full(D_FF)),
        compiler_params=_cparams("arbitrary"), name="ffn_sample",
    )(x, conv_state[:, 0], conv_state[:, 1], p, *fw)


def _layer_sample(x, p, cache_cmp, cache_slc, cache_win, state_gla, state_conv, page_table, w):
    nb, s, _ = x.shape
    assert s == 1
    x2 = x.reshape(nb, D_MODEL)
    gla, small, nq, kc, ks, kw, mg = _proj_in(x2, w["g_mix_pre"], w["w_in_pack"], nb)
    og, gla_state = _gla_sample(gla, small, state_gla, w["w_gate_pack"], w["b_gla_gate"], w["g_gla_norm"])

    kvc = _compress(page_table, _feature_major(cache_cmp), *w["cmp"])
    on = _nsa_sample(nq, ks, kw, small, _permute_cmp(kvc), cache_slc, cache_win, page_table)

    x1 = _mix_out(og, on, mg, x2, w["w_gla_out"], w["w_nsa_out"], w["w_out"], w["g_mix_post"], nb)
    y, a = _ffn_sample(x1, state_conv, p.reshape(nb, PLE_DIM), w["ffn"])
    kv_shape = (nb, 1, NSA_KV_GROUPS, 2, NSA_DH)
    conv_new = jnp.stack([state_conv[:, 1], a], axis=1)
    return (y.reshape(nb, 1, D_MODEL), kc.reshape(kv_shape), ks.reshape(kv_shape), kw.reshape(kv_shape),
            gla_state, conv_new)


def _prep_weights(g_mix_pre, g_mix_post, g_ffn_pre, g_ffn_post, g_ple, w_in, w_gla_gate_up, b_gla_gate,
                  g_gla_norm, w_gla_out, w_cmp1, b_cmp1, w_cmp2, b_cmp2, pe_cmp, w_nsa_out, w_out, w_ffn_up,
                  conv_ffn_w, b_conv_ffn, w_ffn_down, w_ple_proj, w_ple_gate):
    pe_flat = pe_cmp.transpose(1, 0, 2).reshape(2, 1, CMP_LEN * NSA_DH)
    row = lambda v: v.reshape(1, -1)
    return dict(
        g_mix_pre=g_mix_pre, g_mix_post=g_mix_post, w_in_pack=_pack_w_in(w_in), w_kv_t=_pack_w_kv_t(w_in),
        w_gate_pack=_pack_w_gate(w_gla_gate_up), b_gla_gate=b_gla_gate, g_gla_norm=g_gla_norm,
        w_gla_out=w_gla_out.astype(BF16), w_nsa_out=w_nsa_out.astype(BF16), w_out=w_out.astype(BF16),
        cmp=(_pack_w_cmp1(w_cmp1), w_cmp1.astype(BF16), pe_flat, b_cmp1.reshape(2, 1, CMP_HIDDEN),
             w_cmp2.astype(BF16), b_cmp2.reshape(2, 1, NSA_DH)),
        ffn=(row(g_ffn_pre), w_ffn_up.astype(BF16), conv_ffn_w, row(b_conv_ffn), w_ffn_down.astype(BF16),
             row(g_ffn_post), row(g_ple), w_ple_gate.astype(BF16), w_ple_proj.astype(BF16)),
    )


def kernel(x_prompt, x_sample, cache_cmp_kv, cache_slc_kv, cache_win_kv, state_gla, state_ffn_conv, page_table,
           p_prompt, p_sample, g_mix_pre, g_mix_post, g_ffn_pre, g_ffn_post, g_ple, w_in, w_gla_gate_up,
           b_gla_gate, g_gla_norm, w_gla_out, w_cmp1, b_cmp1, w_cmp2, b_cmp2, pe_cmp, w_nsa_out, w_out, w_ffn_up,
           conv_ffn_w, b_conv_ffn, w_ffn_down, w_ple_proj, w_ple_gate):
    layer_weights = (g_mix_pre, g_mix_post, g_ffn_pre, g_ffn_post, g_ple, w_in, w_gla_gate_up, b_gla_gate,
                     g_gla_norm, w_gla_out, w_cmp1, b_cmp1, w_cmp2, b_cmp2, pe_cmp, w_nsa_out, w_out, w_ffn_up,
                     conv_ffn_w, b_conv_ffn, w_ffn_down, w_ple_proj, w_ple_gate)
    depth = w_in.shape[0]
    yp, ys = x_prompt, x_sample
    extras_p, extras_s = [], []
    for i in range(depth):
        w = _prep_weights(*[v[i] for v in layer_weights])
        yp, *ep = _layer_prompt(yp, p_prompt[i], w)
        ys, *es = _layer_sample(ys, p_sample[i], cache_cmp_kv[i], cache_slc_kv[i], cache_win_kv[i], state_gla[i],
                                state_ffn_conv[i], page_table, w)
        extras_p.append(ep)
        extras_s.append(es)
    stack = lambda rows, j: jnp.stack([r[j] for r in rows])
    outs = [yp, ys]
    for j in range(5):
        outs += [stack(extras_p, j), stack(extras_s, j)]
    return tuple(outs)
```

```python
import functools

import numpy as np
import jax
import jax.numpy as jnp
from jax import lax
from jax.experimental import pallas as pl
from jax.experimental.pallas import tpu as pltpu

F32 = jnp.float32
BF16 = jnp.bfloat16

D_MODEL = 1024
PAGE_SIZE = 128
GLA_HEADS = 4
GLA_DK = 128
GLA_DV = 256
GLA_GATE_RANK = 16
GLA_TAU = 16.0
GLA_SUB = 16
NSA_HEADS = 16
NSA_KV_GROUPS = 4
NSA_GROUP_HEADS = 4
NSA_DH = 64
NSA_KV_W = 2 * NSA_KV_GROUPS * NSA_DH
CMP_LEN = 32
CMP_STRIDE = 16
CMP_HIDDEN = 128
SLC_BLOCK = 64
SLC_TOPK = 16
WINDOW = 512
FORCE_SCORE = 1.0e4
D_FF = 2816
CONV_W = 3
PLE_DIM = 256
EPS = 1e-6

V7X_VMEM_BYTES = 64 * 1024 * 1024
VMEM_LIMIT = V7X_VMEM_BYTES - 8 * 1024 * 1024

_GLA_OFF, _GLA_W = 0, 2 * GLA_HEADS * GLA_DK + 2 * GLA_HEADS * GLA_DV
_SMALL_OFF, _SMALL_W = _GLA_OFF + _GLA_W, 128
_NQ_OFF, _NQ_W = _SMALL_OFF + _SMALL_W, NSA_HEADS * NSA_DH
_NK_OFF, _NK_W = _NQ_OFF + _NQ_W, 3 * NSA_KV_W
_MG_OFF, _MG_W = _NK_OFF + _NK_W, 2 * D_MODEL
_PACK_W = _MG_OFF + _MG_W
_NG_LANE0 = GLA_GATE_RANK

_Q_SCALE = float(NSA_DH ** -0.5 * np.log2(np.e))

_MASK_BIG = float(2.0 ** 100)


def _cparams(*sem):
    return pltpu.CompilerParams(dimension_semantics=sem, vmem_limit_bytes=VMEM_LIMIT)


def _const_spec(shape):
    nd = len(shape)
    return pl.BlockSpec(shape, lambda *_: (0,) * nd, pipeline_mode=pl.Buffered(1))


def _rms(x, g):
    return x * lax.rsqrt(jnp.mean(x * x, axis=-1, keepdims=True) + EPS) * g


def _gelu_tanh(x):
    return 0.5 * x * (1.0 + jnp.tanh(np.sqrt(2.0 / np.pi).astype(np.float32) * (x + 0.044715 * (x * x * x))))


def _sigmoid(x):
    return 1.0 / (1.0 + jnp.exp(-x))


def _dot(a, b):
    return jnp.dot(a, b, preferred_element_type=F32)


def _dot_nt(a, b):
    return lax.dot_general(a, b, (((1,), (1,)), ((), ())), preferred_element_type=F32)


def _dot_tn(a, b):
    return lax.dot_general(a, b, (((0,), (0,)), ((), ())), preferred_element_type=F32)


def _masked_softmax(s, mask):
    s = jnp.where(mask, s, -jnp.inf)
    m = jnp.max(s, axis=-1, keepdims=True)
    m = jnp.where(m > -jnp.inf, m, 0.0)
    e = jnp.exp2(s - m)
    d = jnp.sum(e, axis=-1, keepdims=True)
    return e / jnp.where(d > 0, d, 1.0)


def _lane_column(x, col):
    lane = lax.broadcasted_iota(jnp.int32, x.shape, 1)
    return jnp.sum(jnp.where(lane == col, x, 0.0), axis=-1, keepdims=True)


def _pack_w_in(w_in):
    sizes = (512, 512, 1024, 1024, GLA_GATE_RANK, 1024, NSA_KV_W, NSA_KV_W, NSA_KV_W, NSA_HEADS * 3, 2 * D_MODEL)
    pts = [int(v) for v in np.cumsum(sizes)[:-1]]
    gq, gk, gv, gr, ga, nq, nkc, nks, nkw, ng, mg = jnp.split(w_in, pts, axis=1)
    small = jnp.concatenate([ga, ng, jnp.zeros((D_MODEL, _SMALL_W - GLA_GATE_RANK - NSA_HEADS * 3), w_in.dtype)], axis=1)
    return jnp.concatenate([gq, gk, gv, gr, small, nq, nkc, nks, nkw, mg], axis=1).astype(BF16)


def _pack_w_kv_t(w_in):
    lo = 512 + 512 + 1024 + 1024 + GLA_GATE_RANK + NSA_HEADS * NSA_DH
    return w_in[:, lo:lo + _NK_W].T.astype(BF16)


def _proj_in_kernel(x_ref, g_ref, w_ref, *refs, kv_major, tiles_per_seq=None, n_slc=None):
    h = _rms(x_ref[...], g_ref[...]).astype(BF16)

    def seg(off, width):
        return _dot(h, w_ref[:, off:off + width])

    if kv_major:
        wkv_ref, gla_ref, small_ref, nq_ref, kc_ref, ks_ref, kw_ref, ksa_ref, vsa_ref, kwp_ref, vwa_ref, mg_ref = refs
        nk = _dot_nt(wkv_ref[...], h)
        kc_ref[0] = nk[0:NSA_KV_W]
        ks_ref[0] = nk[NSA_KV_W:2 * NSA_KV_W]
        kw_ref[0] = nk[2 * NSA_KV_W:3 * NSA_KV_W]
        nkb = nk.astype(BF16)
        tm = nk.shape[1]
        DH = NSA_DH
        tok = (pl.program_id(0) % tiles_per_seq) * tm + lax.broadcasted_iota(jnp.int32, (n_slc, tm), 1)
        blk = lax.broadcasted_iota(jnp.int32, (n_slc, tm), 0)
        onehot = jnp.where(blk == tok // SLC_BLOCK, _MASK_BIG, 0.0).astype(BF16)
        zeros = jnp.zeros((DH, tm), BF16)
        ones = jnp.ones((DH, tm), BF16)
        for g in range(NSA_KV_GROUPS):
            s0 = NSA_KV_W + g * 2 * DH
            w0 = 2 * NSA_KV_W + g * 2 * DH
            ksa_ref[0, g * (2 * DH + n_slc):(g + 1) * (2 * DH + n_slc)] = jnp.concatenate(
                [nkb[s0:s0 + DH], zeros, onehot], axis=0)
            vsa_ref[0, g * 2 * DH:(g + 1) * 2 * DH] = jnp.concatenate([ones, nkb[s0 + DH:s0 + 2 * DH]], axis=0)
            kwp_ref[0, g * 2 * DH:(g + 1) * 2 * DH] = jnp.concatenate([nkb[w0:w0 + DH], zeros], axis=0)
            vwa_ref[0, g * 2 * DH:(g + 1) * 2 * DH] = jnp.concatenate([ones, nkb[w0 + DH:w0 + 2 * DH]], axis=0)
    else:
        gla_ref, small_ref, nq_ref, kc_ref, ks_ref, kw_ref, mg_ref = refs
        nk = seg(_NK_OFF, _NK_W)
        kc_ref[...] = nk[:, 0:NSA_KV_W]
        ks_ref[...] = nk[:, NSA_KV_W:2 * NSA_KV_W]
        kw_ref[...] = nk[:, 2 * NSA_KV_W:3 * NSA_KV_W]
    gla_ref[...] = seg(_GLA_OFF, _GLA_W)
    small_ref[...] = seg(_SMALL_OFF, _SMALL_W)
    nq_ref[...] = (seg(_NQ_OFF, _NQ_W) * _Q_SCALE).astype(BF16)
    mg_ref[...] = seg(_MG_OFF, _MG_W)


def _proj_in(x, g_pre, w_pack, tm, w_kv_t=None, seq=None):
    m = x.shape[0]
    assert m % tm == 0
    kv_major = w_kv_t is not None
    row = lambda w: pl.BlockSpec((tm, w), lambda i: (i, 0))
    sds = jax.ShapeDtypeStruct
    in_specs = [row(D_MODEL), _const_spec((1, D_MODEL)), _const_spec((D_MODEL, _PACK_W))]
    args = [x, g_pre.reshape(1, D_MODEL), w_pack]
    head = [(sds((m, _GLA_W), F32), row(_GLA_W)), (sds((m, _SMALL_W), F32), row(_SMALL_W)),
            (sds((m, _NQ_W), BF16), row(_NQ_W))]
    if kv_major:
        assert seq % tm == 0
        tps = seq // tm
        col = lambda w: pl.BlockSpec((1, w, tm), lambda i: (i // tps, 0, i % tps))
        in_specs.append(_const_spec((_NK_W, D_MODEL)))
        args.append(w_kv_t)
        n_slc = seq // SLC_BLOCK
        att_rows = [NSA_KV_GROUPS * (2 * NSA_DH + n_slc)] + [NSA_KV_W] * 3
        kv = ([(sds((m // seq, NSA_KV_W, seq), F32), col(NSA_KV_W))] * 3
              + [(sds((m // seq, r, seq), BF16), col(r)) for r in att_rows])
        static = dict(tiles_per_seq=tps, n_slc=n_slc)
    else:
        kv = [(sds((m, NSA_KV_W), F32), row(NSA_KV_W))] * 3
        static = {}
    outs = head + kv + [(sds((m, _MG_W), F32), row(_MG_W))]
    return pl.pallas_call(
        functools.partial(_proj_in_kernel, kv_major=kv_major, **static),
        out_shape=tuple(o[0] for o in outs), grid=(m // tm,), in_specs=in_specs,
        out_specs=tuple(o[1] for o in outs),
        compiler_params=_cparams("parallel"), name="proj_in",
    )(*args)


def _log_decay(small, wg, bg):
    x = _dot(small.astype(BF16), wg) + bg
    return (jnp.minimum(x, 0.0) - jnp.log1p(jnp.exp(-jnp.abs(x)))) * (1.0 / GLA_TAU)


def _pack_w_gate(w_gate_up):
    pad = jnp.zeros((_SMALL_W - GLA_GATE_RANK, GLA_HEADS * GLA_DK), w_gate_up.dtype)
    return jnp.concatenate([w_gate_up, pad], axis=0).astype(BF16)


def _gla_out(o, gn, r):
    o = o * lax.rsqrt(jnp.mean(o * o, axis=-1, keepdims=True) + EPS) * gn
    return (o * (r * _sigmoid(r))).astype(BF16)


def _gla_chunk_kernel(q_ref, k_ref, v_ref, r_ref, small_ref, wg_ref, bg_ref, gn_ref, og_ref, st_ref,
                      s_scr, kp_scr, bp_scr, *, chunk):
    c = pl.program_id(1)
    SUB, DK, DV = GLA_SUB, GLA_DK, GLA_DV

    @pl.when(c == 0)
    def _():
        s_scr[...] = jnp.zeros_like(s_scr)
        kp_scr[:, 0:SUB, :] = jnp.zeros((GLA_HEADS, SUB, DK), F32)
        bp_scr[:, 0:SUB, :] = jnp.zeros((GLA_HEADS, SUB, DK), F32)

    small = small_ref[...]
    for h in range(GLA_HEADS):
        kh = slice(h * DK, (h + 1) * DK)
        vh = slice(h * DV, (h + 1) * DV)
        la = _log_decay(small, wg_ref[:, kh], bg_ref[:, kh])
        o, st_new = _gla_head_chunk(q_ref[:, kh] * (DK ** -0.5), k_ref[:, kh], v_ref[:, vh].astype(BF16), la,
                                    s_scr[h], kp_scr.at[h], bp_scr.at[h], chunk)
        s_scr[h] = st_new
        og_ref[:, vh] = _gla_out(o, gn_ref[:, vh], r_ref[:, vh])

    @pl.when(c == pl.num_programs(1) - 1)
    def _():
        for h in range(GLA_HEADS):
            st_ref[0, h] = s_scr[h].T


def _gla_head_chunk(q, k, vb, la, st, kp_ref, bp_ref, chunk):
    C, SUB = chunk, GLA_SUB
    ri = lax.broadcasted_iota(jnp.int32, (C, C), 0)
    ci = lax.broadcasted_iota(jnp.int32, (C, C), 1)
    tri = jnp.where(ci <= ri, 1.0, 0.0).astype(BF16)
    hi = la.astype(BF16)
    r1 = la - hi.astype(F32)
    mid = r1.astype(BF16)
    lo = (r1 - mid.astype(F32)).astype(BF16)
    b = _dot(tri, hi) + _dot(tri, mid) + _dot(tri, lo)

    o = _dot_nt((q * jnp.exp(b)).astype(BF16), st.astype(BF16))
    b_last = b[C - 1:C, :]
    kdec = (k * jnp.exp(b_last - b)).astype(BF16)
    st_new = st * jnp.exp(b_last) + _dot_tn(vb, kdec)

    row = lax.broadcasted_iota(jnp.int32, (C, GLA_DK), 0)
    blocks = [jnp.zeros((SUB, C), F32)]
    for i in range(1, C // SUB):
        beta = b[SUB * i - 1:SUB * i, :]
        qi = (q[SUB * i:SUB * (i + 1)] * jnp.exp(b[SUB * i:SUB * (i + 1)] - beta)).astype(BF16)
        ki = (k * jnp.exp(jnp.where(row < SUB * i, beta - b, -jnp.inf))).astype(BF16)
        blocks.append(_dot_nt(qi, ki))
    a = jnp.concatenate(blocks, axis=0)

    kp_ref[SUB:SUB + C, :] = k
    bp_ref[SUB:SUB + C, :] = b
    tmod = ri % SUB
    for j in range(SUB):
        ks = kp_ref[SUB - j:SUB - j + C, :]
        bs = bp_ref[SUB - j:SUB - j + C, :]
        aj = jnp.sum(q * ks * jnp.exp(b - bs), axis=-1, keepdims=True)
        a = a + jnp.where((ci == ri - j) & (tmod >= j), aj, 0.0)
    return o + _dot(a.astype(BF16), vb), st_new


def _gla_prompt(gla, small, w_gate_pack, b_gate, g_norm, batch, seq, chunk=128):
    n = seq // chunk
    assert seq % chunk == 0 and chunk % GLA_SUB == 0
    H = GLA_HEADS
    kw, vw = H * GLA_DK, H * GLA_DV
    assert vw == 2 * kw
    kern = functools.partial(_gla_chunk_kernel, chunk=chunk)
    return pl.pallas_call(
        kern,
        out_shape=(jax.ShapeDtypeStruct((batch * seq, vw), BF16),
                   jax.ShapeDtypeStruct((batch, H, GLA_DK, GLA_DV), F32)),
        grid=(batch, n),
        in_specs=[
            pl.BlockSpec((chunk, kw), lambda b, c: (b * n + c, 0)),
            pl.BlockSpec((chunk, kw), lambda b, c: (b * n + c, 1)),
            pl.BlockSpec((chunk, vw), lambda b, c: (b * n + c, 1)),
            pl.BlockSpec((chunk, vw), lambda b, c: (b * n + c, 2)),
            pl.BlockSpec((chunk, _SMALL_W), lambda b, c: (b * n + c, 0)),
            _const_spec((_SMALL_W, kw)), _const_spec((1, kw)), _const_spec((1, vw)),
        ],
        out_specs=(pl.BlockSpec((chunk, vw), lambda b, c: (b * n + c, 0)),
                   pl.BlockSpec((1, H, GLA_DK, GLA_DV), lambda b, c: (b, 0, 0, 0))),
        scratch_shapes=[pltpu.VMEM((H, GLA_DV, GLA_DK), F32),
                        pltpu.VMEM((H, GLA_SUB + chunk, GLA_DK), F32),
                        pltpu.VMEM((H, GLA_SUB + chunk, GLA_DK), F32)],
        compiler_params=_cparams("parallel", "arbitrary"), name="gla_chunk",
    )(gla, gla, gla, gla, small, w_gate_pack, b_gate.reshape(1, -1), g_norm.reshape(1, -1))


def _pack_w_cmp1(w1):
    w = w1.reshape(2, 2, CMP_STRIDE, NSA_DH, CMP_HIDDEN)
    per_c = jnp.concatenate([w[:, 0], w[:, 1]], axis=-1)
    z = jnp.zeros_like(per_c[0])
    w16 = jnp.concatenate([jnp.concatenate([per_c[0], z], axis=-1),
                           jnp.concatenate([z, per_c[1]], axis=-1)], axis=1)
    return w16.reshape(CMP_STRIDE // 2, 4 * NSA_DH, 4 * CMP_HIDDEN).astype(BF16)


def _compress_kernel(pt_ref, kv_hbm, wp_ref, w1_ref, pef_ref, b1_ref, w2_ref, b2_ref, out_ref,
                     buf, tok_scr, hb_scr, sem, *, n_pages, pages_per_row):
    bidx = pl.program_id(0)
    slot = bidx % 2
    n_chunk = n_pages * (PAGE_SIZE // CMP_STRIDE)
    GW = 2 * NSA_DH

    def page_copy(seq, p, slt):
        page = pt_ref[seq, p]
        off = pl.multiple_of((page % pages_per_row) * PAGE_SIZE, PAGE_SIZE)
        src = kv_hbm.at[page // pages_per_row, :, :, :, pl.ds(off, PAGE_SIZE)]
        dst = buf.at[slt, :, :, :, pl.ds(pl.multiple_of(p * PAGE_SIZE, PAGE_SIZE), PAGE_SIZE)]
        return pltpu.make_async_copy(src, dst, sem.at[slt])

    def start_pages(seq, slt):
        def start(p, carry):
            page_copy(seq, p, slt).start()
            return carry
        lax.fori_loop(0, n_pages, start, 0)

    @pl.when(bidx == 0)
    def _():
        start_pages(0, 0)

    @pl.when(bidx + 1 < pl.num_programs(0))
    def _():
        start_pages(bidx + 1, 1 - slot)

    def wait(p, carry):
        page_copy(bidx, p, slot).wait()
        return carry

    lax.fori_loop(0, n_pages, wait, 0)

    HID = CMP_HIDDEN
    hb_scr[n_chunk:n_chunk + 8, :] = jnp.zeros((8, HID), F32)
    const = [_dot(pef_ref[c].astype(BF16), w1_ref[c]) + b1_ref[c] for c in range(2)]
    def to_token_major(g, half):
        for p in range(n_pages):
            ps = slice(p * PAGE_SIZE, (p + 1) * PAGE_SIZE)
            tok_scr[half, ps, :] = buf[slot, g, :, :, ps].reshape(GW, PAGE_SIZE).T

    to_token_major(0, 0)
    for g in range(NSA_KV_GROUPS):
        if g + 1 < NSA_KV_GROUPS:
            to_token_major(g + 1, (g + 1) % 2)
        acc = jnp.zeros((n_chunk, 4 * HID), F32)
        for p in range(0, CMP_STRIDE, 2):
            xs = [tok_scr[g % 2, pl.ds(p + i, n_chunk, stride=CMP_STRIDE), :].astype(BF16) for i in range(2)]
            acc = acc + _dot(jnp.concatenate(xs, axis=1), wp_ref[p // 2])
        for c in range(2):
            hb_scr[0:n_chunk, :] = acc[:, (2 * c + 1) * HID:(2 * c + 2) * HID]
            hid = acc[:, 2 * c * HID:(2 * c + 1) * HID] + hb_scr[pl.ds(1, n_chunk), :] + const[c]
            out_ref[0, g * 2 + c] = _dot(_gelu_tanh(hid).astype(BF16), w2_ref[c]) + b2_ref[c]


def _compress(page_table, kv_t, wp, w1b, pe_flat, b1, w2b, b2):
    batch, n_pages = page_table.shape
    n_chunk = n_pages * (PAGE_SIZE // CMP_STRIDE)
    assert kv_t.shape[-1] % PAGE_SIZE == 0
    kern = functools.partial(_compress_kernel, n_pages=n_pages, pages_per_row=kv_t.shape[-1] // PAGE_SIZE)
    grid_spec = pltpu.PrefetchScalarGridSpec(
        num_scalar_prefetch=1, grid=(batch,),
        in_specs=[pl.BlockSpec(memory_space=pl.ANY),
                  _const_spec((CMP_STRIDE // 2, 4 * NSA_DH, 4 * CMP_HIDDEN)),
                  _const_spec((2, CMP_LEN * NSA_DH, CMP_HIDDEN)),
                  _const_spec((2, 1, CMP_LEN * NSA_DH)),
                  _const_spec((2, 1, CMP_HIDDEN)),
                  _const_spec((2, CMP_HIDDEN, NSA_DH)),
                  _const_spec((2, 1, NSA_DH))],
        out_specs=pl.BlockSpec((1, 2 * NSA_KV_GROUPS, n_chunk, NSA_DH), lambda b, pt: (b, 0, 0, 0)),
        scratch_shapes=[pltpu.VMEM((2, NSA_KV_GROUPS, 2, NSA_DH, n_pages * PAGE_SIZE), F32),
                        pltpu.VMEM((2, n_pages * PAGE_SIZE, 2 * NSA_DH), F32),
                        pltpu.VMEM((n_chunk + 8, CMP_HIDDEN), F32),
                        pltpu.SemaphoreType.DMA((2,))])
    return pl.pallas_call(
        kern, out_shape=jax.ShapeDtypeStruct((batch, 2 * NSA_KV_GROUPS, n_chunk, NSA_DH), F32),
        grid_spec=grid_spec, compiler_params=_cparams("arbitrary"), name="compress",
    )(page_table, kv_t, wp, w1b, pe_flat, b1, w2b, b2)


def _feature_major(kv):
    return kv.transpose(0, 2, 3, 4, 1)


def _token_major(kv_t, batch, seq):
    return kv_t.reshape(batch, NSA_KV_GROUPS, 2, NSA_DH, seq).transpose(0, 4, 1, 2, 3)


def _permute_cmp(kvc):
    b, gc, n, d = kvc.shape
    return kvc.reshape(b, gc, n // 4, 4, d).transpose(0, 1, 3, 2, 4).reshape(b, gc, n, d)


def _group_queries(qall):
    return jnp.concatenate([qall[:, r * NSA_DH:(r + 1) * NSA_DH] for r in range(NSA_GROUP_HEADS)], axis=0)


def _block_importance(psum, n_slc):
    p0, p1, p2, p3 = (psum[:, m * n_slc:(m + 1) * n_slc] for m in range(4))
    lane = lax.broadcasted_iota(jnp.int32, p3.shape, 1)
    p3s = jnp.where(lane == 0, 0.0, pltpu.roll(p3, 1, axis=1))
    return ((((((p0 + p3s) + p1) + p0) + p2) + p1) + p3) + p2


def _top_blocks_t(score_t, count):
    blk = lax.broadcasted_iota(jnp.int32, score_t.shape, 0).astype(F32)
    n = float(score_t.shape[0])
    finite = score_t > -jnp.inf
    picked = jnp.zeros(score_t.shape, F32)
    for _ in range(count):
        mx = jnp.max(score_t, axis=0, keepdims=True)
        idx = jnp.min(jnp.where(score_t == mx, blk, n), axis=0, keepdims=True)
        pick = blk == idx
        picked = jnp.where(pick, 1.0, picked)
        score_t = jnp.where(pick, -jnp.inf, score_t)
    return jnp.where(finite, picked, 0.0)


_N_FORCED = 3


def _cmp_topk_kernel(q_ref, kc_ref, vc_ref, small_ref, oc_ref, sel_ref, imp_scr, *, tq, sub, n_slc):
    g = pl.program_id(1)
    q0 = pl.program_id(2) * tq
    rows = NSA_GROUP_HEADS * sub
    n_cmp = 4 * n_slc
    kc = kc_ref[0, 0].astype(BF16)
    vc = vc_ref[0, 0].astype(BF16)
    lane = lax.broadcasted_iota(jnp.int32, (1, n_cmp), 1)
    blk_end = (4 * (lane % n_slc) + lane // n_slc) * CMP_STRIDE + (CMP_LEN - 1)
    for j in range(tq // sub):
        js = slice(j * sub, (j + 1) * sub)
        q = _group_queries(q_ref[js, :])
        s = _dot_nt(q, kc)
        t_rows = (q0 + j * sub) + lax.broadcasted_iota(jnp.int32, (rows, 1), 0) % sub
        p = _masked_softmax(s, blk_end <= t_rows)
        oc = _dot(p.astype(BF16), vc)
        psum = ((p[0:sub] + p[sub:2 * sub]) + p[2 * sub:3 * sub]) + p[3 * sub:4 * sub]
        imp_scr[js, :] = _block_importance(psum, n_slc)
        sig = _sigmoid(small_ref[js, :])
        outs = []
        for r in range(NSA_GROUP_HEADS):
            gate = _lane_column(sig, _NG_LANE0 + (g * NSA_GROUP_HEADS + r) * 3)
            outs.append(gate * oc[r * sub:(r + 1) * sub])
        oc_ref[js, :] = jnp.concatenate(outs, axis=1)

    cur = (q0 + lax.broadcasted_iota(jnp.int32, (1, tq), 1)) // SLC_BLOCK
    blk = lax.broadcasted_iota(jnp.int32, (n_slc, 1), 0)
    forced = (blk == 0) | (blk == cur) | (blk == cur - 1)
    valid = blk <= cur
    score_t = jnp.where(valid & jnp.logical_not(forced), imp_scr[...].T, -jnp.inf)
    sel_t = jnp.where(valid & forced, 1.0, _top_blocks_t(score_t, min(SLC_TOPK, n_slc) - _N_FORCED))
    sel_ref[0] = (sel_t.T - 1.0).astype(BF16)


def _cmp_topk_prompt(nq, kvc_perm, small, batch, seq, tq=512, sub=128):
    G = NSA_KV_GROUPS
    n_slc = seq // SLC_BLOCK
    assert kvc_perm.shape[2] == 4 * n_slc and seq % tq == 0 and tq % sub == 0 and n_slc >= _N_FORCED
    nqt = seq // tq
    gw = NSA_GROUP_HEADS * NSA_DH
    kern = functools.partial(_cmp_topk_kernel, tq=tq, sub=sub, n_slc=n_slc)
    return pl.pallas_call(
        kern,
        out_shape=(jax.ShapeDtypeStruct((batch * seq, NSA_HEADS * NSA_DH), F32),
                   jax.ShapeDtypeStruct((batch * G, seq, n_slc), BF16)),
        grid=(batch, G, nqt),
        in_specs=[pl.BlockSpec((tq, gw), lambda b, g, i: (b * nqt + i, g)),
                  pl.BlockSpec((1, 1, 4 * n_slc, NSA_DH), lambda b, g, i: (b, 2 * g, 0, 0)),
                  pl.BlockSpec((1, 1, 4 * n_slc, NSA_DH), lambda b, g, i: (b, 2 * g + 1, 0, 0)),
                  pl.BlockSpec((tq, _SMALL_W), lambda b, g, i: (b * nqt + i, 0))],
        out_specs=(pl.BlockSpec((tq, gw), lambda b, g, i: (b * nqt + i, g)),
                   pl.BlockSpec((1, tq, n_slc), lambda b, g, i: (b * G + g, i, 0))),
        scratch_shapes=[pltpu.VMEM((tq, n_slc), F32)],
        compiler_params=_cparams("parallel", "parallel", "parallel"), name="cmp_topk",
    )(nq, kvc_perm, kvc_perm, small)


def _slc_win_kernel(q_ref, sel_ref, ks_ref, vs_ref, kw_ref, vw_ref, small_ref, oc_ref, on_ref, *, tq, tk, n_slc):
    g = pl.program_id(1)
    q0 = pl.program_id(2) * tq
    rows = NSA_GROUP_HEADS * tq
    DH = NSA_DH
    q = _group_queries(q_ref[...])
    qpad = jnp.concatenate([q, jnp.zeros((rows, DH), BF16)], axis=1)
    sel4 = jnp.concatenate([sel_ref[0]] * NSA_GROUP_HEADS, axis=0)
    qaug = jnp.concatenate([qpad, sel4], axis=1)
    t_rows = q0 + lax.broadcasted_iota(jnp.int32, (rows, 1), 0) % tq
    tok_l = lax.broadcasted_iota(jnp.int32, (1, tk), 1)

    def tile(kt, m, acc, causal):
        k0 = pl.multiple_of(kt * tk, tk)
        s = _dot(qaug, ks_ref[0, :, pl.ds(k0, tk)])
        if causal:
            s = jnp.where(k0 + tok_l <= t_rows, s, -_MASK_BIG)
        m_new = jnp.maximum(m, jnp.max(s, axis=-1, keepdims=True))
        p = jnp.exp2(s - m_new)
        acc = jnp.exp2(m - m_new) * acc + _dot_nt(p.astype(BF16), vs_ref[0, :, pl.ds(k0, tk)])
        return m_new, acc

    unroll = 4

    def tiles(i, carry):
        for u in range(unroll):
            carry = tile(unroll * i + u, *carry, False)
        return carry

    last = (q0 + tq - 1) // tk
    carry = (jnp.full((rows, 1), -1e30, F32), jnp.zeros((rows, 2 * DH), F32))
    carry = lax.fori_loop(0, last // unroll, tiles, carry)
    m, acc = lax.fori_loop(unroll * (last // unroll), last + 1, lambda kt, c: tile(kt, *c, True), carry)
    o_s = acc[:, DH:] / acc[:, 0:DH]

    wlen = WINDOW + tq
    w0 = pl.multiple_of(jnp.maximum(q0 - WINDOW, 0), tq)
    s_w = _dot(qpad, kw_ref[0, :, pl.ds(w0, wlen)])
    age = t_rows - (w0 + lax.broadcasted_iota(jnp.int32, (1, wlen), 1))
    s_w = jnp.where(age >= 0, jnp.where(age < WINDOW, s_w, -_MASK_BIG), -_MASK_BIG)
    e_w = jnp.exp2(s_w - jnp.max(s_w, axis=-1, keepdims=True))
    acc_w = _dot_nt(e_w.astype(BF16), vw_ref[0, :, pl.ds(w0, wlen)])
    o_w = acc_w[:, DH:] / acc_w[:, 0:DH]

    sig = _sigmoid(small_ref[...])
    oc = oc_ref[...]
    outs = []
    for r in range(NSA_GROUP_HEADS):
        lane0 = _NG_LANE0 + (g * NSA_GROUP_HEADS + r) * 3
        g_s = _lane_column(sig, lane0 + 1)
        g_w = _lane_column(sig, lane0 + 2)
        rs = slice(r * tq, (r + 1) * tq)
        outs.append(oc[:, r * DH:(r + 1) * DH] + g_s * o_s[rs] + g_w * o_w[rs])
    on_ref[...] = jnp.concatenate(outs, axis=1).astype(BF16)


def _slc_win_prompt(nq, selm, att, small, oc, batch, seq, tq=256, tk=512):
    G = NSA_KV_GROUPS
    n_slc = seq // SLC_BLOCK
    assert seq % tk == 0 and seq % tq == 0 and tk % tq == 0 and seq >= WINDOW + tq
    nqt = seq // tq
    gw = NSA_GROUP_HEADS * NSA_DH
    kvw = 2 * NSA_DH
    kern = functools.partial(_slc_win_kernel, tq=tq, tk=tk, n_slc=n_slc)
    return pl.pallas_call(
        kern, out_shape=jax.ShapeDtypeStruct((batch * seq, NSA_HEADS * NSA_DH), BF16),
        grid=(batch, G, nqt),
        in_specs=[pl.BlockSpec((tq, gw), lambda b, g, i: (b * nqt + i, g)),
                  pl.BlockSpec((1, tq, n_slc), lambda b, g, i: (b * G + g, i, 0)),
                  pl.BlockSpec((1, kvw + n_slc, seq), lambda b, g, i: (b, g, 0)),
                  pl.BlockSpec((1, kvw, seq), lambda b, g, i: (b, g, 0)),
                  pl.BlockSpec((1, kvw, seq), lambda b, g, i: (b, g, 0)),
                  pl.BlockSpec((1, kvw, seq), lambda b, g, i: (b, g, 0)),
                  pl.BlockSpec((tq, _SMALL_W), lambda b, g, i: (b * nqt + i, 0)),
                  pl.BlockSpec((tq, gw), lambda b, g, i: (b * nqt + i, g))],
        out_specs=pl.BlockSpec((tq, gw), lambda b, g, i: (b * nqt + i, g)),
        compiler_params=_cparams("parallel", "parallel", "parallel"), name="slc_win",
    )(nq, selm, *att, small, oc)


def _mix_out_kernel(og_ref, on_ref, mg_ref, x_ref, wgo_ref, wno_ref, wo_ref, gpost_ref, y_ref):
    yg = _dot(og_ref[...], wgo_ref[...])
    yn = _dot(on_ref[...], wno_ref[...])
    mg = mg_ref[...]
    mix = _sigmoid(mg[:, 0:D_MODEL]) * yg + _sigmoid(mg[:, D_MODEL:]) * yn
    z = _dot(mix.astype(BF16), wo_ref[...])
    y_ref[...] = x_ref[...] + _rms(z, gpost_ref[...])


def _mix_out(og, on, mg, x, wgo, wno, wo, g_post, tm):
    m = x.shape[0]
    row = lambda w: pl.BlockSpec((tm, w), lambda i: (i, 0))
    wspec = _const_spec((D_MODEL, D_MODEL))
    return pl.pallas_call(
        _mix_out_kernel, out_shape=jax.ShapeDtypeStruct((m, D_MODEL), F32), grid=(m // tm,),
        in_specs=[row(D_MODEL), row(D_MODEL), row(2 * D_MODEL), row(D_MODEL), wspec, wspec, wspec,
                  _const_spec((1, D_MODEL))],
        out_specs=row(D_MODEL), compiler_params=_cparams("parallel"), name="mix_out",
    )(og, on, mg, x, wgo, wno, wo, g_post.reshape(1, D_MODEL))


_FFN_HALO = 16


def _ffn_tail(x, conv, bgate, p, wdn_ref, gpost_ref, gple_ref, wpg_ref, wpp_ref):
    y = _dot((_gelu_tanh(conv) * bgate).astype(BF16), wdn_ref[...])
    x2 = x + _rms(y, gpost_ref[...])
    gate = _sigmoid(_dot(_rms(x2, gple_ref[...]).astype(BF16), wpg_ref[...]))
    return x2 + gate * _dot(p.astype(BF16), wpp_ref[...])


def _ffn_prompt_kernel(x_ref, xh_ref, p_ref, gpre_ref, wup_ref, cw_ref, cb_ref, wdn_ref, gpost_ref, gple_ref,
                       wpg_ref, wpp_ref, y_ref, alast_ref, h_scr, a_scr, *, tm, tiles_per_seq):
    H = _FFN_HALO
    x = x_ref[...]
    h_scr[0:H, :] = _rms(xh_ref[...], gpre_ref[...]).astype(BF16)
    h_scr[H:H + tm, :] = _rms(x, gpre_ref[...]).astype(BF16)
    ab = _dot(h_scr[...], wup_ref[...])
    keep = jnp.where(pl.program_id(0) % tiles_per_seq == 0, 0.0, 1.0)
    a_scr[0:H, :] = ab[0:H, 0:D_FF] * keep
    a_scr[H:H + tm, :] = ab[H:, 0:D_FF]
    cw = cw_ref[...]
    conv = (a_scr[H - 2:H - 2 + tm, :] * cw[0:1] + a_scr[H - 1:H - 1 + tm, :] * cw[1:2]
            + a_scr[H:H + tm, :] * cw[2:3]) + cb_ref[...]
    y_ref[...] = _ffn_tail(x, conv, ab[H:, D_FF:], p_ref[...], wdn_ref, gpost_ref, gple_ref, wpg_ref, wpp_ref)
    alast_ref[0] = a_scr[H + tm - 8:H + tm, :]


def _ffn_weight_specs():
    return [_const_spec((1, D_MODEL)), _const_spec((D_MODEL, 2 * D_FF)), _const_spec((CONV_W, D_FF)),
            _const_spec((1, D_FF)), _const_spec((D_FF, D_MODEL)), _const_spec((1, D_MODEL)),
            _const_spec((1, D_MODEL)), _const_spec((D_MODEL, D_MODEL)), _const_spec((PLE_DIM, D_MODEL))]


def _ffn_prompt(x, p, fw, batch, seq, tm=256):
    assert seq % tm == 0 and tm % _FFN_HALO == 0
    tps = seq // tm
    hb = tm // _FFN_HALO
    kern = functools.partial(_ffn_prompt_kernel, tm=tm, tiles_per_seq=tps)
    return pl.pallas_call(
        kern,
        out_shape=(jax.ShapeDtypeStruct((batch * seq, D_MODEL), F32), jax.ShapeDtypeStruct((batch, 8, D_FF), F32)),
        grid=(batch * tps,),
        in_specs=[pl.BlockSpec((tm, D_MODEL), lambda i: (i, 0)),
                  pl.BlockSpec((_FFN_HALO, D_MODEL), lambda i: (jnp.maximum(i * hb - 1, 0), 0)),
                  pl.BlockSpec((tm, PLE_DIM), lambda i: (i, 0))] + _ffn_weight_specs(),
        out_specs=(pl.BlockSpec((tm, D_MODEL), lambda i: (i, 0)),
                   pl.BlockSpec((1, 8, D_FF), lambda i: (i // tps, 0, 0))),
        scratch_shapes=[pltpu.VMEM((_FFN_HALO + tm, D_MODEL), BF16), pltpu.VMEM((_FFN_HALO + tm, D_FF), F32)],
        compiler_params=_cparams("arbitrary"), name="ffn_prompt",
    )(x, x, p, *fw)


def _layer_prompt(x, p, w, tm=256):
    batch, seq, _ = x.shape
    m = batch * seq
    x2 = x.reshape(m, D_MODEL)
    gla, small, nq, kc_t, ks_t, kw_t, *att, mg = _proj_in(x2, w["g_mix_pre"], w["w_in_pack"], tm, w["w_kv_t"], seq)
    og, gla_state = _gla_prompt(gla, small, w["w_gate_pack"], w["b_gla_gate"], w["g_gla_norm"], batch, seq)

    n_pages = seq // PAGE_SIZE
    ident_pt = jnp.arange(batch * n_pages, dtype=jnp.int32).reshape(batch, n_pages)
    kvc = _compress(ident_pt, kc_t.reshape(batch, NSA_KV_GROUPS, 2, NSA_DH, seq), *w["cmp"])
    oc, selm = _cmp_topk_prompt(nq, _permute_cmp(kvc), small, batch, seq)
    on = _slc_win_prompt(nq, selm, att, small, oc, batch, seq)

    x1 = _mix_out(og, on, mg, x2, w["w_gla_out"], w["w_nsa_out"], w["w_out"], w["g_mix_post"], tm)
    y, alast = _ffn_prompt(x1, p.reshape(m, PLE_DIM), w["ffn"], batch, seq, tm)

    win = min(WINDOW, seq)
    return (y.reshape(batch, seq, D_MODEL), _token_major(kc_t, batch, seq), _token_major(ks_t, batch, seq),
            _token_major(kw_t[:, :, seq - win:], batch, win), gla_state, alast[:, 8 - (CONV_W - 1):])


def _row_to_col(row):
    n = row.shape[-1]
    ri = lax.broadcasted_iota(jnp.int32, (n, n), 0)
    ci = lax.broadcasted_iota(jnp.int32, (n, n), 1)
    return jnp.sum(jnp.where(ri == ci, jnp.broadcast_to(row, (n, n)), 0.0), axis=-1, keepdims=True)


def _gla_step_kernel(gla_ref, small_ref, s0_ref, wg_ref, bg_ref, gn_ref, og_ref, s1_ref):
    row = gla_ref[0]
    small = small_ref[0]
    H, DK, DV = GLA_HEADS, GLA_DK, GLA_DV
    outs = []
    for h in range(H):
        q = row[:, h * DK:(h + 1) * DK] * (DK ** -0.5)
        k = row[:, H * DK + h * DK:H * DK + (h + 1) * DK]
        v = row[:, 2 * H * DK + h * DV:2 * H * DK + (h + 1) * DV]
        r = row[:, 2 * H * DK + H * DV + h * DV:2 * H * DK + H * DV + (h + 1) * DV]
        alpha = jnp.exp(_log_decay(small, wg_ref[:, h * DK:(h + 1) * DK], bg_ref[:, h * DK:(h + 1) * DK]))
        s0 = s0_ref[0, h]
        s1_ref[0, h] = _row_to_col(alpha) * s0 + _row_to_col(k) * v
        o = _dot((q * alpha).astype(BF16), s0.astype(BF16)) + jnp.sum(q * k, axis=-1, keepdims=True) * v
        outs.append(_gla_out(o, gn_ref[:, h * DV:(h + 1) * DV], r))
    og_ref[0] = jnp.concatenate(outs, axis=1)


def _gla_sample(gla, small, state, w_gate_pack, b_gate, g_norm):
    nb = gla.shape[0]
    H = GLA_HEADS
    og, s1 = pl.pallas_call(
        _gla_step_kernel,
        out_shape=(jax.ShapeDtypeStruct((nb, 1, H * GLA_DV), BF16), jax.ShapeDtypeStruct(state.shape, F32)),
        grid=(nb,),
        in_specs=[pl.BlockSpec((1, 1, _GLA_W), lambda b: (b, 0, 0)),
                  pl.BlockSpec((1, 1, _SMALL_W), lambda b: (b, 0, 0)),
                  pl.BlockSpec((1, H, GLA_DK, GLA_DV), lambda b: (b, 0, 0, 0)),
                  _const_spec((_SMALL_W, H * GLA_DK)), _const_spec((1, H * GLA_DK)), _const_spec((1, H * GLA_DV))],
        out_specs=(pl.BlockSpec((1, 1, H * GLA_DV), lambda b: (b, 0, 0)),
                   pl.BlockSpec((1, H, GLA_DK, GLA_DV), lambda b: (b, 0, 0, 0))),
        compiler_params=_cparams("parallel"), name="gla_step",
    )(gla.reshape(nb, 1, _GLA_W), small.reshape(nb, 1, _SMALL_W), state, w_gate_pack,
      b_gate.reshape(1, -1), g_norm.reshape(1, -1))
    return og.reshape(nb, H * GLA_DV), s1


def _head_group_rows(x_groups):
    grp = lax.broadcasted_iota(jnp.int32, x_groups[0].shape, 0) // NSA_GROUP_HEADS
    out = jnp.zeros(x_groups[0].shape, F32)
    for g, xg in enumerate(x_groups):
        out = jnp.where(grp == g, xg, out)
    return out


def _cmp_topk_sample_kernel(q_ref, kvc_ref, oc_ref, idx_ref, *, t, n_lane, n_pick):
    q = q_ref[0]
    n_cmp = kvc_ref.shape[2]
    lane = lax.broadcasted_iota(jnp.int32, (1, n_cmp), 1)
    blk_end = (4 * (lane % n_lane) + lane // n_lane) * CMP_STRIDE + (CMP_LEN - 1)
    ocs, imps = [], []
    for g in range(NSA_KV_GROUPS):
        s = _dot_nt(q, kvc_ref[0, 2 * g].astype(BF16))
        p = _masked_softmax(s, jnp.broadcast_to(blk_end <= t, s.shape))
        ocs.append(_dot(p.astype(BF16), kvc_ref[0, 2 * g + 1].astype(BF16)))
        r0 = NSA_GROUP_HEADS * g
        psum = ((p[r0:r0 + 1] + p[r0 + 1:r0 + 2]) + p[r0 + 2:r0 + 3]) + p[r0 + 3:r0 + 4]
        imps.append(_block_importance(psum, n_lane))
    oc_ref[0] = _head_group_rows(ocs)

    imp = jnp.concatenate(imps + [jnp.zeros((8 - NSA_KV_GROUPS, n_lane), F32)], axis=0)
    blk = lax.broadcasted_iota(jnp.int32, imp.shape, 1)
    cur = t // SLC_BLOCK
    forced = (blk == 0) | (blk == cur) | (blk == cur - 1)
    score = jnp.where(blk <= cur, jnp.where(forced, FORCE_SCORE, imp), -jnp.inf)
    idx_out = jnp.zeros(imp.shape, jnp.int32)
    for i in range(n_pick):
        mx = jnp.max(score, axis=-1, keepdims=True)
        idx = jnp.min(jnp.where(score == mx, blk, n_lane), axis=-1, keepdims=True)
        idx_out = jnp.where(blk == i, idx, idx_out)
        score = jnp.where(blk == idx, -jnp.inf, score)
    idx_ref[0] = idx_out


def _attend_with_new_token(qpad, qf, kv, valid, new_row):
    DH = NSA_DH
    is_k = lax.broadcasted_iota(jnp.int32, kv.shape, 0) < DH
    s = jnp.where(valid, _dot(qpad, jnp.where(is_k, kv, jnp.zeros_like(kv))), -jnp.inf)
    s_n = jnp.sum(qf * new_row[:, 0:DH].astype(BF16).astype(F32), axis=-1, keepdims=True)
    m = jnp.maximum(jnp.max(s, axis=-1, keepdims=True), s_n)
    e = jnp.exp2(s - m)
    en = jnp.exp2(s_n - m)
    num = _dot_nt(e.astype(BF16), kv)[:, DH:] + en * new_row[:, DH:]
    return num / (jnp.sum(e, axis=-1, keepdims=True) + en)


def _slc_win_sample_kernel(idx_ref, pt_ref, *refs, t, win, n_pick):
    G = NSA_KV_GROUPS
    GW = 2 * NSA_DH
    kv_refs = refs[:G * n_pick]
    q_ref, ksn_ref, kwn_ref, cw_ref, small_ref, oc_ref, on_ref = refs[G * n_pick:]
    b = pl.program_id(0)
    q = q_ref[0]
    qf = q.astype(F32)
    qpad = jnp.concatenate([q, jnp.zeros_like(q)], axis=1)
    ksn = ksn_ref[0]
    kwn = kwn_ref[0]
    blocks_per_page = PAGE_SIZE // SLC_BLOCK
    lane_blk = (lax.broadcasted_iota(jnp.int32, (1, n_pick * PAGE_SIZE), 1) // SLC_BLOCK) % blocks_per_page
    pos = (t - win) + lax.broadcasted_iota(jnp.int32, (1, win), 1)
    w_valid = (pos <= t) & (pos > t - WINDOW) & (pos >= 0)
    o_s, o_w = [], []
    for g in range(G):
        ks = range(g * n_pick, (g + 1) * n_pick)
        kv = jnp.concatenate([kv_refs[i][0, 0].reshape(GW, PAGE_SIZE) for i in ks], axis=1).astype(BF16)
        want = jnp.concatenate([jnp.full((1, PAGE_SIZE), idx_ref[b, i] % blocks_per_page, jnp.int32) for i in ks], axis=1)
        o_s.append(_attend_with_new_token(qpad, qf, kv, lane_blk == want, ksn[:, g * GW:(g + 1) * GW]))
        kvw = cw_ref[0, g].reshape(GW, win).astype(BF16)
        o_w.append(_attend_with_new_token(qpad, qf, kvw, w_valid, kwn[:, g * GW:(g + 1) * GW]))
    o_s = _head_group_rows(o_s)
    o_w = _head_group_rows(o_w)
    sig = jnp.broadcast_to(_sigmoid(small_ref[0]), (NSA_HEADS, _SMALL_W))
    lane = lax.broadcasted_iota(jnp.int32, sig.shape, 1)
    head = lax.broadcasted_iota(jnp.int32, sig.shape, 0)
    gate = [jnp.sum(jnp.where(lane == _NG_LANE0 + 3 * head + x, sig, 0.0), axis=-1, keepdims=True) for x in range(3)]
    on_ref[0] = (gate[0] * oc_ref[0] + gate[1] * o_s + gate[2] * o_w).astype(BF16)


def _nsa_sample(nq, ks_new, kw_new, small, kvc_perm, cache_slc, cache_win, page_table):
    nb, n_pages = page_table.shape
    past = n_pages * PAGE_SIZE
    t = past
    n_lane = past // SLC_BLOCK
    n_pick = SLC_TOPK - 1
    assert kvc_perm.shape[2] == 4 * n_lane and n_lane >= SLC_TOPK and n_lane % 128 == 0
    win = cache_win.shape[1]
    q3 = nq.reshape(nb, NSA_HEADS, NSA_DH)
    G = NSA_KV_GROUPS

    oc, idx = pl.pallas_call(
        functools.partial(_cmp_topk_sample_kernel, t=t, n_lane=n_lane, n_pick=n_pick),
        out_shape=(jax.ShapeDtypeStruct((nb, NSA_HEADS, NSA_DH), F32), jax.ShapeDtypeStruct((nb, 8, n_lane), jnp.int32)),
        grid=(nb,),
        in_specs=[pl.BlockSpec((1, NSA_HEADS, NSA_DH), lambda b: (b, 0, 0)),
                  pl.BlockSpec((1, 2 * G, 4 * n_lane, NSA_DH), lambda b: (b, 0, 0, 0))],
        out_specs=(pl.BlockSpec((1, NSA_HEADS, NSA_DH), lambda b: (b, 0, 0)),
                   pl.BlockSpec((1, 8, n_lane), lambda b: (b, 0, 0))),
        compiler_params=_cparams("parallel"), name="cmp_topk_sample",
    )(q3, kvc_perm)
    blk_idx = idx[:, :G, :n_pick].reshape(nb, G * n_pick)

    half = PAGE_SIZE // SLC_BLOCK
    slc_t = _feature_major(cache_slc)
    win_t = _feature_major(cache_win)

    def kv_spec(i):
        def index(b, idx_ref, pt_ref):
            return (pt_ref[b, idx_ref[b, i] // half], i // n_pick, 0, 0, 0)
        return pl.BlockSpec((1, 1, 2, NSA_DH, PAGE_SIZE), index)

    per_seq = lambda shape: pl.BlockSpec((1,) + shape, lambda b, *_: (b,) + (0,) * len(shape))
    grid_spec = pltpu.PrefetchScalarGridSpec(
        num_scalar_prefetch=2, grid=(nb,),
        in_specs=[kv_spec(i) for i in range(G * n_pick)] + [
            per_seq((NSA_HEADS, NSA_DH)), per_seq((1, NSA_KV_W)), per_seq((1, NSA_KV_W)),
            per_seq((G, 2, NSA_DH, win)), per_seq((1, _SMALL_W)), per_seq((NSA_HEADS, NSA_DH))],
        out_specs=per_seq((NSA_HEADS, NSA_DH)))
    on = pl.pallas_call(
        functools.partial(_slc_win_sample_kernel, t=t, win=win, n_pick=n_pick),
        out_shape=jax.ShapeDtypeStruct((nb, NSA_HEADS, NSA_DH), BF16), grid_spec=grid_spec,
        compiler_params=_cparams("parallel"), name="slc_win_sample",
    )(blk_idx, page_table, *([slc_t] * (G * n_pick)), q3,
      ks_new.reshape(nb, 1, NSA_KV_W), kw_new.reshape(nb, 1, NSA_KV_W),
      win_t, small.reshape(nb, 1, _SMALL_W), oc)
    return on.reshape(nb, NSA_HEADS * NSA_DH)


def _ffn_sample_kernel(x_ref, s0_ref, s1_ref, p_ref, gpre_ref, wup_ref, cw_ref, cb_ref, wdn_ref, gpost_ref,
                       gple_ref, wpg_ref, wpp_ref, y_ref, a_ref):
    x = x_ref[...]
    ab = _dot(_rms(x, gpre_ref[...]).astype(BF16), wup_ref[...])
    a = ab[:, 0:D_FF]
    cw = cw_ref[...]
    conv = (s0_ref[...] * cw[0:1] + s1_ref[...] * cw[1:2] + a * cw[2:3]) + cb_ref[...]
    y_ref[...] = _ffn_tail(x, conv, ab[:, D_FF:], p_ref[...], wdn_ref, gpost_ref, gple_ref, wpg_ref, wpp_ref)
    a_ref[...] = a


def _ffn_sample(x, conv_state, p, fw):
    nb = x.shape[0]
    full = lambda w: pl.BlockSpec((nb, w), lambda i: (0, 0))
    return pl.pallas_call(
        _ffn_sample_kernel,
        out_shape=(jax.ShapeDtypeStruct((nb, D_MODEL), F32), jax.ShapeDtypeStruct((nb, D_FF), F32)),
        grid=(1,),
        in_specs=[full(D_MODEL), full(D_FF), full(D_FF), full(PLE_DIM)] + _ffn_weight_specs(),
        out_specs=(full(D_MODEL), full(D_FF)),
        compiler_params=_cparams("arbitrary"), name="ffn_sample",
    )(x, conv_state[:, 0], conv_state[:, 1], p, *fw)


def _layer_sample(x, p, cache_cmp, cache_slc, cache_win, state_gla, state_conv, page_table, w):
    nb, s, _ = x.shape
    assert s == 1
    x2 = x.reshape(nb, D_MODEL)
    gla, small, nq, kc, ks, kw, mg = _proj_in(x2, w["g_mix_pre"], w["w_in_pack"], nb)
    og, gla_state = _gla_sample(gla, small, state_gla, w["w_gate_pack"], w["b_gla_gate"], w["g_gla_norm"])

    kvc = _compress(page_table, _feature_major(cache_cmp), *w["cmp"])
    on = _nsa_sample(nq, ks, kw, small, _permute_cmp(kvc), cache_slc, cache_win, page_table)

    x1 = _mix_out(og, on, mg, x2, w["w_gla_out"], w["w_nsa_out"], w["w_out"], w["g_mix_post"], nb)
    y, a = _ffn_sample(x1, state_conv, p.reshape(nb, PLE_DIM), w["ffn"])
    kv_shape = (nb, 1, NSA_KV_GROUPS, 2, NSA_DH)
    conv_new = jnp.stack([state_conv[:, 1], a], axis=1)
    return (y.reshape(nb, 1, D_MODEL), kc.reshape(kv_shape), ks.reshape(kv_shape), kw.reshape(kv_shape),
            gla_state, conv_new)


def _prep_weights(g_mix_pre, g_mix_post, g_ffn_pre, g_ffn_post, g_ple, w_in, w_gla_gate_up, b_gla_gate,
                  g_gla_norm, w_gla_out, w_cmp1, b_cmp1, w_cmp2, b_cmp2, pe_cmp, w_nsa_out, w_out, w_ffn_up,
                  conv_ffn_w, b_conv_ffn, w_ffn_down, w_ple_proj, w_ple_gate):
    pe_flat = pe_cmp.transpose(1, 0, 2).reshape(2, 1, CMP_LEN * NSA_DH)
    row = lambda v: v.reshape(1, -1)
    return dict(
        g_mix_pre=g_mix_pre, g_mix_post=g_mix_post, w_in_pack=_pack_w_in(w_in), w_kv_t=_pack_w_kv_t(w_in),
        w_gate_pack=_pack_w_gate(w_gla_gate_up), b_gla_gate=b_gla_gate, g_gla_norm=g_gla_norm,
        w_gla_out=w_gla_out.astype(BF16), w_nsa_out=w_nsa_out.astype(BF16), w_out=w_out.astype(BF16),
        cmp=(_pack_w_cmp1(w_cmp1), w_cmp1.astype(BF16), pe_flat, b_cmp1.reshape(2, 1, CMP_HIDDEN),
             w_cmp2.astype(BF16), b_cmp2.reshape(2, 1, NSA_DH)),
        ffn=(row(g_ffn_pre), w_ffn_up.astype(BF16), conv_ffn_w, row(b_conv_ffn), w_ffn_down.astype(BF16),
             row(g_ffn_post), row(g_ple), w_ple_gate.astype(BF16), w_ple_proj.astype(BF16)),
    )


def kernel(x_prompt, x_sample, cache_cmp_kv, cache_slc_kv, cache_win_kv, state_gla, state_ffn_conv, page_table,
           p_prompt, p_sample, g_mix_pre, g_mix_post, g_ffn_pre, g_ffn_post, g_ple, w_in, w_gla_gate_up,
           b_gla_gate, g_gla_norm, w_gla_out, w_cmp1, b_cmp1, w_cmp2, b_cmp2, pe_cmp, w_nsa_out, w_out, w_ffn_up,
           conv_ffn_w, b_conv_ffn, w_ffn_down, w_ple_proj, w_ple_gate):
    layer_weights = (g_mix_pre, g_mix_post, g_ffn_pre, g_ffn_post, g_ple, w_in, w_gla_gate_up, b_gla_gate,
                     g_gla_norm, w_gla_out, w_cmp1, b_cmp1, w_cmp2, b_cmp2, pe_cmp, w_nsa_out, w_out, w_ffn_up,
                     conv_ffn_w, b_conv_ffn, w_ffn_down, w_ple_proj, w_ple_gate)
    depth = w_in.shape[0]
    yp, ys = x_prompt, x_sample
    extras_p, extras_s = [], []
    for i in range(depth):
        w = _prep_weights(*[v[i] for v in layer_weights])
        yp, *ep = _layer_prompt(yp, p_prompt[i], w)
        ys, *es = _layer_sample(ys, p_sample[i], cache_cmp_kv[i], cache_slc_kv[i], cache_win_kv[i], state_gla[i],
                                state_ffn_conv[i], page_table, w)
        extras_p.append(ep)
        extras_s.append(es)
    stack = lambda rows, j: jnp.stack([r[j] for r in rows])
    outs = [yp, ys]
    for j in range(5):
        outs += [stack(extras_p, j), stack(extras_s, j)]
    return tuple(outs)
```
